```python
import jax, jax.numpy as jnp
from jax import lax
import numpy as np

D_MODEL = 2048
BATCH = 2
SEQ = 8192
DEPTH = 4

N_MIXERS = 3
HEAD_DIM = 64
N_HEADS = D_MODEL // HEAD_DIM
N_KV_HEADS = N_HEADS // 8
GROUP = N_HEADS // N_KV_HEADS
ATTN_WIDTH = N_HEADS * HEAD_DIM
KV_WIDTH = N_KV_HEADS * HEAD_DIM
QKV_WIDTH = ATTN_WIDTH + 2 * KV_WIDTH
ROPE_DIM = HEAD_DIM // 4
ROPE_THETA = 500000.0
SCALE = HEAD_DIM ** -0.5
BLOCK = 128
SWA_WINDOW = 128
IDX_HEADS = 16
IDX_DIM = 64
IDX_SCALE = IDX_DIM ** -0.5
IDX_W_SCALE = IDX_HEADS ** -0.5
TOPK_MAX = 256
DSA_IN_WIDTH = QKV_WIDTH + IDX_HEADS * IDX_DIM + IDX_DIM + IDX_HEADS
DILATED_BRANCHES = ((128, 1), (512, 4), (2048, 16))
MAX_DIL_WINDOW = 2048
D_FF = 4 * D_MODEL
NORM_EPS = 1e-5
NEG_INF = -1e30
N_LAYERS_A = (DEPTH + 2) // 3
N_LAYERS_B = (DEPTH + 1) // 3
N_LAYERS_C = DEPTH // 3

kernel_name = "hybrid_swa_dsa_dilated_trunk"


def rms_norm(x, g):
    xf = x.astype(jnp.float32)
    y = xf * lax.rsqrt(jnp.mean(xf * xf, axis=-1, keepdims=True) + NORM_EPS)
    return (y * g.astype(jnp.float32)).astype(x.dtype)


def rope_tables(positions):
    inv = ROPE_THETA ** (-jnp.arange(0, ROPE_DIM, 2, dtype=jnp.float32) / ROPE_DIM)
    ang = positions.astype(jnp.float32)[..., None] * inv
    return jnp.cos(ang)[:, :, None, :], jnp.sin(ang)[:, :, None, :]


def apply_rope(x, cos, sin):
    half = ROPE_DIM // 2
    x1 = x[..., :half].astype(jnp.float32)
    x2 = x[..., half:ROPE_DIM].astype(jnp.float32)
    rot = jnp.concatenate([x1 * cos - x2 * sin, x2 * cos + x1 * sin], axis=-1).astype(x.dtype)
    return jnp.concatenate([rot, x[..., ROPE_DIM:]], axis=-1)


def split_qkv(proj, cos, sin):
    B, T = proj.shape[:2]
    q = proj[..., :ATTN_WIDTH].reshape(B, T, N_HEADS, HEAD_DIM)
    k = proj[..., ATTN_WIDTH:ATTN_WIDTH + KV_WIDTH].reshape(B, T, N_KV_HEADS, HEAD_DIM)
    v = proj[..., ATTN_WIDTH + KV_WIDTH:QKV_WIDTH].reshape(B, T, N_KV_HEADS, HEAD_DIM)
    q = apply_rope(q, cos, sin).reshape(B, T, N_KV_HEADS, GROUP, HEAD_DIM)
    k = apply_rope(k, cos, sin)
    return q, k, v


def block_map(fn, batch, seq):
    starts = jnp.arange(seq // BLOCK, dtype=jnp.int32) * BLOCK
    out = lax.map(fn, starts)
    return jnp.swapaxes(out, 0, 1).reshape(batch, seq, out.shape[-1])


def sliding_window_attention(h, w_in, sinks, w_out, cos, sin):
    B, T, _ = h.shape
    q, k, v = split_qkv(h @ w_in, cos, sin)
    pad = ((0, 0), (SWA_WINDOW, 0), (0, 0), (0, 0))
    k_pad, v_pad = jnp.pad(k, pad), jnp.pad(v, pad)
    sink = sinks.astype(jnp.float32).reshape(N_KV_HEADS, GROUP, 1, 1)
    span = BLOCK + SWA_WINDOW

    def one_block(t0):
        qb = lax.dynamic_slice_in_dim(q, t0, BLOCK, axis=1)
        kb = lax.dynamic_slice_in_dim(k_pad, t0, span, axis=1)
        vb = lax.dynamic_slice_in_dim(v_pad, t0, span, axis=1)
        qpos = t0 + jnp.arange(BLOCK)
        kpos = t0 - SWA_WINDOW + jnp.arange(span)
        dist = qpos[:, None] - kpos[None, :]
        mask = (dist >= 0) & (dist < SWA_WINDOW) & (kpos[None, :] >= 0)
        s = jnp.einsum('bqkgd,bskd->bkgqs', qb, kb).astype(jnp.float32) * SCALE
        s = jnp.where(mask, s, NEG_INF)
        m = jnp.maximum(jnp.max(s, axis=-1, keepdims=True), sink)
        p = jnp.exp(s - m)
        den = jnp.sum(p, axis=-1, keepdims=True) + jnp.exp(sink - m)
        o = jnp.einsum('bkgqs,bskd->bqkgd', (p / den).astype(v.dtype), vb)
        return o.reshape(B, BLOCK, ATTN_WIDTH)

    return block_map(one_block, B, T) @ w_out


def dsa_attention(h, w_in, w_out, cos, sin):
    B, T, _ = h.shape
    proj = h @ w_in
    q, k, v = split_qkv(proj[..., :QKV_WIDTH], cos, sin)
    o = QKV_WIDTH
    qi = proj[..., o:o + IDX_HEADS * IDX_DIM].reshape(B, T, IDX_HEADS, IDX_DIM)
    o += IDX_HEADS * IDX_DIM
    ki = proj[..., o:o + IDX_DIM][:, :, None, :]
    o += IDX_DIM
    wi = proj[..., o:o + IDX_HEADS]
    qi = apply_rope(qi, cos, sin)
    ki = apply_rope(ki, cos, sin)[:, :, 0]
    topk = min(TOPK_MAX, T // 4)
    kpos = jnp.arange(T)

    def one_block(t0):
        qib = lax.dynamic_slice_in_dim(qi, t0, BLOCK, axis=1)
        wib = lax.dynamic_slice_in_dim(wi, t0, BLOCK, axis=1).astype(jnp.float32) * IDX_W_SCALE
        qb = lax.dynamic_slice_in_dim(q, t0, BLOCK, axis=1)
        qpos = t0 + jnp.arange(BLOCK)
        rel = jax.nn.relu(jnp.einsum('bqhd,bsd->bqhs', qib, ki).astype(jnp.float32) * IDX_SCALE)
        score = jnp.einsum('bqhs,bqh->bqs', rel, wib)
        score = jnp.where(kpos[None, None, :] <= qpos[None, :, None], score, NEG_INF)
        _, idx = lax.top_k(score, topk)
        valid = idx <= qpos[None, :, None]
        k_sel = jax.vmap(lambda kb, ib: kb[ib])(k, idx)
        v_sel = jax.vmap(lambda vb, ib: vb[ib])(v, idx)
        s = jnp.einsum('bqkgd,bqskd->bkgqs', qb, k_sel).astype(jnp.float32) * SCALE
        s = jnp.where(valid[:, None, None], s, NEG_INF)
        p = jax.nn.softmax(s, axis=-1)
        out = jnp.einsum('bkgqs,bqskd->bqkgd', p.astype(v.dtype), v_sel)
        return out.reshape(B, BLOCK, ATTN_WIDTH)

    return block_map(one_block, B, T) @ w_out


def dilated_attention(h, w_in, w_out, cos, sin):
    B, T, _ = h.shape
    q, k, v = split_qkv(h @ w_in, cos, sin)
    pad = ((0, 0), (MAX_DIL_WINDOW, 0), (0, 0), (0, 0))
    k_pad, v_pad = jnp.pad(k, pad), jnp.pad(v, pad)

    def one_block(t0):
        qb = lax.dynamic_slice_in_dim(q, t0, BLOCK, axis=1)
        qpos = t0 + jnp.arange(BLOCK)
        maxes, dens, nums = [], [], []
        for window, dil in DILATED_BRANCHES:
            steps = jnp.arange(window // dil + 1)
            kpos = qpos[:, None] - dil * steps[None, :]
            kb = k_pad[:, kpos + MAX_DIL_WINDOW]
            vb = v_pad[:, kpos + MAX_DIL_WINDOW]
            s = jnp.einsum('bqkgd,bqjkd->bkgqj', qb, kb).astype(jnp.float32) * SCALE
            s = jnp.where(kpos >= 0, s, NEG_INF)
            m = jnp.max(s, axis=-1, keepdims=True)
            e = jnp.exp(s - m)
            maxes.append(m)
            dens.append(jnp.sum(e, axis=-1, keepdims=True))
            nums.append(jnp.einsum('bkgqj,bqjkd->bkgqd', e, vb.astype(jnp.float32)))
        mx = jnp.max(jnp.stack(maxes), axis=0)
        c0, c1, c2 = (jnp.exp(m - mx) for m in maxes)
        num = c0 * nums[0] + c1 * nums[1] + c2 * nums[2]
        den = c0 * dens[0] + c1 * dens[1] + c2 * dens[2]
        o = (num / den).astype(v.dtype)
        return jnp.transpose(o, (0, 3, 1, 2, 4)).reshape(B, BLOCK, ATTN_WIDTH)

    return block_map(one_block, B, T) @ w_out


def sq_relu_mlp(h, w_up, w_down):
    return jnp.square(jax.nn.relu(h @ w_up)) @ w_down


def setup_inputs(seed: int = 0) -> dict:
    key = jax.random.key(seed)
    ks = jax.random.split(key, 13)
    nrm = jax.random.normal
    x = nrm(ks[0], (BATCH, SEQ, D_MODEL), jnp.float32)
    positions = jnp.broadcast_to(jnp.arange(SEQ, dtype=jnp.int32), (BATCH, SEQ))
    norm_attn = 1.0 + 0.02 * nrm(ks[1], (DEPTH, D_MODEL), jnp.float32)
    norm_mlp = 1.0 + 0.02 * nrm(ks[2], (DEPTH, D_MODEL), jnp.float32)
    w_up = nrm(ks[3], (DEPTH, D_MODEL, D_FF), jnp.float32) * D_MODEL ** -0.5
    w_down = nrm(ks[4], (DEPTH, D_FF, D_MODEL), jnp.float32) * D_FF ** -0.5
    final_norm = 1.0 + 0.02 * nrm(ks[5], (D_MODEL,), jnp.float32)
    a_w_in = nrm(ks[6], (N_LAYERS_A, D_MODEL, QKV_WIDTH), jnp.float32) * D_MODEL ** -0.5
    a_sinks = 0.5 * nrm(ks[7], (N_LAYERS_A, N_HEADS), jnp.float32)
    a_w_out = nrm(ks[8], (N_LAYERS_A, ATTN_WIDTH, D_MODEL), jnp.float32) * ATTN_WIDTH ** -0.5
    b_w_in = nrm(ks[9], (N_LAYERS_B, D_MODEL, DSA_IN_WIDTH), jnp.float32) * D_MODEL ** -0.5
    b_w_out = nrm(ks[10], (N_LAYERS_B, ATTN_WIDTH, D_MODEL), jnp.float32) * ATTN_WIDTH ** -0.5
    c_w_in = nrm(ks[11], (N_LAYERS_C, D_MODEL, QKV_WIDTH), jnp.float32) * D_MODEL ** -0.5
    c_w_out = nrm(ks[12], (N_LAYERS_C, ATTN_WIDTH, D_MODEL), jnp.float32) * ATTN_WIDTH ** -0.5
    return {"x": x, "positions": positions, "norm_attn": norm_attn, "norm_mlp": norm_mlp,
            "w_up": w_up, "w_down": w_down, "final_norm": final_norm,
            "a_w_in": a_w_in, "a_sinks": a_sinks, "a_w_out": a_w_out,
            "b_w_in": b_w_in, "b_w_out": b_w_out, "c_w_in": c_w_in, "c_w_out": c_w_out}


def reference(x, positions, norm_attn, norm_mlp, w_up, w_down, final_norm,
              a_w_in, a_sinks, a_w_out, b_w_in, b_w_out, c_w_in, c_w_out):
    cos, sin = rope_tables(positions)
    for i in range(DEPTH):
        j = i // N_MIXERS
        kind = i % N_MIXERS
        h = rms_norm(x, norm_attn[i])
        if kind == 0:
            mix = sliding_window_attention(h, a_w_in[j], a_sinks[j], a_w_out[j], cos, sin)
        elif kind == 1:
            mix = dsa_attention(h, b_w_in[j], b_w_out[j], cos, sin)
        else:
            mix = dilated_attention(h, c_w_in[j], c_w_out[j], cos, sin)
        x = x + mix
        h = rms_norm(x, norm_mlp[i])
        x = x + sq_relu_mlp(h, w_up[i], w_down[i])
    return rms_norm(x, final_norm)
```

```python
import functools

import jax
import jax.numpy as jnp
from jax import lax
from jax.experimental import pallas as pl
from jax.experimental.pallas import tpu as pltpu

HEAD_DIM = 64
N_KV_HEADS = 4
GROUP = 8
N_HEADS = N_KV_HEADS * GROUP
ATTN_WIDTH = N_HEADS * HEAD_DIM
KV_WIDTH = N_KV_HEADS * HEAD_DIM
ROPE_DIM = HEAD_DIM // 4
ROPE_HALF = ROPE_DIM // 2
ROPE_THETA = 500000.0
SCALE = HEAD_DIM ** -0.5
BLOCK = 128
SWA_WINDOW = 128
IDX_HEADS = 16
IDX_DIM = 64
IDX_SCALE = IDX_DIM ** -0.5
IDX_W_SCALE = IDX_HEADS ** -0.5
TOPK_MAX = 256
DILATED_BRANCHES = ((128, 1), (512, 4), (2048, 16))
MAX_DIL_WINDOW = 2048
NORM_EPS = 1e-5
NEG_INF = -1e30

LANES = 128
VMEM_LIMIT = 52 * 1024 * 1024

BF16 = jnp.bfloat16
F32 = jnp.float32


def _cparams(sem):
    return pltpu.CompilerParams(dimension_semantics=sem, vmem_limit_bytes=VMEM_LIMIT)


def _norm_proj_kernel(x_ref, g_ref, w_ref, cs_ref, rope_ref, o_ref, *rest,
                      n_rope_groups, groups_per_tile, aux_group):
    if aux_group is None:
        (h_ref,) = rest
        aux_ref = None
    else:
        aux_ref, h_ref = rest
    j = pl.program_id(1)

    @pl.when(j == 0)
    def _():
        x = x_ref[...]
        ms = jnp.mean(x * x, axis=-1, keepdims=True)
        h_ref[...] = ((x * lax.rsqrt(ms + NORM_EPS)) * g_ref[...]).astype(BF16)

    acc = jnp.dot(h_ref[...], w_ref[...], preferred_element_type=F32) * cs_ref[...]
    cos_t = rope_ref[:, 0:LANES]
    sin_lo = rope_ref[:, LANES:2 * LANES]
    sin_hi = rope_ref[:, 2 * LANES:3 * LANES]
    for g in range(groups_per_tile):
        a = acc[:, g * LANES:(g + 1) * LANES]
        is_rope = (j * groups_per_tile + g) < n_rope_groups

        @pl.when(is_rope)
        def _(a=a, g=g):
            r = (a * cos_t + pltpu.roll(a, LANES - ROPE_HALF, 1) * sin_lo
                 + pltpu.roll(a, ROPE_HALF, 1) * sin_hi)
            o_ref[:, g * LANES:(g + 1) * LANES] = r.astype(o_ref.dtype)

        @pl.when(jnp.logical_not(is_rope))
        def _(a=a, g=g):
            o_ref[:, g * LANES:(g + 1) * LANES] = a.astype(o_ref.dtype)

    if aux_ref is not None:
        @pl.when(j == pl.num_programs(1) - 1)
        def _():
            aux_ref[...] = acc[:, aux_group * LANES:(aux_group + 1) * LANES]


def _norm_proj(x2, g, w, colscale, rope_tab, *, tn, n_rope_groups, aux_group=None, tm=1024):
    n, d = x2.shape
    width = w.shape[1]
    gpt = tn // LANES
    kern = functools.partial(_norm_proj_kernel, n_rope_groups=n_rope_groups,
                             groups_per_tile=gpt, aux_group=aux_group)
    out_shape = [jax.ShapeDtypeStruct((n, width), BF16)]
    out_specs = [pl.BlockSpec((tm, tn), lambda i, j: (i, j))]
    if aux_group is not None:
        out_shape.append(jax.ShapeDtypeStruct((n, LANES), F32))
        out_specs.append(pl.BlockSpec((tm, LANES), lambda i, j: (i, 0)))
    res = pl.pallas_call(
        kern,
        out_shape=out_shape,
        grid=(n // tm, width // tn),
        in_specs=[
            pl.BlockSpec((tm, d), lambda i, j: (i, 0)),
            pl.BlockSpec((1, d), lambda i, j: (0, 0)),
            pl.BlockSpec((d, tn), lambda i, j: (0, j)),
            pl.BlockSpec((1, tn), lambda i, j: (0, j)),
            pl.BlockSpec((tm, 3 * LANES), lambda i, j: (i, 0)),
        ],
        out_specs=out_specs,
        scratch_shapes=[pltpu.VMEM((tm, d), BF16)],
        compiler_params=_cparams(("parallel", "arbitrary")),
        name="norm_proj",
    )(x2, g.reshape(1, d), w, colscale, rope_tab)
    return res


def _out_proj_kernel(o_ref, w_ref, x_ref, y_ref):
    y_ref[...] = x_ref[...] + jnp.dot(o_ref[...], w_ref[...], preferred_element_type=F32)


def _out_proj(o, w, x2, *, tm=1024, tn=512):
    n, k = o.shape
    d = w.shape[1]
    return pl.pallas_call(
        _out_proj_kernel,
        out_shape=jax.ShapeDtypeStruct((n, d), F32),
        grid=(n // tm, d // tn),
        in_specs=[
            pl.BlockSpec((tm, k), lambda i, j: (i, 0)),
            pl.BlockSpec((k, tn), lambda i, j: (0, j)),
            pl.BlockSpec((tm, tn), lambda i, j: (i, j)),
        ],
        out_specs=pl.BlockSpec((tm, tn), lambda i, j: (i, j)),
        compiler_params=_cparams(("parallel", "arbitrary")),
        name="out_proj",
    )(o, w, x2)


def _mlp_kernel(x_ref, g_ref, wu_ref, wd_ref, y_ref, h_ref, acc_ref):
    f = pl.program_id(1)

    @pl.when(f == 0)
    def _():
        x = x_ref[...]
        ms = jnp.mean(x * x, axis=-1, keepdims=True)
        h_ref[...] = ((x * lax.rsqrt(ms + NORM_EPS)) * g_ref[...]).astype(BF16)
        acc_ref[...] = jnp.zeros_like(acc_ref)

    u = jnp.dot(h_ref[...], wu_ref[...], preferred_element_type=F32)
    u = jnp.maximum(u, 0.0)
    a = (u * u).astype(BF16)
    acc_ref[...] += jnp.dot(a, wd_ref[...], preferred_element_type=F32)

    @pl.when(f == pl.num_programs(1) - 1)
    def _():
        y_ref[...] = x_ref[...] + acc_ref[...]


def _mlp(x2, g, w_up, w_down, *, tm=512, tf=512):
    n, d = x2.shape
    d_ff = w_up.shape[1]
    return pl.pallas_call(
        _mlp_kernel,
        out_shape=jax.ShapeDtypeStruct((n, d), F32),
        grid=(n // tm, d_ff // tf),
        in_specs=[
            pl.BlockSpec((tm, d), lambda i, f: (i, 0)),
            pl.BlockSpec((1, d), lambda i, f: (0, 0)),
            pl.BlockSpec((d, tf), lambda i, f: (0, f)),
            pl.BlockSpec((tf, d), lambda i, f: (f, 0)),
        ],
        out_specs=pl.BlockSpec((tm, d), lambda i, f: (i, 0)),
        scratch_shapes=[pltpu.VMEM((tm, d), BF16), pltpu.VMEM((tm, d), F32)],
        compiler_params=_cparams(("parallel", "arbitrary")),
        name="mlp",
    )(x2, g.reshape(1, d), w_up, w_down)


def _final_norm_kernel(x_ref, g_ref, y_ref):
    x = x_ref[...]
    ms = jnp.mean(x * x, axis=-1, keepdims=True)
    y_ref[...] = (x * lax.rsqrt(ms + NORM_EPS)) * g_ref[...]


def _final_norm(x2, g, *, tm=512):
    n, d = x2.shape
    return pl.pallas_call(
        _final_norm_kernel,
        out_shape=jax.ShapeDtypeStruct((n, d), F32),
        grid=(n // tm,),
        in_specs=[pl.BlockSpec((tm, d), lambda i: (i, 0)),
                  pl.BlockSpec((1, d), lambda i: (0, 0))],
        out_specs=pl.BlockSpec((tm, d), lambda i: (i, 0)),
        compiler_params=_cparams(("parallel",)),
        name="final_norm",
    )(x2, g.reshape(1, d))


def _stack_heads(q, kv):
    base = kv * GROUP * HEAD_DIM
    return jnp.concatenate(
        [q[:, base + g * HEAD_DIM: base + (g + 1) * HEAD_DIM] for g in range(GROUP)], axis=0)


def _qk(q, k):
    return lax.dot_general(q, k, (((1,), (1,)), ((), ())), preferred_element_type=F32)


def _tile_rows(m2d, reps):
    return jnp.concatenate([m2d] * reps, axis=0)


def _store_heads(o_ref, kv, o_stack):
    base = kv * GROUP * HEAD_DIM
    for g in range(GROUP):
        o_ref[:, base + g * HEAD_DIM: base + (g + 1) * HEAD_DIM] = (
            o_stack[g * BLOCK:(g + 1) * BLOCK, :].astype(o_ref.dtype))


def _swa_kernel(sink_ref, q_ref, kp_ref, kc_ref, vp_ref, vc_ref, o_ref):
    i = pl.program_id(1)
    q = q_ref[...]
    k2 = jnp.concatenate([kp_ref[...], kc_ref[...]], axis=0)
    v2 = jnp.concatenate([vp_ref[...], vc_ref[...]], axis=0)
    r = lax.broadcasted_iota(jnp.int32, (BLOCK, 2 * BLOCK), 0)
    c = lax.broadcasted_iota(jnp.int32, (BLOCK, 2 * BLOCK), 1)
    dist = r + SWA_WINDOW - c
    mask = (dist >= 0) & (dist < SWA_WINDOW) & ((c >= SWA_WINDOW) | (i > 0))
    mask8 = _tile_rows(mask, GROUP)
    for kv in range(N_KV_HEADS):
        qs = _stack_heads(q, kv)
        kh = k2[:, kv * HEAD_DIM:(kv + 1) * HEAD_DIM]
        vh = v2[:, kv * HEAD_DIM:(kv + 1) * HEAD_DIM]
        sink = jnp.concatenate(
            [jnp.full((BLOCK, 1), sink_ref[kv * GROUP + g], F32) for g in range(GROUP)], axis=0)
        s = jnp.where(mask8, _qk(qs, kh), NEG_INF)
        m = jnp.maximum(jnp.max(s, axis=-1, keepdims=True), sink)
        p = jnp.exp(s - m)
        den = jnp.sum(p, axis=-1, keepdims=True) + jnp.exp(sink - m)
        o = jnp.dot(p.astype(BF16), vh, preferred_element_type=F32) / den
        _store_heads(o_ref, kv, o)


def _swa_attention(proj, sinks, batch, seq):
    nb = seq // BLOCK
    n = batch * seq
    kcol = ATTN_WIDTH // KV_WIDTH
    return pl.pallas_call(
        _swa_kernel,
        out_shape=jax.ShapeDtypeStruct((n, ATTN_WIDTH), BF16),
        grid=(batch, nb),
        in_specs=[
            pl.BlockSpec(memory_space=pltpu.SMEM),
            pl.BlockSpec((BLOCK, ATTN_WIDTH), lambda b, i: (b * nb + i, 0)),
            pl.BlockSpec((BLOCK, KV_WIDTH), lambda b, i: (b * nb + jnp.maximum(i - 1, 0), kcol)),
            pl.BlockSpec((BLOCK, KV_WIDTH), lambda b, i: (b * nb + i, kcol)),
            pl.BlockSpec((BLOCK, KV_WIDTH), lambda b, i: (b * nb + jnp.maximum(i - 1, 0), kcol + 1)),
            pl.BlockSpec((BLOCK, KV_WIDTH), lambda b, i: (b * nb + i, kcol + 1)),
        ],
        out_specs=pl.BlockSpec((BLOCK, ATTN_WIDTH), lambda b, i: (b * nb + i, 0)),
        compiler_params=_cparams(("parallel", "arbitrary")),
        name="swa_attention",
    )(sinks, proj, proj, proj, proj, proj)


def _dilated_kernel(q_ref, k_ref, v_ref, o_ref):
    i = pl.program_id(1)
    q = q_ref[...]
    r = lax.broadcasted_iota(jnp.int32, (BLOCK, BLOCK), 0)
    c = lax.broadcasted_iota(jnp.int32, (BLOCK, BLOCK), 1)
    n_back = jnp.minimum(i, MAX_DIL_WINDOW // BLOCK) + 1

    for kv in range(N_KV_HEADS):
        qs = _stack_heads(q, kv)

        def body(delta, carry, kv=kv, qs=qs):
            m, l, acc = carry
            start = pl.multiple_of((i - delta) * BLOCK, BLOCK)
            kh = k_ref[pl.ds(start, BLOCK), kv * HEAD_DIM:(kv + 1) * HEAD_DIM]
            vh = v_ref[pl.ds(start, BLOCK), kv * HEAD_DIM:(kv + 1) * HEAD_DIM]
            dist = delta * BLOCK + r - c
            mult = jnp.zeros((BLOCK, BLOCK), F32)
            for window, dil in DILATED_BRANCHES:
                hit = (dist >= 0) & (dist <= window) & ((dist & (dil - 1)) == 0)
                mult = mult + jnp.where(hit, 1.0, 0.0)
            mult8 = _tile_rows(mult, GROUP)
            s = jnp.where(mult8 > 0.0, _qk(qs, kh), NEG_INF)
            m_new = jnp.maximum(m, jnp.max(s, axis=-1, keepdims=True))
            alpha = jnp.exp(m - m_new)
            p = mult8 * jnp.exp(s - m_new)
            l = alpha * l + jnp.sum(p, axis=-1, keepdims=True)
            acc = alpha * acc + jnp.dot(p.astype(BF16), vh, preferred_element_type=F32)
            return m_new, l, acc

        init = (jnp.full((GROUP * BLOCK, 1), NEG_INF, F32),
                jnp.zeros((GROUP * BLOCK, 1), F32),
                jnp.zeros((GROUP * BLOCK, HEAD_DIM), F32))
        m, l, acc = lax.fori_loop(0, n_back, body, init)
        _store_heads(o_ref, kv, acc / l)


def _dilated_attention(proj, batch, seq):
    nb = seq // BLOCK
    n = batch * seq
    kcol = ATTN_WIDTH // KV_WIDTH
    return pl.pallas_call(
        _dilated_kernel,
        out_shape=jax.ShapeDtypeStruct((n, ATTN_WIDTH), BF16),
        grid=(batch, nb),
        in_specs=[
            pl.BlockSpec((BLOCK, ATTN_WIDTH), lambda b, i: (b * nb + i, 0)),
            pl.BlockSpec((seq, KV_WIDTH), lambda b, i: (b, kcol)),
            pl.BlockSpec((seq, KV_WIDTH), lambda b, i: (b, kcol + 1)),
        ],
        out_specs=pl.BlockSpec((BLOCK, ATTN_WIDTH), lambda b, i: (b * nb + i, 0)),
        compiler_params=_cparams(("parallel", "arbitrary")),
        name="dilated_attention",
    )(proj, proj, proj)


IDX_CHUNK = 512
ATT_CHUNK = 256
INT_MIN = -2 ** 31


def _sortable_to_f32(t):
    bits = jnp.where(t >= 0, t, t ^ jnp.int32(0x7FFFFFFF))
    return lax.bitcast_convert_type(bits, F32)


def _dsa_kernel(q_ref, qi_ref, wi_ref, ki_ref, k_ref, v_ref, o_ref, sc_ref, *, topk):
    i = pl.program_id(1)
    n_idx = (i * BLOCK + BLOCK + IDX_CHUNK - 1) // IDX_CHUNK
    n_att = (i * BLOCK + BLOCK + ATT_CHUNK - 1) // ATT_CHUNK
    qpos = i * BLOCK + lax.broadcasted_iota(jnp.int32, (BLOCK, 1), 0)

    qi = qi_ref[...]
    qis = jnp.concatenate(
        [qi[:, h * IDX_DIM:(h + 1) * IDX_DIM] for h in range(IDX_HEADS)], axis=0)
    wi = wi_ref[...] * IDX_W_SCALE

    def idx_body(cidx, carry):
        start = pl.multiple_of(cidx * IDX_CHUNK, IDX_CHUNK)
        kic = ki_ref[pl.ds(start, IDX_CHUNK), 0:IDX_DIM]
        rel = jnp.maximum(_qk(qis, kic), 0.0)
        score = jnp.zeros((BLOCK, IDX_CHUNK), F32)
        for h in range(IDX_HEADS):
            score = score + rel[h * BLOCK:(h + 1) * BLOCK, :] * wi[:, h:h + 1]
        kpos = start + lax.broadcasted_iota(jnp.int32, (BLOCK, IDX_CHUNK), 1)
        sc_ref[:, pl.ds(start, IDX_CHUNK)] = jnp.where(kpos <= qpos, score, NEG_INF)
        return carry

    lax.fori_loop(0, n_idx, idx_body, 0)

    def count_ge(cand_f):
        def cbody(cidx, cnt):
            start = pl.multiple_of(cidx * IDX_CHUNK, IDX_CHUNK)
            blk = sc_ref[:, pl.ds(start, IDX_CHUNK)]
            hit = jnp.where(blk >= cand_f, 1.0, 0.0)
            for t in range(IDX_CHUNK // LANES):
                cnt = cnt + hit[:, t * LANES:(t + 1) * LANES]
            return cnt
        cnt = lax.fori_loop(0, n_idx, cbody, jnp.zeros((BLOCK, LANES), F32))
        return jnp.sum(cnt, axis=-1, keepdims=True)

    def bit_body(b, t):
        bit = lax.shift_left(jnp.int32(1), 31 - b)
        cand = jnp.where(b == 0, jnp.zeros_like(t), t | bit)
        ok = count_ge(_sortable_to_f32(cand)) >= float(topk)
        return jnp.where(ok, cand, t)

    t_int = lax.fori_loop(0, 32, bit_body, jnp.full((BLOCK, 1), INT_MIN, jnp.int32))
    thr = jnp.maximum(_sortable_to_f32(t_int), jnp.float32(NEG_INF * 0.5))

    q = q_ref[...]
    for kv in range(N_KV_HEADS):
        qs = _stack_heads(q, kv)

        def att_body(cidx, carry, kv=kv, qs=qs):
            m, l, acc = carry
            start = pl.multiple_of(cidx * ATT_CHUNK, ATT_CHUNK)
            kh = k_ref[pl.ds(start, ATT_CHUNK), kv * HEAD_DIM:(kv + 1) * HEAD_DIM]
            vh = v_ref[pl.ds(start, ATT_CHUNK), kv * HEAD_DIM:(kv + 1) * HEAD_DIM]
            sel = sc_ref[:, pl.ds(start, ATT_CHUNK)] >= thr
            sel8 = _tile_rows(sel, GROUP)
            s = jnp.where(sel8, _qk(qs, kh), NEG_INF)
            m_new = jnp.maximum(m, jnp.max(s, axis=-1, keepdims=True))
            alpha = jnp.exp(m - m_new)
            p = jnp.where(sel8, jnp.exp(s - m_new), 0.0)
            l = alpha * l + jnp.sum(p, axis=-1, keepdims=True)
            acc = alpha * acc + jnp.dot(p.astype(BF16), vh, preferred_element_type=F32)
            return m_new, l, acc

        init = (jnp.full((GROUP * BLOCK, 1), NEG_INF, F32),
                jnp.zeros((GROUP * BLOCK, 1), F32),
                jnp.zeros((GROUP * BLOCK, HEAD_DIM), F32))
        m, l, acc = lax.fori_loop(0, n_att, att_body, init)
        _store_heads(o_ref, kv, acc / l)


DSA_QI_OFF = ATTN_WIDTH
DSA_K_OFF = DSA_QI_OFF + IDX_HEADS * IDX_DIM
DSA_KI_OFF = DSA_K_OFF + KV_WIDTH
DSA_WI_OFF = DSA_KI_OFF + LANES
DSA_V_OFF = DSA_WI_OFF + LANES
DSA_WIDTH = DSA_V_OFF + KV_WIDTH


def _dsa_attention(proj, wi, batch, seq):
    nb = seq // BLOCK
    n = batch * seq
    topk = min(TOPK_MAX, seq // 4)
    qi_w = IDX_HEADS * IDX_DIM
    seq_pad = -(-seq // IDX_CHUNK) * IDX_CHUNK
    return pl.pallas_call(
        functools.partial(_dsa_kernel, topk=topk),
        out_shape=jax.ShapeDtypeStruct((n, ATTN_WIDTH), BF16),
        grid=(batch, nb),
        in_specs=[
            pl.BlockSpec((BLOCK, ATTN_WIDTH), lambda b, i: (b * nb + i, 0)),
            pl.BlockSpec((BLOCK, qi_w), lambda b, i: (b * nb + i, DSA_QI_OFF // qi_w)),
            pl.BlockSpec((BLOCK, LANES), lambda b, i: (b * nb + i, 0)),
            pl.BlockSpec((seq, LANES), lambda b, i: (b, DSA_KI_OFF // LANES)),
            pl.BlockSpec((seq, KV_WIDTH), lambda b, i: (b, DSA_K_OFF // KV_WIDTH)),
            pl.BlockSpec((seq, KV_WIDTH), lambda b, i: (b, DSA_V_OFF // KV_WIDTH)),
        ],
        out_specs=pl.BlockSpec((BLOCK, ATTN_WIDTH), lambda b, i: (b * nb + i, 0)),
        scratch_shapes=[pltpu.VMEM((BLOCK, seq_pad), F32)],
        compiler_params=_cparams(("parallel", "arbitrary")),
        name="dsa_attention",
    )(proj, proj, wi, proj, proj, proj)


def _rope_table(positions):
    inv = ROPE_THETA ** (-jnp.arange(0, ROPE_DIM, 2, dtype=F32) / ROPE_DIM)
    ang = positions.astype(F32).reshape(-1, 1) * inv[None, :]
    cos, sin = jnp.cos(ang), jnp.sin(ang)
    n = ang.shape[0]
    pad = HEAD_DIM - ROPE_DIM
    cos_h = jnp.concatenate([cos, cos, jnp.ones((n, pad), F32)], axis=1)
    lo_h = jnp.concatenate([-sin, jnp.zeros((n, HEAD_DIM - ROPE_HALF), F32)], axis=1)
    hi_h = jnp.concatenate([jnp.zeros((n, ROPE_HALF), F32), sin, jnp.zeros((n, pad), F32)], axis=1)
    reps = LANES // HEAD_DIM
    return jnp.concatenate([jnp.tile(cos_h, (1, reps)), jnp.tile(lo_h, (1, reps)),
                            jnp.tile(hi_h, (1, reps))], axis=1)


def _qkv_colscale():
    return jnp.concatenate([jnp.full((1, ATTN_WIDTH), SCALE, F32),
                            jnp.ones((1, 2 * KV_WIDTH), F32)], axis=1)


def _dsa_weight(w_in):
    d = w_in.shape[0]
    o = ATTN_WIDTH
    wq = w_in[:, :o]
    wk = w_in[:, o:o + KV_WIDTH]
    wv = w_in[:, o + KV_WIDTH:o + 2 * KV_WIDTH]
    o += 2 * KV_WIDTH
    wqi = w_in[:, o:o + IDX_HEADS * IDX_DIM]
    o += IDX_HEADS * IDX_DIM
    wki = w_in[:, o:o + IDX_DIM]
    o += IDX_DIM
    wwi = w_in[:, o:o + IDX_HEADS]
    z = lambda c: jnp.zeros((d, c), w_in.dtype)
    return jnp.concatenate([wq, wqi, wk, wki, z(LANES - IDX_DIM), wwi, z(LANES - IDX_HEADS), wv], axis=1)


def _dsa_colscale():
    return jnp.concatenate([jnp.full((1, ATTN_WIDTH), SCALE, F32),
                            jnp.full((1, IDX_HEADS * IDX_DIM), IDX_SCALE, F32),
                            jnp.ones((1, DSA_WIDTH - DSA_K_OFF), F32)], axis=1)


def kernel(x, positions, norm_attn, norm_mlp, w_up, w_down, final_norm,
           a_w_in, a_sinks, a_w_out, b_w_in, b_w_out, c_w_in, c_w_out):
    batch, seq, d = x.shape
    depth = norm_attn.shape[0]
    x2 = x.reshape(batch * seq, d)
    rope_tab = _rope_table(positions)
    qkv_scale = _qkv_colscale()
    qkv_rope_groups = (ATTN_WIDTH + KV_WIDTH) // LANES
    for i in range(depth):
        j, kind = divmod(i, 3)
        if kind == 0:
            proj = _norm_proj(x2, norm_attn[i], a_w_in[j].astype(BF16), qkv_scale, rope_tab,
                              tn=512, n_rope_groups=qkv_rope_groups)[0]
            o = _swa_attention(proj, a_sinks[j], batch, seq)
            w_out = a_w_out[j]
        elif kind == 1:
            proj, wi = _norm_proj(x2, norm_attn[i], _dsa_weight(b_w_in[j]).astype(BF16),
                                  _dsa_colscale(), rope_tab, tn=768,
                                  n_rope_groups=DSA_WI_OFF // LANES,
                                  aux_group=(DSA_WI_OFF % 768) // LANES)
            o = _dsa_attention(proj, wi, batch, seq)
            w_out = b_w_out[j]
        else:
            proj = _norm_proj(x2, norm_attn[i], c_w_in[j].astype(BF16), qkv_scale, rope_tab,
                              tn=512, n_rope_groups=qkv_rope_groups)[0]
            o = _dilated_attention(proj, batch, seq)
            w_out = c_w_out[j]
        x2 = _out_proj(o, w_out.astype(BF16), x2)
        x2 = _mlp(x2, norm_mlp[i], w_up[i].astype(BF16), w_down[i].astype(BF16))
    return _final_norm(x2, final_norm).reshape(batch, seq, d)
```

```python
import functools

import jax
import jax.numpy as jnp
from jax import lax
from jax.experimental import pallas as pl
from jax.experimental.pallas import tpu as pltpu

HEAD_DIM = 64
N_KV_HEADS = 4
GROUP = 8
N_HEADS = N_KV_HEADS * GROUP
ATTN_WIDTH = N_HEADS * HEAD_DIM
KV_WIDTH = N_KV_HEADS * HEAD_DIM
ROPE_DIM = HEAD_DIM // 4
ROPE_HALF = ROPE_DIM // 2
ROPE_THETA = 500000.0
SCALE = HEAD_DIM ** -0.5
BLOCK = 128
SWA_WINDOW = 128
IDX_HEADS = 16
IDX_DIM = 64
IDX_SCALE = IDX_DIM ** -0.5
IDX_W_SCALE = IDX_HEADS ** -0.5
TOPK_MAX = 256
DILATED_BRANCHES = ((128, 1), (512, 4), (2048, 16))
MAX_DIL_WINDOW = 2048
NORM_EPS = 1e-5
NEG_INF = -1e30

LANES = 128
BF16_ROWS = 16
VMEM_LIMIT = 52 * 1024 * 1024

BF16 = jnp.bfloat16
F32 = jnp.float32

KDUP_WIDTH = N_KV_HEADS * LANES
QROWS = GROUP * BLOCK
EXT = 2 * LANES
VT_ROWS = HEAD_DIM + BF16_ROWS


def _cparams(sem):
    return pltpu.CompilerParams(dimension_semantics=sem, vmem_limit_bytes=VMEM_LIMIT)


def _norm_proj_kernel(x_ref, g_ref, w_ref, cs_ref, rope_ref, o_ref, *rest,
                      n_rope_groups, groups_per_tile, aux_group):
    if aux_group is None:
        (h_ref,) = rest
        aux_ref = None
    else:
        aux_ref, h_ref = rest
    j = pl.program_id(1)

    @pl.when(j == 0)
    def _():
        x = x_ref[...]
        ms = jnp.mean(x * x, axis=-1, keepdims=True)
        h_ref[...] = ((x * lax.rsqrt(ms + NORM_EPS)) * g_ref[...]).astype(BF16)

    acc = jnp.dot(h_ref[...], w_ref[...], preferred_element_type=F32) * cs_ref[...]
    cos_t = rope_ref[:, 0:LANES]
    sin_lo = rope_ref[:, LANES:2 * LANES]
    sin_hi = rope_ref[:, 2 * LANES:3 * LANES]
    for g in range(groups_per_tile):
        a = acc[:, g * LANES:(g + 1) * LANES]
        is_rope = (j * groups_per_tile + g) < n_rope_groups

        @pl.when(is_rope)
        def _(a=a, g=g):
            r = (a * cos_t + pltpu.roll(a, LANES - ROPE_HALF, 1) * sin_lo
                 + pltpu.roll(a, ROPE_HALF, 1) * sin_hi)
            o_ref[:, g * LANES:(g + 1) * LANES] = r.astype(o_ref.dtype)

        @pl.when(jnp.logical_not(is_rope))
        def _(a=a, g=g):
            o_ref[:, g * LANES:(g + 1) * LANES] = a.astype(o_ref.dtype)

    if aux_ref is not None:
        @pl.when(j == pl.num_programs(1) - 1)
        def _():
            aux_ref[...] = acc[:, aux_group * LANES:(aux_group + 1) * LANES]


def _norm_proj(x2, g, w, colscale, rope_tab, *, tn, n_rope_groups, aux_group=None, tm=512):
    n, d = x2.shape
    width = w.shape[1]
    gpt = tn // LANES
    kern = functools.partial(_norm_proj_kernel, n_rope_groups=n_rope_groups,
                             groups_per_tile=gpt, aux_group=aux_group)
    out_shape = [jax.ShapeDtypeStruct((n, width), BF16)]
    out_specs = [pl.BlockSpec((tm, tn), lambda i, j: (i, j))]
    if aux_group is not None:
        out_shape.append(jax.ShapeDtypeStruct((n, LANES), F32))
        out_specs.append(pl.BlockSpec((tm, LANES), lambda i, j: (i, 0)))
    res = pl.pallas_call(
        kern,
        out_shape=out_shape,
        grid=(n // tm, width // tn),
        in_specs=[
            pl.BlockSpec((tm, d), lambda i, j: (i, 0)),
            pl.BlockSpec((1, d), lambda i, j: (0, 0)),
            pl.BlockSpec((d, tn), lambda i, j: (0, j)),
            pl.BlockSpec((1, tn), lambda i, j: (0, j)),
            pl.BlockSpec((tm, 3 * LANES), lambda i, j: (i, 0)),
        ],
        out_specs=out_specs,
        scratch_shapes=[pltpu.VMEM((tm, d), BF16)],
        compiler_params=_cparams(("parallel", "arbitrary")),
        name="norm_proj",
    )(x2, g.reshape(1, d), w, colscale, rope_tab)
    return res


def _out_proj_kernel(o_ref, w_ref, x_ref, y_ref):
    y_ref[...] = x_ref[...] + jnp.dot(o_ref[...], w_ref[...], preferred_element_type=F32)


def _out_proj(o, w, x2, *, tm=1024, tn=512):
    n, k = o.shape
    d = w.shape[1]
    return pl.pallas_call(
        _out_proj_kernel,
        out_shape=jax.ShapeDtypeStruct((n, d), F32),
        grid=(n // tm, d // tn),
        in_specs=[
            pl.BlockSpec((tm, k), lambda i, j: (i, 0)),
            pl.BlockSpec((k, tn), lambda i, j: (0, j)),
            pl.BlockSpec((tm, tn), lambda i, j: (i, j)),
        ],
        out_specs=pl.BlockSpec((tm, tn), lambda i, j: (i, j)),
        compiler_params=_cparams(("parallel", "arbitrary")),
        name="out_proj",
    )(o, w, x2)


def _mlp_kernel(x_ref, g_ref, wu_ref, wd_ref, y_ref, h_ref, acc_ref):
    f = pl.program_id(1)

    @pl.when(f == 0)
    def _():
        x = x_ref[...]
        ms = jnp.mean(x * x, axis=-1, keepdims=True)
        h_ref[...] = ((x * lax.rsqrt(ms + NORM_EPS)) * g_ref[...]).astype(BF16)
        acc_ref[...] = jnp.zeros_like(acc_ref)

    u = jnp.dot(h_ref[...], wu_ref[...], preferred_element_type=F32)
    u = jnp.maximum(u, 0.0)
    a = (u * u).astype(BF16)
    acc_ref[...] += jnp.dot(a, wd_ref[...], preferred_element_type=F32)

    @pl.when(f == pl.num_programs(1) - 1)
    def _():
        y_ref[...] = x_ref[...] + acc_ref[...]


def _mlp(x2, g, w_up, w_down, *, tm=512, tf=512):
    n, d = x2.shape
    d_ff = w_up.shape[1]
    return pl.pallas_call(
        _mlp_kernel,
        out_shape=jax.ShapeDtypeStruct((n, d), F32),
        grid=(n // tm, d_ff // tf),
        in_specs=[
            pl.BlockSpec((tm, d), lambda i, f: (i, 0)),
            pl.BlockSpec((1, d), lambda i, f: (0, 0)),
            pl.BlockSpec((d, tf), lambda i, f: (0, f)),
            pl.BlockSpec((tf, d), lambda i, f: (f, 0)),
        ],
        out_specs=pl.BlockSpec((tm, d), lambda i, f: (i, 0)),
        scratch_shapes=[pltpu.VMEM((tm, d), BF16), pltpu.VMEM((tm, d), F32)],
        compiler_params=_cparams(("parallel", "arbitrary")),
        name="mlp",
    )(x2, g.reshape(1, d), w_up, w_down)


def _final_norm_kernel(x_ref, g_ref, y_ref):
    x = x_ref[...]
    ms = jnp.mean(x * x, axis=-1, keepdims=True)
    y_ref[...] = (x * lax.rsqrt(ms + NORM_EPS)) * g_ref[...]


def _final_norm(x2, g, *, tm=512):
    n, d = x2.shape
    return pl.pallas_call(
        _final_norm_kernel,
        out_shape=jax.ShapeDtypeStruct((n, d), F32),
        grid=(n // tm,),
        in_specs=[pl.BlockSpec((tm, d), lambda i: (i, 0)),
                  pl.BlockSpec((1, d), lambda i: (0, 0))],
        out_specs=pl.BlockSpec((tm, d), lambda i: (i, 0)),
        compiler_params=_cparams(("parallel",)),
        name="final_norm",
    )(x2, g.reshape(1, d))


def _fill_q_ext(q_ref, qx_ref):
    lane = lax.broadcasted_iota(jnp.int32, (BLOCK, LANES), 1)
    row = lax.broadcasted_iota(jnp.int32, (BLOCK, LANES), 0)
    eye = jnp.where(lane == row, 1.0, 0.0).astype(BF16)
    low = lane < HEAD_DIM
    for kv in range(N_KV_HEADS):
        for g in range(GROUP):
            h = kv * GROUP + g
            tile = q_ref[:, (h // 2) * LANES:(h // 2 + 1) * LANES]
            keep = low if h % 2 == 0 else jnp.logical_not(low)
            qx_ref[kv, g * BLOCK:(g + 1) * BLOCK, 0:LANES] = jnp.where(keep, tile, jnp.zeros_like(tile))
            qx_ref[kv, g * BLOCK:(g + 1) * BLOCK, LANES:EXT] = eye


def _scores_t(k_tile, bias, qx):
    k_ext = jnp.concatenate([k_tile, bias], axis=1)
    return lax.dot_general(k_ext, qx, (((1,), (1,)), ((), ())), preferred_element_type=F32)


def _vt_ext(vt):
    return jnp.concatenate([vt, jnp.ones((BF16_ROWS, vt.shape[1]), BF16)], axis=0)


MXU_TILE = 256


def _flash_chunks(c_lo, c_end, tc, make_bias, k_ref, vt_ref, refs, use_mult):
    qx_ref, s_ref, cm_ref, m_ref, acc_ref, bias_ref, mult_ref = refs
    m_ref[...] = jnp.full(m_ref.shape, NEG_INF, F32)
    acc_ref[...] = jnp.zeros(acc_ref.shape, F32)
    c_last = c_end - 1

    def put_bias(c):
        bias, mult = make_bias(c)
        bias_ref[c & 1] = bias
        if use_mult:
            mult_ref[c & 1] = mult

    def issue_scores(c, kv):
        start = pl.multiple_of(c * tc, tc)
        k_tile = k_ref[pl.ds(start, tc), kv * LANES:(kv + 1) * LANES]
        s_t = _scores_t(k_tile, bias_ref[c & 1], qx_ref[kv])
        s_ref[kv % 2] = s_t
        cm_ref[kv % 2] = jnp.max(s_t, axis=0, keepdims=True)

    def consume(c, kv):
        slot = kv % 2
        start = pl.multiple_of(c * tc, tc)
        vt = _vt_ext(vt_ref[kv * HEAD_DIM:(kv + 1) * HEAD_DIM, pl.ds(start, tc)])
        m_old = m_ref[kv]
        m_new = jnp.maximum(m_old, cm_ref[slot])
        alpha = jnp.exp(m_old - m_new)
        m_ref[kv] = m_new
        for n in range(QROWS // MXU_TILE):
            cols = slice(n * MXU_TILE, (n + 1) * MXU_TILE)
            part = alpha[:, cols] * acc_ref[kv, :, cols]
            for kk in range(tc // MXU_TILE):
                rows = slice(kk * MXU_TILE, (kk + 1) * MXU_TILE)
                p = jnp.exp(s_ref[slot, rows, cols] - m_new[:, cols])
                if use_mult:
                    mult = mult_ref[c & 1, rows, :]
                    p = p * jnp.concatenate([mult] * (MXU_TILE // BLOCK), axis=1)
                part = part + jnp.dot(vt[:, rows], p.astype(BF16), preferred_element_type=F32)
            acc_ref[kv, :, cols] = part

    put_bias(c_lo)
    issue_scores(c_lo, 0)

    def body(c, carry):
        c_next = jnp.minimum(c + 1, c_last)
        put_bias(c_next)
        for kv in range(N_KV_HEADS):
            if kv + 1 < N_KV_HEADS:
                issue_scores(c, kv + 1)
            else:
                issue_scores(c_next, 0)
            consume(c, kv)
        return carry

    lax.fori_loop(c_lo, c_end, body, 0)


def _store_out(o_ref, kv, o_t):
    for gp in range(GROUP // 2):
        pair = jnp.concatenate([o_t[:, (2 * gp) * BLOCK:(2 * gp + 1) * BLOCK],
                                o_t[:, (2 * gp + 1) * BLOCK:(2 * gp + 2) * BLOCK]], axis=0)
        col = (kv * GROUP + 2 * gp) * HEAD_DIM
        o_ref[:, col:col + LANES] = pair.T.astype(o_ref.dtype)


def _finish_flash(o_ref, acc_ref):
    for kv in range(N_KV_HEADS):
        acc = acc_ref[kv]
        _store_out(o_ref, kv, acc[0:HEAD_DIM, :] / acc[HEAD_DIM:HEAD_DIM + 1, :])


def _bias_of(valid):
    return jnp.where(valid, 0.0, NEG_INF).astype(BF16)


def _qx_scratch():
    return pltpu.VMEM((N_KV_HEADS, QROWS, EXT), BF16)


def _flash_scratch(tc, use_mult):
    return [
        _qx_scratch(),
        pltpu.VMEM((2, tc, QROWS), F32),
        pltpu.VMEM((2, 1, QROWS), F32),
        pltpu.VMEM((N_KV_HEADS, 1, QROWS), F32),
        pltpu.VMEM((N_KV_HEADS, VT_ROWS, QROWS), F32),
        pltpu.VMEM((2, tc, BLOCK), BF16),
        pltpu.VMEM((2, tc, BLOCK) if use_mult else (2, SUBLANES, BLOCK), F32),
    ]


def _swa_kernel(sink_ref, q_ref, kp_ref, kc_ref, vp_ref, vc_ref, o_ref, qx_ref):
    i = pl.program_id(1)
    _fill_q_ext(q_ref, qx_ref)
    key = lax.broadcasted_iota(jnp.int32, (2 * BLOCK, BLOCK), 0)
    qry = lax.broadcasted_iota(jnp.int32, (2 * BLOCK, BLOCK), 1)
    dist = qry + SWA_WINDOW - key
    bias = _bias_of((dist >= 0) & (dist < SWA_WINDOW) & ((key >= SWA_WINDOW) | (i > 0)))
    vt2 = _vt_ext(jnp.concatenate([vp_ref[...], vc_ref[...]], axis=1))
    for kv in range(N_KV_HEADS):
        k_tile = jnp.concatenate([kp_ref[:, kv * LANES:(kv + 1) * LANES],
                                  kc_ref[:, kv * LANES:(kv + 1) * LANES]], axis=0)
        s_t = _scores_t(k_tile, bias, qx_ref[kv])
        sink = jnp.concatenate(
            [jnp.full((1, BLOCK), sink_ref[kv * GROUP + g], F32) for g in range(GROUP)], axis=1)
        m = jnp.maximum(jnp.max(s_t, axis=0, keepdims=True), sink)
        p = jnp.exp(s_t - m).astype(BF16)
        vt = jnp.concatenate([vt2[kv * HEAD_DIM:(kv + 1) * HEAD_DIM, :], vt2[KV_WIDTH:, :]], axis=0)
        acc = jnp.dot(vt, p, preferred_element_type=F32)
        den = acc[HEAD_DIM:HEAD_DIM + 1, :] + jnp.exp(sink - m)
        _store_out(o_ref, kv, acc[0:HEAD_DIM, :] / den)


def _swa_attention(proj, vt, sinks, batch, seq):
    nb = seq // BLOCK
    n = batch * seq
    kcol = ATTN_WIDTH // KDUP_WIDTH
    return pl.pallas_call(
        _swa_kernel,
        out_shape=jax.ShapeDtypeStruct((n, ATTN_WIDTH), BF16),
        grid=(batch, nb),
        in_specs=[
            pl.BlockSpec(memory_space=pltpu.SMEM),
            pl.BlockSpec((BLOCK, ATTN_WIDTH), lambda b, i: (b * nb + i, 0)),
            pl.BlockSpec((BLOCK, KDUP_WIDTH), lambda b, i: (b * nb + jnp.maximum(i - 1, 0), kcol)),
            pl.BlockSpec((BLOCK, KDUP_WIDTH), lambda b, i: (b * nb + i, kcol)),
            pl.BlockSpec((None, KV_WIDTH, BLOCK), lambda b, i: (b, 0, jnp.maximum(i - 1, 0))),
            pl.BlockSpec((None, KV_WIDTH, BLOCK), lambda b, i: (b, 0, i)),
        ],
        out_specs=pl.BlockSpec((BLOCK, ATTN_WIDTH), lambda b, i: (b * nb + i, 0)),
        scratch_shapes=[_qx_scratch()],
        compiler_params=_cparams(("parallel", "arbitrary")),
        name="swa_attention",
    )(sinks, proj, proj, proj, vt, vt)


DIL_CHUNK = 256


def _dilated_kernel(q_ref, k_ref, vt_ref, o_ref, *refs):
    i = pl.program_id(1)
    qx_ref, acc_ref = refs[0], refs[4]
    _fill_q_ext(q_ref, qx_ref)
    t0 = i * BLOCK
    c_end = (t0 + BLOCK - 1) // DIL_CHUNK + 1
    c_lo = jnp.maximum(t0 - MAX_DIL_WINDOW, 0) // DIL_CHUNK
    key = lax.broadcasted_iota(jnp.int32, (DIL_CHUNK, BLOCK), 0)
    qry = lax.broadcasted_iota(jnp.int32, (DIL_CHUNK, BLOCK), 1)

    def make_bias(c):
        dist = (t0 - c * DIL_CHUNK) + qry - key
        mult = jnp.zeros((DIL_CHUNK, BLOCK), F32)
        for window, dil in DILATED_BRANCHES:
            hit = (dist >= 0) & (dist <= window) & ((dist & (dil - 1)) == 0)
            mult = mult + jnp.where(hit, 1.0, 0.0)
        return _bias_of(mult > 0.0), mult

    _flash_chunks(c_lo, c_end, DIL_CHUNK, make_bias, k_ref, vt_ref, refs, True)
    _finish_flash(o_ref, acc_ref)


def _dilated_attention(proj, vt, batch, seq):
    nb = seq // BLOCK
    n = batch * seq
    kcol = ATTN_WIDTH // KDUP_WIDTH
    return pl.pallas_call(
        _dilated_kernel,
        out_shape=jax.ShapeDtypeStruct((n, ATTN_WIDTH), BF16),
        grid=(batch, nb),
        in_specs=[
            pl.BlockSpec((BLOCK, ATTN_WIDTH), lambda b, i: (b * nb + i, 0)),
            pl.BlockSpec((seq, KDUP_WIDTH), lambda b, i: (b, kcol)),
            pl.BlockSpec((None, KV_WIDTH, seq), lambda b, i: (b, 0, 0)),
        ],
        out_specs=pl.BlockSpec((BLOCK, ATTN_WIDTH), lambda b, i: (b * nb + i, 0)),
        scratch_shapes=_flash_scratch(DIL_CHUNK, True),
        compiler_params=_cparams(("parallel", "arbitrary")),
        name="dilated_attention",
    )(proj, proj, vt)


IDX_CHUNK = 512
ATT_CHUNK = 512
INT_MIN = -2 ** 31
F32_BITS = 32
SUBLANES = 8


def _sortable_to_f32(t):
    bits = jnp.where(t >= 0, t, t ^ jnp.int32(0x7FFFFFFF))
    return lax.bitcast_convert_type(bits, F32)


def _dsa_kernel(q_ref, qi_ref, wi_ref, ki_ref, k_ref, vt_ref, o_ref, sc_ref, *refs, topk):
    qx_ref, acc_ref = refs[0], refs[4]
    i = pl.program_id(1)
    t0 = i * BLOCK
    n_idx = (t0 + BLOCK + IDX_CHUNK - 1) // IDX_CHUNK
    n_att = (t0 + BLOCK + ATT_CHUNK - 1) // ATT_CHUNK

    qis = jnp.concatenate(
        [qi_ref[:, h * IDX_DIM:(h + 1) * IDX_DIM] for h in range(IDX_HEADS)], axis=0)
    w_t = (wi_ref[...] * IDX_W_SCALE).T
    key = lax.broadcasted_iota(jnp.int32, (IDX_CHUNK, BLOCK), 0)
    qry = lax.broadcasted_iota(jnp.int32, (IDX_CHUNK, BLOCK), 1)

    def idx_body(c, carry):
        start = pl.multiple_of(c * IDX_CHUNK, IDX_CHUNK)
        kic = ki_ref[pl.ds(start, IDX_CHUNK), 0:IDX_DIM]
        rel = jnp.maximum(
            lax.dot_general(kic, qis, (((1,), (1,)), ((), ())), preferred_element_type=F32), 0.0)
        score = jnp.zeros((IDX_CHUNK, BLOCK), F32)
        for h in range(IDX_HEADS):
            score = score + rel[:, h * BLOCK:(h + 1) * BLOCK] * w_t[h:h + 1, :]
        sc_ref[pl.ds(start, IDX_CHUNK), :] = jnp.where(start + key <= t0 + qry, score, NEG_INF)
        return carry

    lax.fori_loop(0, n_idx, idx_body, 0)

    def count_ge(cand_f):
        def cbody(c, cnt):
            start = pl.multiple_of(c * IDX_CHUNK, IDX_CHUNK)
            hit = jnp.where(sc_ref[pl.ds(start, IDX_CHUNK), :] >= cand_f, 1.0, 0.0)
            parts = [hit[r * SUBLANES:(r + 1) * SUBLANES, :] for r in range(IDX_CHUNK // SUBLANES)]
            while len(parts) > 1:
                parts = [a + b for a, b in zip(parts[0::2], parts[1::2])]
            return cnt + parts[0]
        cnt = lax.fori_loop(0, n_idx, cbody, jnp.zeros((SUBLANES, BLOCK), F32))
        return jnp.sum(cnt, axis=0, keepdims=True)

    def bit_cond(state):
        b, _, _, n_open = state
        return (b < F32_BITS) & (n_open > 0)

    def bit_body(state):
        b, t, done, _ = state
        bit = lax.shift_left(jnp.int32(1), F32_BITS - 1 - b)
        cand = jnp.where(b == 0, jnp.zeros_like(t), t | bit)
        cnt = count_ge(_sortable_to_f32(cand))
        take = (cnt >= float(topk)) & (done == 0)
        t = jnp.where(take, cand, t)
        done = jnp.where(take & (cnt == float(topk)), 1, done)
        n_open = jnp.sum(1 - done)
        return b + 1, t, done, n_open

    state = (jnp.int32(0), jnp.full((1, BLOCK), INT_MIN, jnp.int32),
             jnp.zeros((1, BLOCK), jnp.int32), jnp.int32(BLOCK))
    _, t_int, _, _ = lax.while_loop(bit_cond, bit_body, state)
    thr = jnp.maximum(_sortable_to_f32(t_int), jnp.float32(NEG_INF * 0.5))

    _fill_q_ext(q_ref, qx_ref)

    def make_bias(c):
        start = pl.multiple_of(c * ATT_CHUNK, ATT_CHUNK)
        return _bias_of(sc_ref[pl.ds(start, ATT_CHUNK), :] >= thr), None

    _flash_chunks(0, n_att, ATT_CHUNK, make_bias, k_ref, vt_ref, refs, False)
    _finish_flash(o_ref, acc_ref)


DSA_QI_OFF = ATTN_WIDTH
DSA_K_OFF = DSA_QI_OFF + IDX_HEADS * IDX_DIM
DSA_KI_OFF = DSA_K_OFF + KDUP_WIDTH
DSA_WI_OFF = DSA_KI_OFF + LANES
DSA_V_OFF = DSA_WI_OFF + LANES
DSA_WIDTH = DSA_V_OFF + KV_WIDTH
DSA_TN = 1024


def _dsa_attention(proj, wi, vt, batch, seq):
    nb = seq // BLOCK
    n = batch * seq
    topk = min(TOPK_MAX, seq // 4)
    qi_w = IDX_HEADS * IDX_DIM
    chunk = max(IDX_CHUNK, ATT_CHUNK)
    seq_pad = -(-seq // chunk) * chunk
    return pl.pallas_call(
        functools.partial(_dsa_kernel, topk=topk),
        out_shape=jax.ShapeDtypeStruct((n, ATTN_WIDTH), BF16),
        grid=(batch, nb),
        in_specs=[
            pl.BlockSpec((BLOCK, ATTN_WIDTH), lambda b, i: (b * nb + i, 0)),
            pl.BlockSpec((BLOCK, qi_w), lambda b, i: (b * nb + i, DSA_QI_OFF // qi_w)),
            pl.BlockSpec((BLOCK, LANES), lambda b, i: (b * nb + i, 0)),
            pl.BlockSpec((seq, LANES), lambda b, i: (b, DSA_KI_OFF // LANES)),
            pl.BlockSpec((seq, KDUP_WIDTH), lambda b, i: (b, DSA_K_OFF // KDUP_WIDTH)),
            pl.BlockSpec((None, KV_WIDTH, seq), lambda b, i: (b, 0, 0)),
        ],
        out_specs=pl.BlockSpec((BLOCK, ATTN_WIDTH), lambda b, i: (b * nb + i, 0)),
        scratch_shapes=[pltpu.VMEM((seq_pad, BLOCK), F32)] + _flash_scratch(ATT_CHUNK, False),
        compiler_params=_cparams(("parallel", "arbitrary")),
        name="dsa_attention",
    )(proj, proj, wi, proj, proj, vt)


def _rope_table(positions):
    inv = ROPE_THETA ** (-jnp.arange(0, ROPE_DIM, 2, dtype=F32) / ROPE_DIM)
    ang = positions.astype(F32).reshape(-1, 1) * inv[None, :]
    cos, sin = jnp.cos(ang), jnp.sin(ang)
    n = ang.shape[0]
    pad = HEAD_DIM - ROPE_DIM
    cos_h = jnp.concatenate([cos, cos, jnp.ones((n, pad), F32)], axis=1)
    lo_h = jnp.concatenate([-sin, jnp.zeros((n, HEAD_DIM - ROPE_HALF), F32)], axis=1)
    hi_h = jnp.concatenate([jnp.zeros((n, ROPE_HALF), F32), sin, jnp.zeros((n, pad), F32)], axis=1)
    reps = LANES // HEAD_DIM
    return jnp.concatenate([jnp.tile(cos_h, (1, reps)), jnp.tile(lo_h, (1, reps)),
                            jnp.tile(hi_h, (1, reps))], axis=1)


def _dup_heads(wk):
    d = wk.shape[0]
    w4 = wk.reshape(d, N_KV_HEADS, 1, HEAD_DIM)
    return jnp.broadcast_to(w4, (d, N_KV_HEADS, LANES // HEAD_DIM, HEAD_DIM)).reshape(d, KDUP_WIDTH)


QKV_WIDTH_EXT = ATTN_WIDTH + KDUP_WIDTH + KV_WIDTH
QKV_TN = QKV_WIDTH_EXT // 2


def _qkv_weight(w_in):
    o = ATTN_WIDTH
    return jnp.concatenate([w_in[:, :o], _dup_heads(w_in[:, o:o + KV_WIDTH]),
                            w_in[:, o + KV_WIDTH:o + 2 * KV_WIDTH]], axis=1)


def _qkv_colscale():
    return jnp.concatenate([jnp.full((1, ATTN_WIDTH), SCALE, F32),
                            jnp.ones((1, KDUP_WIDTH + KV_WIDTH), F32)], axis=1)


def _dsa_weight(w_in):
    d = w_in.shape[0]
    o = ATTN_WIDTH
    wq = w_in[:, :o]
    wk = w_in[:, o:o + KV_WIDTH]
    wv = w_in[:, o + KV_WIDTH:o + 2 * KV_WIDTH]
    o += 2 * KV_WIDTH
    wqi = w_in[:, o:o + IDX_HEADS * IDX_DIM]
    o += IDX_HEADS * IDX_DIM
    wki = w_in[:, o:o + IDX_DIM]
    o += IDX_DIM
    wwi = w_in[:, o:o + IDX_HEADS]
    z = lambda c: jnp.zeros((d, c), w_in.dtype)
    return jnp.concatenate([wq, wqi, _dup_heads(wk), wki, z(LANES - IDX_DIM),
                            wwi, z(LANES - IDX_HEADS), wv], axis=1)


def _dsa_colscale():
    return jnp.concatenate([jnp.full((1, ATTN_WIDTH), SCALE, F32),
                            jnp.full((1, IDX_HEADS * IDX_DIM), IDX_SCALE, F32),
                            jnp.ones((1, DSA_WIDTH - DSA_K_OFF), F32)], axis=1)


def _v_transposed(proj, v_off, batch, seq):
    v = proj[:, v_off:v_off + KV_WIDTH].reshape(batch, seq, KV_WIDTH)
    return jnp.swapaxes(v, 1, 2)


def kernel(x, positions, norm_attn, norm_mlp, w_up, w_down, final_norm,
           a_w_in, a_sinks, a_w_out, b_w_in, b_w_out, c_w_in, c_w_out):
    batch, seq, d = x.shape
    depth = norm_attn.shape[0]
    x2 = x.reshape(batch * seq, d)
    rope_tab = _rope_table(positions)
    qkv_scale = _qkv_colscale()
    qkv_rope_groups = (ATTN_WIDTH + KDUP_WIDTH) // LANES
    qkv_v_off = ATTN_WIDTH + KDUP_WIDTH
    for i in range(depth):
        j, kind = divmod(i, 3)
        if kind == 1:
            proj, wi = _norm_proj(x2, norm_attn[i], _dsa_weight(b_w_in[j]).astype(BF16),
                                  _dsa_colscale(), rope_tab, tn=DSA_TN,
                                  n_rope_groups=DSA_WI_OFF // LANES,
                                  aux_group=(DSA_WI_OFF % DSA_TN) // LANES)
            o = _dsa_attention(proj, wi, _v_transposed(proj, DSA_V_OFF, batch, seq), batch, seq)
            w_out = b_w_out[j]
        else:
            w_in = a_w_in[j] if kind == 0 else c_w_in[j]
            proj = _norm_proj(x2, norm_attn[i], _qkv_weight(w_in).astype(BF16), qkv_scale, rope_tab,
                              tn=QKV_TN, n_rope_groups=qkv_rope_groups)[0]
            vt = _v_transposed(proj, qkv_v_off, batch, seq)
            if kind == 0:
                o = _swa_attention(proj, vt, a_sinks[j], batch, seq)
                w_out = a_w_out[j]
            else:
                o = _dilated_attention(proj, vt, batch, seq)
                w_out = c_w_out[j]
        x2 = _out_proj(o, w_out.astype(BF16), x2)
        x2 = _mlp(x2, norm_mlp[i], w_up[i].astype(BF16), w_down[i].astype(BF16))
    return _final_norm(x2, final_norm).reshape(batch, seq, d)
```

```python
import functools

import jax
import jax.numpy as jnp
from jax import lax
from jax.experimental import pallas as pl
from jax.experimental.pallas import tpu as pltpu

HEAD_DIM = 64
N_KV_HEADS = 4
GROUP = 8
N_HEADS = N_KV_HEADS * GROUP
ATTN_WIDTH = N_HEADS * HEAD_DIM
KV_WIDTH = N_KV_HEADS * HEAD_DIM
ROPE_DIM = HEAD_DIM // 4
ROPE_HALF = ROPE_DIM // 2
ROPE_THETA = 500000.0
SCALE = HEAD_DIM ** -0.5
BLOCK = 128
SWA_WINDOW = 128
IDX_HEADS = 16
IDX_DIM = 64
IDX_SCALE = IDX_DIM ** -0.5
IDX_W_SCALE = IDX_HEADS ** -0.5
TOPK_MAX = 256
DILATED_BRANCHES = ((128, 1), (512, 4), (2048, 16))
MAX_DIL_WINDOW = 2048
NORM_EPS = 1e-5
NEG_INF = -1e30

LANES = 128
BF16_ROWS = 16
SUBLANES = 8
MXU_TILE = 256
LOG2E = 1.4426950408889634
VMEM_LIMIT = 52 * 1024 * 1024

BF16 = jnp.bfloat16
F32 = jnp.float32

KDUP_WIDTH = N_KV_HEADS * LANES
QROWS = GROUP * BLOCK
EXT = 2 * LANES
VT_ROWS = HEAD_DIM + BF16_ROWS


def _cparams(sem):
    return pltpu.CompilerParams(dimension_semantics=sem, vmem_limit_bytes=VMEM_LIMIT)


def _norm_proj_kernel(x_ref, g_ref, w_ref, cs_ref, rope_ref, o_ref, *rest,
                      n_rope_groups, groups_per_tile, aux_group):
    if aux_group is None:
        (h_ref,) = rest
        aux_ref = None
    else:
        aux_ref, h_ref = rest
    j = pl.program_id(1)

    @pl.when(j == 0)
    def _():
        x = x_ref[...]
        ms = jnp.mean(x * x, axis=-1, keepdims=True)
        h_ref[...] = ((x * lax.rsqrt(ms + NORM_EPS)) * g_ref[...]).astype(BF16)

    cos_t = rope_ref[:, 0:LANES]
    sin_lo = rope_ref[:, LANES:2 * LANES]
    sin_hi = rope_ref[:, 2 * LANES:3 * LANES]
    h = h_ref[...]
    sub_groups = MXU_TILE // LANES
    n_sub = groups_per_tile // sub_groups

    def project(s):
        cols = slice(s * MXU_TILE, (s + 1) * MXU_TILE)
        return jnp.dot(h, w_ref[:, cols], preferred_element_type=F32) * cs_ref[:, cols]

    def finish(s, acc):
        for gg in range(sub_groups):
            g = s * sub_groups + gg
            a = acc[:, gg * LANES:(gg + 1) * LANES]
            r = (a * cos_t + pltpu.roll(a, LANES - ROPE_HALF, 1) * sin_lo
                 + pltpu.roll(a, ROPE_HALF, 1) * sin_hi)
            is_rope = (j * groups_per_tile + g) < n_rope_groups
            o_ref[:, g * LANES:(g + 1) * LANES] = jnp.where(is_rope, r, a).astype(o_ref.dtype)
            if aux_ref is not None and g == aux_group:
                aux_ref[...] = a

    acc = project(0)
    for s in range(1, n_sub):
        nxt = project(s)
        finish(s - 1, acc)
        acc = nxt
    finish(n_sub - 1, acc)


def _norm_proj(x2, g, w, colscale, rope_tab, *, tn, n_rope_groups, aux_group=None, tm=512):
    n, d = x2.shape
    width = w.shape[1]
    gpt = tn // LANES
    kern = functools.partial(_norm_proj_kernel, n_rope_groups=n_rope_groups,
                             groups_per_tile=gpt, aux_group=aux_group)
    out_shape = [jax.ShapeDtypeStruct((n, width), BF16)]
    out_specs = [pl.BlockSpec((tm, tn), lambda i, j: (i, j))]
    if aux_group is not None:
        out_shape.append(jax.ShapeDtypeStruct((n, LANES), F32))
        out_specs.append(pl.BlockSpec((tm, LANES), lambda i, j: (i, 0)))
    res = pl.pallas_call(
        kern,
        out_shape=out_shape,
        grid=(n // tm, width // tn),
        in_specs=[
            pl.BlockSpec((tm, d), lambda i, j: (i, 0)),
            pl.BlockSpec((1, d), lambda i, j: (0, 0)),
            pl.BlockSpec((d, tn), lambda i, j: (0, j)),
            pl.BlockSpec((1, tn), lambda i, j: (0, j)),
            pl.BlockSpec((tm, 3 * LANES), lambda i, j: (i, 0)),
        ],
        out_specs=out_specs,
        scratch_shapes=[pltpu.VMEM((tm, d), BF16)],
        compiler_params=_cparams(("parallel", "arbitrary")),
        name="norm_proj",
    )(x2, g.reshape(1, d), w, colscale, rope_tab)
    return res


def _out_proj_kernel(o_ref, w_ref, x_ref, y_ref):
    y_ref[...] = x_ref[...] + jnp.dot(o_ref[...], w_ref[...], preferred_element_type=F32)


def _out_proj(o, w, x2, *, tm=1024, tn=512):
    n, k = o.shape
    d = w.shape[1]
    return pl.pallas_call(
        _out_proj_kernel,
        out_shape=jax.ShapeDtypeStruct((n, d), F32),
        grid=(n // tm, d // tn),
        in_specs=[
            pl.BlockSpec((tm, k), lambda i, j: (i, 0)),
            pl.BlockSpec((k, tn), lambda i, j: (0, j)),
            pl.BlockSpec((tm, tn), lambda i, j: (i, j)),
        ],
        out_specs=pl.BlockSpec((tm, tn), lambda i, j: (i, j)),
        compiler_params=_cparams(("parallel", "arbitrary")),
        name="out_proj",
    )(o, w, x2)


def _mlp_kernel(x_ref, g_ref, wu_ref, wd_ref, y_ref, h_ref):
    f = pl.program_id(1)

    @pl.when(f == 0)
    def _():
        x = x_ref[...]
        ms = jnp.mean(x * x, axis=-1, keepdims=True)
        h_ref[...] = ((x * lax.rsqrt(ms + NORM_EPS)) * g_ref[...]).astype(BF16)
        y_ref[...] = x

    u = jnp.dot(h_ref[...], wu_ref[...], preferred_element_type=F32)
    u = jnp.maximum(u, 0.0)
    a = (u * u).astype(BF16)
    y_ref[...] += jnp.dot(a, wd_ref[...], preferred_element_type=F32)


def _mlp(x2, g, w_up, w_down, *, tm=512, tf=1024):
    n, d = x2.shape
    d_ff = w_up.shape[1]
    return pl.pallas_call(
        _mlp_kernel,
        out_shape=jax.ShapeDtypeStruct((n, d), F32),
        grid=(n // tm, d_ff // tf),
        in_specs=[
            pl.BlockSpec((tm, d), lambda i, f: (i, 0)),
            pl.BlockSpec((1, d), lambda i, f: (0, 0)),
            pl.BlockSpec((d, tf), lambda i, f: (0, f)),
            pl.BlockSpec((tf, d), lambda i, f: (f, 0)),
        ],
        out_specs=pl.BlockSpec((tm, d), lambda i, f: (i, 0)),
        scratch_shapes=[pltpu.VMEM((tm, d), BF16)],
        compiler_params=_cparams(("parallel", "arbitrary")),
        name="mlp",
    )(x2, g.reshape(1, d), w_up, w_down)


def _final_norm_kernel(x_ref, g_ref, y_ref):
    x = x_ref[...]
    ms = jnp.mean(x * x, axis=-1, keepdims=True)
    y_ref[...] = (x * lax.rsqrt(ms + NORM_EPS)) * g_ref[...]


def _final_norm(x2, g, *, tm=512):
    n, d = x2.shape
    return pl.pallas_call(
        _final_norm_kernel,
        out_shape=jax.ShapeDtypeStruct((n, d), F32),
        grid=(n // tm,),
        in_specs=[pl.BlockSpec((tm, d), lambda i: (i, 0)),
                  pl.BlockSpec((1, d), lambda i: (0, 0))],
        out_specs=pl.BlockSpec((tm, d), lambda i: (i, 0)),
        compiler_params=_cparams(("parallel",)),
        name="final_norm",
    )(x2, g.reshape(1, d))


def _fill_q_ext(q_ref, qx_ref):
    lane = lax.broadcasted_iota(jnp.int32, (BLOCK, LANES), 1)
    row = lax.broadcasted_iota(jnp.int32, (BLOCK, LANES), 0)
    eye = jnp.where(lane == row, 1.0, 0.0).astype(BF16)
    low = lane < HEAD_DIM
    for kv in range(N_KV_HEADS):
        for g in range(GROUP):
            h = kv * GROUP + g
            tile = q_ref[:, (h // 2) * LANES:(h // 2 + 1) * LANES]
            keep = low if h % 2 == 0 else jnp.logical_not(low)
            qx_ref[kv, g * BLOCK:(g + 1) * BLOCK, 0:LANES] = jnp.where(keep, tile, jnp.zeros_like(tile))
            qx_ref[kv, g * BLOCK:(g + 1) * BLOCK, LANES:EXT] = eye


def _scores_t(k_tile, bias, qx):
    k_ext = jnp.concatenate([k_tile, bias], axis=1)
    return lax.dot_general(k_ext, qx, (((1,), (1,)), ((), ())), preferred_element_type=F32)


def _vt_ext(vt):
    return jnp.concatenate([vt, jnp.ones((BF16_ROWS, vt.shape[1]), BF16)], axis=0)


def _flash_chunks(c_lo, c_end, tc, make_bias, k_ref, vt_ref, refs, use_mult, init=True):
    qx_ref, s_ref, cm_ref, m_ref, acc_ref, bias_ref, mult_ref = refs
    if init:
        m_ref[...] = jnp.full(m_ref.shape, NEG_INF, F32)
        acc_ref[...] = jnp.zeros(acc_ref.shape, F32)
    c_last = c_end - 1

    def put_bias(c):
        bias, mult = make_bias(c)
        bias_ref[c & 1] = bias
        if use_mult:
            mult_ref[c & 1] = mult

    def issue_scores(c, kv):
        start = pl.multiple_of(c * tc, tc)
        k_tile = k_ref[pl.ds(start, tc), kv * LANES:(kv + 1) * LANES]
        s_t = _scores_t(k_tile, bias_ref[c & 1], qx_ref[kv])
        s_ref[kv % 2] = s_t
        cm_ref[kv % 2] = jnp.max(s_t, axis=0, keepdims=True)

    def consume(c, kv):
        slot = kv % 2
        start = pl.multiple_of(c * tc, tc)
        vt = _vt_ext(vt_ref[kv * HEAD_DIM:(kv + 1) * HEAD_DIM, pl.ds(start, tc)])
        m_old = m_ref[kv]
        m_new = jnp.maximum(m_old, cm_ref[slot])
        alpha = jnp.exp2(m_old - m_new)
        m_ref[kv] = m_new
        for n in range(QROWS // MXU_TILE):
            cols = slice(n * MXU_TILE, (n + 1) * MXU_TILE)
            part = alpha[:, cols] * acc_ref[kv, :, cols]
            for kk in range(tc // MXU_TILE):
                rows = slice(kk * MXU_TILE, (kk + 1) * MXU_TILE)
                p = jnp.exp2(s_ref[slot, rows, cols] - m_new[:, cols])
                if use_mult:
                    mult = mult_ref[c & 1, rows, :]
                    p = p * jnp.concatenate([mult] * (MXU_TILE // BLOCK), axis=1)
                part = part + jnp.dot(vt[:, rows], p.astype(BF16), preferred_element_type=F32)
            acc_ref[kv, :, cols] = part

    put_bias(c_lo)
    issue_scores(c_lo, 0)

    def body(c, carry):
        c_next = jnp.minimum(c + 1, c_last)
        put_bias(c_next)
        for kv in range(N_KV_HEADS):
            if kv + 1 < N_KV_HEADS:
                issue_scores(c, kv + 1)
            else:
                issue_scores(c_next, 0)
            consume(c, kv)
        return carry

    lax.fori_loop(c_lo, c_end, body, 0)


def _store_out(o_ref, kv, o_t):
    for gp in range(GROUP // 2):
        pair = jnp.concatenate([o_t[:, (2 * gp) * BLOCK:(2 * gp + 1) * BLOCK],
                                o_t[:, (2 * gp + 1) * BLOCK:(2 * gp + 2) * BLOCK]], axis=0)
        col = (kv * GROUP + 2 * gp) * HEAD_DIM
        o_ref[:, col:col + LANES] = pair.T.astype(o_ref.dtype)


def _finish_flash(o_ref, acc_ref):
    for kv in range(N_KV_HEADS):
        acc = acc_ref[kv]
        _store_out(o_ref, kv, acc[0:HEAD_DIM, :] / acc[HEAD_DIM:HEAD_DIM + 1, :])


def _bias_of(valid):
    return jnp.where(valid, 0.0, NEG_INF).astype(BF16)


def _qx_scratch():
    return pltpu.VMEM((N_KV_HEADS, QROWS, EXT), BF16)


def _flash_scratch(tc, use_mult):
    return [
        _qx_scratch(),
        pltpu.VMEM((2, tc, QROWS), F32),
        pltpu.VMEM((2, 1, QROWS), F32),
        pltpu.VMEM((N_KV_HEADS, 1, QROWS), F32),
        pltpu.VMEM((N_KV_HEADS, VT_ROWS, QROWS), F32),
        pltpu.VMEM((2, tc, BLOCK), BF16),
        pltpu.VMEM((2, tc, BLOCK) if use_mult else (2, SUBLANES, BLOCK), F32),
    ]


def _swa_kernel(sink_ref, q_ref, kp_ref, kc_ref, vp_ref, vc_ref, o_ref, qx_ref):
    i = pl.program_id(1)
    _fill_q_ext(q_ref, qx_ref)
    key = lax.broadcasted_iota(jnp.int32, (2 * BLOCK, BLOCK), 0)
    qry = lax.broadcasted_iota(jnp.int32, (2 * BLOCK, BLOCK), 1)
    dist = qry + SWA_WINDOW - key
    bias = _bias_of((dist >= 0) & (dist < SWA_WINDOW) & ((key >= SWA_WINDOW) | (i > 0)))
    vt2 = _vt_ext(jnp.concatenate([vp_ref[...], vc_ref[...]], axis=1))
    scores = []
    for kv in range(N_KV_HEADS):
        k_tile = jnp.concatenate([kp_ref[:, kv * LANES:(kv + 1) * LANES],
                                  kc_ref[:, kv * LANES:(kv + 1) * LANES]], axis=0)
        scores.append(_scores_t(k_tile, bias, qx_ref[kv]))
    for kv in range(N_KV_HEADS):
        s_t = scores[kv]
        sink = jnp.concatenate(
            [jnp.full((1, BLOCK), sink_ref[kv * GROUP + g] * LOG2E, F32) for g in range(GROUP)], axis=1)
        m = jnp.maximum(jnp.max(s_t, axis=0, keepdims=True), sink)
        p = jnp.exp2(s_t - m).astype(BF16)
        vt = jnp.concatenate([vt2[kv * HEAD_DIM:(kv + 1) * HEAD_DIM, :], vt2[KV_WIDTH:, :]], axis=0)
        acc = jnp.dot(vt, p, preferred_element_type=F32)
        den = acc[HEAD_DIM:HEAD_DIM + 1, :] + jnp.exp2(sink - m)
        _store_out(o_ref, kv, acc[0:HEAD_DIM, :] / den)


def _swa_attention(proj, vt, sinks, batch, seq):
    nb = seq // BLOCK
    n = batch * seq
    kcol = ATTN_WIDTH // KDUP_WIDTH
    return pl.pallas_call(
        _swa_kernel,
        out_shape=jax.ShapeDtypeStruct((n, ATTN_WIDTH), BF16),
        grid=(batch, nb),
        in_specs=[
            pl.BlockSpec(memory_space=pltpu.SMEM),
            pl.BlockSpec((BLOCK, ATTN_WIDTH), lambda b, i: (b * nb + i, 0)),
            pl.BlockSpec((BLOCK, KDUP_WIDTH), lambda b, i: (b * nb + jnp.maximum(i - 1, 0), kcol)),
            pl.BlockSpec((BLOCK, KDUP_WIDTH), lambda b, i: (b * nb + i, kcol)),
            pl.BlockSpec((None, KV_WIDTH, BLOCK), lambda b, i: (b, 0, jnp.maximum(i - 1, 0))),
            pl.BlockSpec((None, KV_WIDTH, BLOCK), lambda b, i: (b, 0, i)),
        ],
        out_specs=pl.BlockSpec((BLOCK, ATTN_WIDTH), lambda b, i: (b * nb + i, 0)),
        scratch_shapes=[_qx_scratch()],
        compiler_params=_cparams(("parallel", "arbitrary")),
        name="swa_attention",
    )(sinks, proj, proj, proj, vt, vt)


DIL_CHUNK = 256


def _dilated_kernel(q_ref, k_ref, vt_ref, o_ref, *refs):
    i = pl.program_id(1)
    qx_ref, acc_ref = refs[0], refs[4]
    _fill_q_ext(q_ref, qx_ref)
    t0 = i * BLOCK
    c_end = (t0 + BLOCK - 1) // DIL_CHUNK + 1
    c_lo = jnp.maximum(t0 - MAX_DIL_WINDOW, 0) // DIL_CHUNK
    c_mid = jnp.maximum(t0 - DILATED_BRANCHES[-2][0], 0) // DIL_CHUNK
    key = lax.broadcasted_iota(jnp.int32, (DIL_CHUNK, BLOCK), 0)
    qry = lax.broadcasted_iota(jnp.int32, (DIL_CHUNK, BLOCK), 1)

    def hits(c, window, dil):
        dist = (t0 - c * DIL_CHUNK) + qry - key
        return (dist >= 0) & (dist <= window) & ((dist & (dil - 1)) == 0)

    def far_bias(c):
        return _bias_of(hits(c, *DILATED_BRANCHES[-1])), None

    def near_bias(c):
        mult = jnp.zeros((DIL_CHUNK, BLOCK), F32)
        for window, dil in DILATED_BRANCHES:
            mult = mult + jnp.where(hits(c, window, dil), 1.0, 0.0)
        return _bias_of(mult > 0.0), mult

    _flash_chunks(c_lo, c_mid, DIL_CHUNK, far_bias, k_ref, vt_ref, refs, False)
    _flash_chunks(c_mid, c_end, DIL_CHUNK, near_bias, k_ref, vt_ref, refs, True, init=False)
    _finish_flash(o_ref, acc_ref)


def _dilated_attention(proj, vt, batch, seq):
    nb = seq // BLOCK
    n = batch * seq
    kcol = ATTN_WIDTH // KDUP_WIDTH
    return pl.pallas_call(
        _dilated_kernel,
        out_shape=jax.ShapeDtypeStruct((n, ATTN_WIDTH), BF16),
        grid=(batch, nb),
        in_specs=[
            pl.BlockSpec((BLOCK, ATTN_WIDTH), lambda b, i: (b * nb + i, 0)),
            pl.BlockSpec((seq, KDUP_WIDTH), lambda b, i: (b, kcol)),
            pl.BlockSpec((None, KV_WIDTH, seq), lambda b, i: (b, 0, 0)),
        ],
        out_specs=pl.BlockSpec((BLOCK, ATTN_WIDTH), lambda b, i: (b * nb + i, 0)),
        scratch_shapes=_flash_scratch(DIL_CHUNK, True),
        compiler_params=_cparams(("parallel", "arbitrary")),
        name="dilated_attention",
    )(proj, proj, vt)


IDX_CHUNK = 512
ATT_CHUNK = 512
INT_MIN = -2 ** 31
F32_BITS = 32


def _sortable_to_f32(t):
    bits = jnp.where(t >= 0, t, t ^ jnp.int32(0x7FFFFFFF))
    return lax.bitcast_convert_type(bits, F32)


def _dsa_kernel(q_ref, qi_ref, wi_ref, ki_ref, k_ref, vt_ref, o_ref, sc_ref, *refs, topk):
    qx_ref, acc_ref = refs[0], refs[4]
    i = pl.program_id(1)
    t0 = i * BLOCK
    n_idx = (t0 + BLOCK + IDX_CHUNK - 1) // IDX_CHUNK
    n_att = (t0 + BLOCK + ATT_CHUNK - 1) // ATT_CHUNK

    qis = jnp.concatenate(
        [qi_ref[:, h * IDX_DIM:(h + 1) * IDX_DIM] for h in range(IDX_HEADS)], axis=0)
    w_t = (wi_ref[...] * IDX_W_SCALE).T
    key = lax.broadcasted_iota(jnp.int32, (IDX_CHUNK, BLOCK), 0)
    qry = lax.broadcasted_iota(jnp.int32, (IDX_CHUNK, BLOCK), 1)

    def idx_body(c, carry):
        start = pl.multiple_of(c * IDX_CHUNK, IDX_CHUNK)
        kic = ki_ref[pl.ds(start, IDX_CHUNK), 0:IDX_DIM]
        rel = jnp.maximum(
            lax.dot_general(kic, qis, (((1,), (1,)), ((), ())), preferred_element_type=F32), 0.0)
        score = jnp.zeros((IDX_CHUNK, BLOCK), F32)
        for h in range(IDX_HEADS):
            score = score + rel[:, h * BLOCK:(h + 1) * BLOCK] * w_t[h:h + 1, :]
        sc_ref[pl.ds(start, IDX_CHUNK), :] = jnp.where(start + key <= t0 + qry, score, NEG_INF)
        return carry

    lax.fori_loop(0, n_idx, idx_body, 0)

    def count_ge(cand_f):
        def cbody(c, cnt):
            start = pl.multiple_of(c * IDX_CHUNK, IDX_CHUNK)
            hit = jnp.where(sc_ref[pl.ds(start, IDX_CHUNK), :] >= cand_f, 1.0, 0.0)
            parts = [hit[r * SUBLANES:(r + 1) * SUBLANES, :] for r in range(IDX_CHUNK // SUBLANES)]
            while len(parts) > 1:
                parts = [a + b for a, b in zip(parts[0::2], parts[1::2])]
            return cnt + parts[0]
        cnt = lax.fori_loop(0, n_idx, cbody, jnp.zeros((SUBLANES, BLOCK), F32))
        return jnp.sum(cnt, axis=0, keepdims=True)

    def bit_cond(state):
        b, _, _, n_open = state
        return (b < F32_BITS) & (n_open > 0)

    def bit_body(state):
        b, t, done, _ = state
        bit = lax.shift_left(jnp.int32(1), F32_BITS - 1 - b)
        cand = jnp.where(b == 0, jnp.zeros_like(t), t | bit)
        cnt = count_ge(_sortable_to_f32(cand))
        take = (cnt >= float(topk)) & (done == 0)
        t = jnp.where(take, cand, t)
        done = jnp.where(take & (cnt == float(topk)), 1, done)
        n_open = jnp.sum(1 - done)
        return b + 1, t, done, n_open

    state = (jnp.int32(0), jnp.full((1, BLOCK), INT_MIN, jnp.int32),
             jnp.zeros((1, BLOCK), jnp.int32), jnp.int32(BLOCK))
    _, t_int, _, _ = lax.while_loop(bit_cond, bit_body, state)
    thr = jnp.maximum(_sortable_to_f32(t_int), jnp.float32(NEG_INF * 0.5))

    _fill_q_ext(q_ref, qx_ref)

    def make_bias(c):
        start = pl.multiple_of(c * ATT_CHUNK, ATT_CHUNK)
        return _bias_of(sc_ref[pl.ds(start, ATT_CHUNK), :] >= thr), None

    _flash_chunks(0, n_att, ATT_CHUNK, make_bias, k_ref, vt_ref, refs, False)
    _finish_flash(o_ref, acc_ref)


DSA_QI_OFF = ATTN_WIDTH
DSA_K_OFF = DSA_QI_OFF + IDX_HEADS * IDX_DIM
DSA_KI_OFF = DSA_K_OFF + KDUP_WIDTH
DSA_WI_OFF = DSA_KI_OFF + LANES
DSA_V_OFF = DSA_WI_OFF + LANES
DSA_WIDTH = DSA_V_OFF + KV_WIDTH
DSA_TN = 2048


def _dsa_attention(proj, wi, vt, batch, seq):
    nb = seq // BLOCK
    n = batch * seq
    topk = min(TOPK_MAX, seq // 4)
    qi_w = IDX_HEADS * IDX_DIM
    chunk = max(IDX_CHUNK, ATT_CHUNK)
    seq_pad = -(-seq // chunk) * chunk
    return pl.pallas_call(
        functools.partial(_dsa_kernel, topk=topk),
        out_shape=jax.ShapeDtypeStruct((n, ATTN_WIDTH), BF16),
        grid=(batch, nb),
        in_specs=[
            pl.BlockSpec((BLOCK, ATTN_WIDTH), lambda b, i: (b * nb + i, 0)),
            pl.BlockSpec((BLOCK, qi_w), lambda b, i: (b * nb + i, DSA_QI_OFF // qi_w)),
            pl.BlockSpec((BLOCK, LANES), lambda b, i: (b * nb + i, 0)),
            pl.BlockSpec((seq, LANES), lambda b, i: (b, DSA_KI_OFF // LANES)),
            pl.BlockSpec((seq, KDUP_WIDTH), lambda b, i: (b, DSA_K_OFF // KDUP_WIDTH)),
            pl.BlockSpec((None, KV_WIDTH, seq), lambda b, i: (b, 0, 0)),
        ],
        out_specs=pl.BlockSpec((BLOCK, ATTN_WIDTH), lambda b, i: (b * nb + i, 0)),
        scratch_shapes=[pltpu.VMEM((seq_pad, BLOCK), F32)] + _flash_scratch(ATT_CHUNK, False),
        compiler_params=_cparams(("parallel", "arbitrary")),
        name="dsa_attention",
    )(proj, proj, wi, proj, proj, vt)


def _rope_table(positions):
    inv = ROPE_THETA ** (-jnp.arange(0, ROPE_DIM, 2, dtype=F32) / ROPE_DIM)
    ang = positions.astype(F32).reshape(-1, 1) * inv[None, :]
    cos, sin = jnp.cos(ang), jnp.sin(ang)
    n = ang.shape[0]
    pad = HEAD_DIM - ROPE_DIM
    cos_h = jnp.concatenate([cos, cos, jnp.ones((n, pad), F32)], axis=1)
    lo_h = jnp.concatenate([-sin, jnp.zeros((n, HEAD_DIM - ROPE_HALF), F32)], axis=1)
    hi_h = jnp.concatenate([jnp.zeros((n, ROPE_HALF), F32), sin, jnp.zeros((n, pad), F32)], axis=1)
    reps = LANES // HEAD_DIM
    return jnp.concatenate([jnp.tile(cos_h, (1, reps)), jnp.tile(lo_h, (1, reps)),
                            jnp.tile(hi_h, (1, reps))], axis=1)


def _dup_heads(wk):
    d = wk.shape[0]
    w4 = wk.reshape(d, N_KV_HEADS, 1, HEAD_DIM)
    return jnp.broadcast_to(w4, (d, N_KV_HEADS, LANES // HEAD_DIM, HEAD_DIM)).reshape(d, KDUP_WIDTH)


QKV_WIDTH_EXT = ATTN_WIDTH + KDUP_WIDTH + KV_WIDTH


def _qkv_weight(w_in):
    o = ATTN_WIDTH
    return jnp.concatenate([w_in[:, :o], _dup_heads(w_in[:, o:o + KV_WIDTH]),
                            w_in[:, o + KV_WIDTH:o + 2 * KV_WIDTH]], axis=1)


def _qkv_colscale():
    return jnp.concatenate([jnp.full((1, ATTN_WIDTH), SCALE * LOG2E, F32),
                            jnp.ones((1, KDUP_WIDTH + KV_WIDTH), F32)], axis=1)


def _dsa_weight(w_in):
    d = w_in.shape[0]
    o = ATTN_WIDTH
    wq = w_in[:, :o]
    wk = w_in[:, o:o + KV_WIDTH]
    wv = w_in[:, o + KV_WIDTH:o + 2 * KV_WIDTH]
    o += 2 * KV_WIDTH
    wqi = w_in[:, o:o + IDX_HEADS * IDX_DIM]
    o += IDX_HEADS * IDX_DIM
    wki = w_in[:, o:o + IDX_DIM]
    o += IDX_DIM
    wwi = w_in[:, o:o + IDX_HEADS]
    z = lambda c: jnp.zeros((d, c), w_in.dtype)
    return jnp.concatenate([wq, wqi, _dup_heads(wk), wki, z(LANES - IDX_DIM),
                            wwi, z(LANES - IDX_HEADS), wv], axis=1)


def _dsa_colscale():
    return jnp.concatenate([jnp.full((1, ATTN_WIDTH), SCALE * LOG2E, F32),
                            jnp.full((1, IDX_HEADS * IDX_DIM), IDX_SCALE, F32),
                            jnp.ones((1, DSA_WIDTH - DSA_K_OFF), F32)], axis=1)


def _v_transposed(proj, v_off, batch, seq):
    v = proj[:, v_off:v_off + KV_WIDTH].reshape(batch, seq, KV_WIDTH)
    return jnp.swapaxes(v, 1, 2)


def kernel(x, positions, norm_attn, norm_mlp, w_up, w_down, final_norm,
           a_w_in, a_sinks, a_w_out, b_w_in, b_w_out, c_w_in, c_w_out):
    batch, seq, d = x.shape
    depth = norm_attn.shape[0]
    x2 = x.reshape(batch * seq, d)
    rope_tab = _rope_table(positions)
    qkv_scale = _qkv_colscale()
    qkv_rope_groups = (ATTN_WIDTH + KDUP_WIDTH) // LANES
    qkv_v_off = ATTN_WIDTH + KDUP_WIDTH
    for i in range(depth):
        j, kind = divmod(i, 3)
        if kind == 1:
            proj, wi = _norm_proj(x2, norm_attn[i], _dsa_weight(b_w_in[j]).astype(BF16),
                                  _dsa_colscale(), rope_tab, tn=DSA_TN,
                                  n_rope_groups=DSA_WI_OFF // LANES,
                                  aux_group=(DSA_WI_OFF % DSA_TN) // LANES)
            o = _dsa_attention(proj, wi, _v_transposed(proj, DSA_V_OFF, batch, seq), batch, seq)
            w_out = b_w_out[j]
        else:
            w_in = a_w_in[j] if kind == 0 else c_w_in[j]
            proj = _norm_proj(x2, norm_attn[i], _qkv_weight(w_in).astype(BF16), qkv_scale, rope_tab,
                              tn=QKV_WIDTH_EXT, n_rope_groups=qkv_rope_groups)[0]
            vt = _v_transposed(proj, qkv_v_off, batch, seq)
            if kind == 0:
                o = _swa_attention(proj, vt, a_sinks[j], batch, seq)
                w_out = a_w_out[j]
            else:
                o = _dilated_attention(proj, vt, batch, seq)
                w_out = c_w_out[j]
        x2 = _out_proj(o, w_out.astype(BF16), x2)
        x2 = _mlp(x2, norm_mlp[i], w_up[i].astype(BF16), w_down[i].astype(BF16))
    return _final_norm(x2, final_norm).reshape(batch, seq, d)
```

```python
import functools

import jax
import jax.numpy as jnp
from jax import lax
from jax.experimental import pallas as pl
from jax.experimental.pallas import tpu as pltpu

HEAD_DIM = 64
N_KV_HEADS = 4
GROUP = 8
N_HEADS = N_KV_HEADS * GROUP
ATTN_WIDTH = N_HEADS * HEAD_DIM
KV_WIDTH = N_KV_HEADS * HEAD_DIM
ROPE_DIM = HEAD_DIM // 4
ROPE_HALF = ROPE_DIM // 2
ROPE_THETA = 500000.0
SCALE = HEAD_DIM ** -0.5
BLOCK = 128
SWA_WINDOW = 128
IDX_HEADS = 16
IDX_DIM = 64
IDX_SCALE = IDX_DIM ** -0.5
IDX_W_SCALE = IDX_HEADS ** -0.5
TOPK_MAX = 256
DILATED_BRANCHES = ((128, 1), (512, 4), (2048, 16))
NORM_EPS = 1e-5
NEG_INF = -1e30

LANES = 128
BF16_ROWS = 16
SUBLANES = 8
MXU_TILE = 256
LOG2E = 1.4426950408889634
VMEM_LIMIT = 52 * 1024 * 1024

BF16 = jnp.bfloat16
F32 = jnp.float32

KDUP_WIDTH = N_KV_HEADS * LANES
QROWS = GROUP * BLOCK
EXT = 2 * LANES
VT_ROWS = HEAD_DIM + BF16_ROWS


def _cparams(sem):
    return pltpu.CompilerParams(dimension_semantics=sem, vmem_limit_bytes=VMEM_LIMIT)


def _norm_proj_kernel(x_ref, g_ref, w_ref, cs_ref, rope_ref, *rest,
                      n_rope_groups, groups_per_tile, out_groups, aux_group):
    o_refs = rest[:len(out_groups)]
    aux_ref = rest[len(out_groups)] if aux_group is not None else None
    h_ref = rest[-1]
    j = pl.program_id(1)

    @pl.when(j == 0)
    def _():
        x = x_ref[...]
        ms = jnp.mean(x * x, axis=-1, keepdims=True)
        h_ref[...] = ((x * lax.rsqrt(ms + NORM_EPS)) * g_ref[...]).astype(BF16)

    cos_t = rope_ref[:, 0:LANES]
    sin_lo = rope_ref[:, LANES:2 * LANES]
    sin_hi = rope_ref[:, 2 * LANES:3 * LANES]
    h = h_ref[...]
    sub_groups = MXU_TILE // LANES
    n_sub = groups_per_tile // sub_groups
    dest = [(o_ref, k) for o_ref, cnt in zip(o_refs, out_groups) for k in range(cnt)]

    def project(s):
        cols = slice(s * MXU_TILE, (s + 1) * MXU_TILE)
        return jnp.dot(h, w_ref[:, cols], preferred_element_type=F32) * cs_ref[:, cols]

    def finish(s, acc):
        for gg in range(sub_groups):
            g = s * sub_groups + gg
            a = acc[:, gg * LANES:(gg + 1) * LANES]
            r = (a * cos_t + pltpu.roll(a, LANES - ROPE_HALF, 1) * sin_lo
                 + pltpu.roll(a, ROPE_HALF, 1) * sin_hi)
            is_rope = (j * groups_per_tile + g) < n_rope_groups
            o_ref, k = dest[g]
            o_ref[:, k * LANES:(k + 1) * LANES] = jnp.where(is_rope, r, a).astype(o_ref.dtype)
            if aux_ref is not None and g == aux_group:
                aux_ref[...] = a

    acc = project(0)
    for s in range(1, n_sub):
        nxt = project(s)
        finish(s - 1, acc)
        acc = nxt
    finish(n_sub - 1, acc)


def _norm_proj(x2, g, w, colscale, rope_tab, *, tn, n_rope_groups, aux_group=None, out_widths=None, tm=512):
    n, d = x2.shape
    width = w.shape[1]
    gpt = tn // LANES
    if out_widths is None:
        out_widths = (tn,)
        out_shape = [jax.ShapeDtypeStruct((n, width), BF16)]
        out_specs = [pl.BlockSpec((tm, tn), lambda i, j: (i, j))]
    else:
        assert tn == width == sum(out_widths)
        out_shape = [jax.ShapeDtypeStruct((n, ow), BF16) for ow in out_widths]
        out_specs = [pl.BlockSpec((tm, ow), lambda i, j: (i, 0)) for ow in out_widths]
    kern = functools.partial(_norm_proj_kernel, n_rope_groups=n_rope_groups, groups_per_tile=gpt,
                             out_groups=tuple(ow // LANES for ow in out_widths), aux_group=aux_group)
    if aux_group is not None:
        out_shape.append(jax.ShapeDtypeStruct((n, LANES), F32))
        out_specs.append(pl.BlockSpec((tm, LANES), lambda i, j: (i, 0)))
    res = pl.pallas_call(
        kern,
        out_shape=out_shape,
        grid=(n // tm, width // tn),
        in_specs=[
            pl.BlockSpec((tm, d), lambda i, j: (i, 0)),
            pl.BlockSpec((1, d), lambda i, j: (0, 0)),
            pl.BlockSpec((d, tn), lambda i, j: (0, j)),
            pl.BlockSpec((1, tn), lambda i, j: (0, j)),
            pl.BlockSpec((tm, 3 * LANES), lambda i, j: (i, 0)),
        ],
        out_specs=out_specs,
        scratch_shapes=[pltpu.VMEM((tm, d), BF16)],
        compiler_params=_cparams(("parallel", "arbitrary")),
        name="norm_proj",
    )(x2, g.reshape(1, d), w, colscale, rope_tab)
    return res


def _out_proj_kernel(o_ref, w_ref, x_ref, y_ref):
    y_ref[...] = x_ref[...] + jnp.dot(o_ref[...], w_ref[...], preferred_element_type=F32)


def _out_proj(o, w, x2, *, tm=1024, tn=512):
    n, k = o.shape
    d = w.shape[1]
    return pl.pallas_call(
        _out_proj_kernel,
        out_shape=jax.ShapeDtypeStruct((n, d), F32),
        grid=(n // tm, d // tn),
        in_specs=[
            pl.BlockSpec((tm, k), lambda i, j: (i, 0)),
            pl.BlockSpec((k, tn), lambda i, j: (0, j)),
            pl.BlockSpec((tm, tn), lambda i, j: (i, j)),
        ],
        out_specs=pl.BlockSpec((tm, tn), lambda i, j: (i, j)),
        compiler_params=_cparams(("parallel", "arbitrary")),
        name="out_proj",
    )(o, w, x2)


def _mlp_kernel(x_ref, g_ref, wu_ref, wd_ref, y_ref, h_ref):
    f = pl.program_id(1)

    @pl.when(f == 0)
    def _():
        x = x_ref[...]
        ms = jnp.mean(x * x, axis=-1, keepdims=True)
        h_ref[...] = ((x * lax.rsqrt(ms + NORM_EPS)) * g_ref[...]).astype(BF16)
        y_ref[...] = x

    u = jnp.dot(h_ref[...], wu_ref[...], preferred_element_type=F32)
    u = jnp.maximum(u, 0.0)
    a = (u * u).astype(BF16)
    y_ref[...] += jnp.dot(a, wd_ref[...], preferred_element_type=F32)


def _mlp(x2, g, w_up, w_down, *, tm=512, tf=1024):
    n, d = x2.shape
    d_ff = w_up.shape[1]
    return pl.pallas_call(
        _mlp_kernel,
        out_shape=jax.ShapeDtypeStruct((n, d), F32),
        grid=(n // tm, d_ff // tf),
        in_specs=[
            pl.BlockSpec((tm, d), lambda i, f: (i, 0)),
            pl.BlockSpec((1, d), lambda i, f: (0, 0)),
            pl.BlockSpec((d, tf), lambda i, f: (0, f)),
            pl.BlockSpec((tf, d), lambda i, f: (f, 0)),
        ],
        out_specs=pl.BlockSpec((tm, d), lambda i, f: (i, 0)),
        scratch_shapes=[pltpu.VMEM((tm, d), BF16)],
        compiler_params=_cparams(("parallel", "arbitrary")),
        name="mlp",
    )(x2, g.reshape(1, d), w_up, w_down)


def _final_norm_kernel(x_ref, g_ref, y_ref):
    x = x_ref[...]
    ms = jnp.mean(x * x, axis=-1, keepdims=True)
    y_ref[...] = (x * lax.rsqrt(ms + NORM_EPS)) * g_ref[...]


def _final_norm(x2, g, *, tm=512):
    n, d = x2.shape
    return pl.pallas_call(
        _final_norm_kernel,
        out_shape=jax.ShapeDtypeStruct((n, d), F32),
        grid=(n // tm,),
        in_specs=[pl.BlockSpec((tm, d), lambda i: (i, 0)),
                  pl.BlockSpec((1, d), lambda i: (0, 0))],
        out_specs=pl.BlockSpec((tm, d), lambda i: (i, 0)),
        compiler_params=_cparams(("parallel",)),
        name="final_norm",
    )(x2, g.reshape(1, d))


def _fill_q_ext(q_ref, qx_ref):
    lane = lax.broadcasted_iota(jnp.int32, (BLOCK, LANES), 1)
    row = lax.broadcasted_iota(jnp.int32, (BLOCK, LANES), 0)
    eye = jnp.where(lane == row, 1.0, 0.0).astype(BF16)
    low = lane < HEAD_DIM
    for kv in range(N_KV_HEADS):
        for g in range(GROUP):
            h = kv * GROUP + g
            tile = q_ref[:, (h // 2) * LANES:(h // 2 + 1) * LANES]
            keep = low if h % 2 == 0 else jnp.logical_not(low)
            qx_ref[kv, g * BLOCK:(g + 1) * BLOCK, 0:LANES] = jnp.where(keep, tile, jnp.zeros_like(tile))
            qx_ref[kv, g * BLOCK:(g + 1) * BLOCK, LANES:EXT] = eye


def _scores_t(k_tile, bias, qx):
    k_ext = jnp.concatenate([k_tile, bias], axis=1)
    return lax.dot_general(k_ext, qx, (((1,), (1,)), ((), ())), preferred_element_type=F32)


def _vt_ext(vt):
    return jnp.concatenate([vt, jnp.ones((BF16_ROWS, vt.shape[1]), BF16)], axis=0)


def _flash_chunks(c_lo, c_end, tc, make_bias, k_ref, vt_ref, refs):
    qx_ref, s_ref, cm_ref, m_ref, acc_ref, bias_ref = refs
    m_ref[...] = jnp.full(m_ref.shape, NEG_INF, F32)
    acc_ref[...] = jnp.zeros(acc_ref.shape, F32)
    c_last = c_end - 1

    def put_bias(c):
        bias_ref[c & 1] = make_bias(c)

    def issue_scores(c, kv):
        start = pl.multiple_of(c * tc, tc)
        k_tile = k_ref[pl.ds(start, tc), kv * LANES:(kv + 1) * LANES]
        s_t = _scores_t(k_tile, bias_ref[c & 1], qx_ref[kv])
        s_ref[kv % 2] = s_t
        cm_ref[kv % 2] = jnp.max(s_t, axis=0, keepdims=True)

    def consume(c, kv):
        slot = kv % 2
        start = pl.multiple_of(c * tc, tc)
        vt = _vt_ext(vt_ref[kv * HEAD_DIM:(kv + 1) * HEAD_DIM, pl.ds(start, tc)])
        m_old = m_ref[kv]
        m_new = jnp.maximum(m_old, cm_ref[slot])
        alpha = jnp.exp2(m_old - m_new)
        m_ref[kv] = m_new
        for n in range(QROWS // MXU_TILE):
            cols = slice(n * MXU_TILE, (n + 1) * MXU_TILE)
            part = alpha[:, cols] * acc_ref[kv, :, cols]
            for kk in range(tc // MXU_TILE):
                rows = slice(kk * MXU_TILE, (kk + 1) * MXU_TILE)
                p = jnp.exp2(s_ref[slot, rows, cols] - m_new[:, cols])
                part = part + jnp.dot(vt[:, rows], p.astype(BF16), preferred_element_type=F32)
            acc_ref[kv, :, cols] = part

    put_bias(c_lo)
    issue_scores(c_lo, 0)

    def body(c, carry):
        c_next = jnp.minimum(c + 1, c_last)
        put_bias(c_next)
        for kv in range(N_KV_HEADS):
            if kv + 1 < N_KV_HEADS:
                issue_scores(c, kv + 1)
            else:
                issue_scores(c_next, 0)
            consume(c, kv)
        return carry

    lax.fori_loop(c_lo, c_end, body, 0)


def _store_out(o_ref, kv, o_t):
    for gp in range(GROUP // 2):
        pair = jnp.concatenate([o_t[:, (2 * gp) * BLOCK:(2 * gp + 1) * BLOCK],
                                o_t[:, (2 * gp + 1) * BLOCK:(2 * gp + 2) * BLOCK]], axis=0)
        col = (kv * GROUP + 2 * gp) * HEAD_DIM
        o_ref[:, col:col + LANES] = pair.T.astype(o_ref.dtype)


def _finish_flash(o_ref, acc_ref):
    for kv in range(N_KV_HEADS):
        acc = acc_ref[kv]
        _store_out(o_ref, kv, acc[0:HEAD_DIM, :] / acc[HEAD_DIM:HEAD_DIM + 1, :])


def _bias_of(valid):
    return jnp.where(valid, 0.0, NEG_INF).astype(BF16)


def _qx_scratch():
    return pltpu.VMEM((N_KV_HEADS, QROWS, EXT), BF16)


def _flash_scratch(tc):
    return [
        _qx_scratch(),
        pltpu.VMEM((2, tc, QROWS), F32),
        pltpu.VMEM((2, 1, QROWS), F32),
        pltpu.VMEM((N_KV_HEADS, 1, QROWS), F32),
        pltpu.VMEM((N_KV_HEADS, VT_ROWS, QROWS), F32),
        pltpu.VMEM((2, tc, BLOCK), BF16),
    ]


def _window_kernel(*refs, span, with_sinks, emit_lse):
    refs = list(refs)
    sink_ref = refs.pop(0) if with_sinks else None
    q_ref, kp_ref, kc_ref, vp_ref, vc_ref, o_ref = refs[:6]
    lse_ref = refs[6] if emit_lse else None
    qx_ref = refs[-1]
    u = pl.program_id(2)
    _fill_q_ext(q_ref, qx_ref)
    key = lax.broadcasted_iota(jnp.int32, (2 * BLOCK, BLOCK), 0)
    qry = lax.broadcasted_iota(jnp.int32, (2 * BLOCK, BLOCK), 1)
    dist = qry + BLOCK - key
    bias = _bias_of((dist >= 0) & (dist < span) & ((key >= BLOCK) | (u > 0)))
    v2 = jnp.concatenate([vp_ref[...], vc_ref[...]], axis=0)
    vt2 = _vt_ext(v2.astype(F32).T.astype(BF16))
    scores = []
    for kv in range(N_KV_HEADS):
        k_tile = jnp.concatenate([kp_ref[:, kv * LANES:(kv + 1) * LANES],
                                  kc_ref[:, kv * LANES:(kv + 1) * LANES]], axis=0)
        scores.append(_scores_t(k_tile, bias, qx_ref[kv]))
    lse_rows = []
    for kv in range(N_KV_HEADS):
        s_t = scores[kv]
        m = jnp.max(s_t, axis=0, keepdims=True)
        if with_sinks:
            sink = jnp.concatenate(
                [jnp.full((1, BLOCK), sink_ref[kv * GROUP + g] * LOG2E, F32) for g in range(GROUP)], axis=1)
            m = jnp.maximum(m, sink)
        p = jnp.exp2(s_t - m).astype(BF16)
        vt = jnp.concatenate([vt2[kv * HEAD_DIM:(kv + 1) * HEAD_DIM, :], vt2[KV_WIDTH:, :]], axis=0)
        acc = jnp.dot(vt, p, preferred_element_type=F32)
        den = acc[HEAD_DIM:HEAD_DIM + 1, :]
        if with_sinks:
            den = den + jnp.exp2(sink - m)
        _store_out(o_ref, kv, acc[0:HEAD_DIM, :] / den)
        if emit_lse:
            lse = m + jnp.log2(den)
            lse_rows += [lse[:, g * BLOCK:(g + 1) * BLOCK] for g in range(GROUP)]
    if emit_lse:
        pad = jnp.zeros((BLOCK - N_HEADS, BLOCK), F32)
        lse_ref[...] = jnp.concatenate(lse_rows + [pad], axis=0).T


def _window_attention(q, k, v, sinks, batch, seq, *, dil, span, emit_lse):
    sub = seq // dil
    nu = sub // BLOCK
    n = batch * seq
    with_sinks = sinks is not None
    view = lambda a: a.reshape(batch, sub, dil * a.shape[1])
    cur = lambda b, r, u: (b, u, r)
    prev = lambda b, r, u: (b, jnp.maximum(u - 1, 0), r)
    in_specs = [
        pl.BlockSpec((None, BLOCK, ATTN_WIDTH), cur),
        pl.BlockSpec((None, BLOCK, KDUP_WIDTH), prev),
        pl.BlockSpec((None, BLOCK, KDUP_WIDTH), cur),
        pl.BlockSpec((None, BLOCK, KV_WIDTH), prev),
        pl.BlockSpec((None, BLOCK, KV_WIDTH), cur),
    ]
    args = [view(q), view(k), view(k), view(v), view(v)]
    if with_sinks:
        in_specs.insert(0, pl.BlockSpec(memory_space=pltpu.SMEM))
        args.insert(0, sinks)
    out_shape = [jax.ShapeDtypeStruct((batch, sub, dil * ATTN_WIDTH), BF16)]
    out_specs = [pl.BlockSpec((None, BLOCK, ATTN_WIDTH), cur)]
    if emit_lse:
        out_shape.append(jax.ShapeDtypeStruct((batch, sub, dil * LANES), F32))
        out_specs.append(pl.BlockSpec((None, BLOCK, LANES), cur))
    res = pl.pallas_call(
        functools.partial(_window_kernel, span=span, with_sinks=with_sinks, emit_lse=emit_lse),
        out_shape=out_shape,
        grid=(batch, dil, nu),
        in_specs=in_specs,
        out_specs=out_specs,
        scratch_shapes=[_qx_scratch()],
        compiler_params=_cparams(("parallel", "parallel", "arbitrary")),
        name="window_attention",
    )(*args)
    o = res[0].reshape(n, ATTN_WIDTH)
    return (o, res[1].reshape(n, LANES)) if emit_lse else o


def _merge_kernel(*refs):
    n_br = (len(refs) - 2) // 2
    o_refs, l_refs, e_ref, out_ref = refs[:n_br], refs[n_br:2 * n_br], refs[-2], refs[-1]
    lses = [l[...] for l in l_refs]
    top = functools.reduce(jnp.maximum, lses)
    ws = [jnp.exp2(l - top) for l in lses]
    tot = functools.reduce(lambda a, b: a + b, ws)
    out = None
    for w, o_ref in zip(ws, o_refs):
        wn = w / tot
        hi = wn.astype(BF16)
        lo = (wn - hi.astype(F32)).astype(BF16)
        spread = (jnp.dot(hi, e_ref[...], preferred_element_type=F32)
                  + jnp.dot(lo, e_ref[...], preferred_element_type=F32))
        term = spread * o_ref[...].astype(F32)
        out = term if out is None else out + term
    out_ref[...] = out.astype(out_ref.dtype)


def _merge_branches(outs, lses, *, tm=512):
    n = outs[0].shape[0]
    head_of_lane = jnp.arange(ATTN_WIDTH, dtype=jnp.int32) // HEAD_DIM
    expand = (jnp.arange(LANES, dtype=jnp.int32)[:, None] == head_of_lane[None, :]).astype(BF16)
    row = lambda width: pl.BlockSpec((tm, width), lambda i: (i, 0))
    return pl.pallas_call(
        _merge_kernel,
        out_shape=jax.ShapeDtypeStruct((n, ATTN_WIDTH), BF16),
        grid=(n // tm,),
        in_specs=[row(ATTN_WIDTH)] * len(outs) + [row(LANES)] * len(lses)
                 + [pl.BlockSpec((LANES, ATTN_WIDTH), lambda i: (0, 0))],
        out_specs=row(ATTN_WIDTH),
        compiler_params=_cparams(("parallel",)),
        name="merge_branches",
    )(*outs, *lses, expand)


def _dilated_attention(q, k, v, batch, seq):
    outs, lses = [], []
    for window, dil in DILATED_BRANCHES:
        o, lse = _window_attention(q, k, v, None, batch, seq, dil=dil, span=window // dil + 1, emit_lse=True)
        outs.append(o)
        lses.append(lse)
    return _merge_branches(outs, lses)


IDX_CHUNK = 512
ATT_CHUNK = 512
INT_MIN = -2 ** 31
F32_BITS = 32
BITS_PER_CHECK = 4


def _sortable_to_f32(t):
    bits = jnp.where(t >= 0, t, t ^ jnp.int32(0x7FFFFFFF))
    return lax.bitcast_convert_type(bits, F32)


def _dsa_kernel(q_ref, qi_ref, wi_ref, ki_ref, k_ref, vt_ref, o_ref, sc_ref, *refs, topk):
    qx_ref, acc_ref = refs[0], refs[4]
    i = pl.program_id(1)
    t0 = i * BLOCK
    n_idx = (t0 + BLOCK + IDX_CHUNK - 1) // IDX_CHUNK
    n_att = (t0 + BLOCK + ATT_CHUNK - 1) // ATT_CHUNK

    qis = jnp.concatenate(
        [qi_ref[:, h * IDX_DIM:(h + 1) * IDX_DIM] for h in range(IDX_HEADS)], axis=0)
    w_t = (wi_ref[...] * IDX_W_SCALE).T
    key = lax.broadcasted_iota(jnp.int32, (IDX_CHUNK, BLOCK), 0)
    qry = lax.broadcasted_iota(jnp.int32, (IDX_CHUNK, BLOCK), 1)

    def idx_body(c, carry):
        start = pl.multiple_of(c * IDX_CHUNK, IDX_CHUNK)
        kic = ki_ref[pl.ds(start, IDX_CHUNK), 0:IDX_DIM]
        rel = jnp.maximum(
            lax.dot_general(kic, qis, (((1,), (1,)), ((), ())), preferred_element_type=F32), 0.0)
        score = jnp.zeros((IDX_CHUNK, BLOCK), F32)
        for h in range(IDX_HEADS):
            score = score + rel[:, h * BLOCK:(h + 1) * BLOCK] * w_t[h:h + 1, :]
        sc_ref[pl.ds(start, IDX_CHUNK), :] = jnp.where(start + key <= t0 + qry, score, NEG_INF)
        return carry

    lax.fori_loop(0, n_idx, idx_body, 0)

    def count_ge(cand_f):
        def cbody(c, cnt):
            start = pl.multiple_of(c * IDX_CHUNK, IDX_CHUNK)
            hit = jnp.where(sc_ref[pl.ds(start, IDX_CHUNK), :] >= cand_f, 1.0, 0.0)
            parts = [hit[r * SUBLANES:(r + 1) * SUBLANES, :] for r in range(IDX_CHUNK // SUBLANES)]
            while len(parts) > 1:
                parts = [a + b for a, b in zip(parts[0::2], parts[1::2])]
            return cnt + parts[0]
        cnt = lax.fori_loop(0, n_idx, cbody, jnp.zeros((SUBLANES, BLOCK), F32))
        return jnp.sum(cnt, axis=0, keepdims=True)

    def bit_cond(state):
        b, _, _, n_open = state
        return (b < F32_BITS) & (n_open > 0)

    def bit_body(state):
        b0, t, done, _ = state
        for k in range(BITS_PER_CHECK):
            b = b0 + k
            bit = lax.shift_left(jnp.int32(1), F32_BITS - 1 - b)
            cand = jnp.where(b == 0, jnp.zeros_like(t), t | bit)
            cnt = count_ge(_sortable_to_f32(cand))
            take = (cnt >= float(topk)) & (done == 0)
            t = jnp.where(take, cand, t)
            done = jnp.where(take & (cnt == float(topk)), 1, done)
        n_open = jnp.sum(1 - done)
        return b0 + BITS_PER_CHECK, t, done, n_open

    state = (jnp.int32(0), jnp.full((1, BLOCK), INT_MIN, jnp.int32),
             jnp.zeros((1, BLOCK), jnp.int32), jnp.int32(BLOCK))
    _, t_int, _, _ = lax.while_loop(bit_cond, bit_body, state)
    thr = jnp.maximum(_sortable_to_f32(t_int), jnp.float32(NEG_INF * 0.5))

    _fill_q_ext(q_ref, qx_ref)

    def make_bias(c):
        start = pl.multiple_of(c * ATT_CHUNK, ATT_CHUNK)
        return _bias_of(sc_ref[pl.ds(start, ATT_CHUNK), :] >= thr)

    _flash_chunks(0, n_att, ATT_CHUNK, make_bias, k_ref, vt_ref, refs)
    _finish_flash(o_ref, acc_ref)


DSA_QI_OFF = ATTN_WIDTH
DSA_K_OFF = DSA_QI_OFF + IDX_HEADS * IDX_DIM
DSA_KI_OFF = DSA_K_OFF + KDUP_WIDTH
DSA_WI_OFF = DSA_KI_OFF + LANES
DSA_V_OFF = DSA_WI_OFF + LANES
DSA_WIDTH = DSA_V_OFF + KV_WIDTH
DSA_TN = 2048


def _dsa_attention(proj, wi, vt, batch, seq):
    nb = seq // BLOCK
    n = batch * seq
    topk = min(TOPK_MAX, seq // 4)
    qi_w = IDX_HEADS * IDX_DIM
    chunk = max(IDX_CHUNK, ATT_CHUNK)
    seq_pad = -(-seq // chunk) * chunk
    return pl.pallas_call(
        functools.partial(_dsa_kernel, topk=topk),
        out_shape=jax.ShapeDtypeStruct((n, ATTN_WIDTH), BF16),
        grid=(batch, nb),
        in_specs=[
            pl.BlockSpec((BLOCK, ATTN_WIDTH), lambda b, i: (b * nb + i, 0)),
            pl.BlockSpec((BLOCK, qi_w), lambda b, i: (b * nb + i, DSA_QI_OFF // qi_w)),
            pl.BlockSpec((BLOCK, LANES), lambda b, i: (b * nb + i, 0)),
            pl.BlockSpec((seq, LANES), lambda b, i: (b, DSA_KI_OFF // LANES)),
            pl.BlockSpec((seq, KDUP_WIDTH), lambda b, i: (b, DSA_K_OFF // KDUP_WIDTH)),
            pl.BlockSpec((None, KV_WIDTH, seq), lambda b, i: (b, 0, 0)),
        ],
        out_specs=pl.BlockSpec((BLOCK, ATTN_WIDTH), lambda b, i: (b * nb + i, 0)),
        scratch_shapes=[pltpu.VMEM((seq_pad, BLOCK), F32)] + _flash_scratch(ATT_CHUNK),
        compiler_params=_cparams(("parallel", "arbitrary")),
        name="dsa_attention",
    )(proj, proj, wi, proj, proj, vt)


def _rope_table(positions):
    inv = ROPE_THETA ** (-jnp.arange(0, ROPE_DIM, 2, dtype=F32) / ROPE_DIM)
    ang = positions.astype(F32).reshape(-1, 1) * inv[None, :]
    cos, sin = jnp.cos(ang), jnp.sin(ang)
    n = ang.shape[0]
    pad = HEAD_DIM - ROPE_DIM
    cos_h = jnp.concatenate([cos, cos, jnp.ones((n, pad), F32)], axis=1)
    lo_h = jnp.concatenate([-sin, jnp.zeros((n, HEAD_DIM - ROPE_HALF), F32)], axis=1)
    hi_h = jnp.concatenate([jnp.zeros((n, ROPE_HALF), F32), sin, jnp.zeros((n, pad), F32)], axis=1)
    reps = LANES // HEAD_DIM
    return jnp.concatenate([jnp.tile(cos_h, (1, reps)), jnp.tile(lo_h, (1, reps)),
                            jnp.tile(hi_h, (1, reps))], axis=1)


def _dup_heads(wk):
    d = wk.shape[0]
    w4 = wk.reshape(d, N_KV_HEADS, 1, HEAD_DIM)
    return jnp.broadcast_to(w4, (d, N_KV_HEADS, LANES // HEAD_DIM, HEAD_DIM)).reshape(d, KDUP_WIDTH)


QKV_WIDTH_EXT = ATTN_WIDTH + KDUP_WIDTH + KV_WIDTH


def _qkv_weight(w_in):
    o = ATTN_WIDTH
    return jnp.concatenate([w_in[:, :o], _dup_heads(w_in[:, o:o + KV_WIDTH]),
                            w_in[:, o + KV_WIDTH:o + 2 * KV_WIDTH]], axis=1)


def _qkv_colscale():
    return jnp.concatenate([jnp.full((1, ATTN_WIDTH), SCALE * LOG2E, F32),
                            jnp.ones((1, KDUP_WIDTH + KV_WIDTH), F32)], axis=1)


def _dsa_weight(w_in):
    d = w_in.shape[0]
    o = ATTN_WIDTH
    wq = w_in[:, :o]
    wk = w_in[:, o:o + KV_WIDTH]
    wv = w_in[:, o + KV_WIDTH:o + 2 * KV_WIDTH]
    o += 2 * KV_WIDTH
    wqi = w_in[:, o:o + IDX_HEADS * IDX_DIM]
    o += IDX_HEADS * IDX_DIM
    wki = w_in[:, o:o + IDX_DIM]
    o += IDX_DIM
    wwi = w_in[:, o:o + IDX_HEADS]
    z = lambda c: jnp.zeros((d, c), w_in.dtype)
    return jnp.concatenate([wq, wqi, _dup_heads(wk), wki, z(LANES - IDX_DIM),
                            wwi, z(LANES - IDX_HEADS), wv], axis=1)


def _dsa_colscale():
    return jnp.concatenate([jnp.full((1, ATTN_WIDTH), SCALE * LOG2E, F32),
                            jnp.full((1, IDX_HEADS * IDX_DIM), IDX_SCALE, F32),
                            jnp.ones((1, DSA_WIDTH - DSA_K_OFF), F32)], axis=1)


def _v_transposed(proj, v_off, batch, seq):
    v = proj[:, v_off:v_off + KV_WIDTH].reshape(batch, seq, KV_WIDTH)
    return jnp.swapaxes(v, 1, 2)


def kernel(x, positions, norm_attn, norm_mlp, w_up, w_down, final_norm,
           a_w_in, a_sinks, a_w_out, b_w_in, b_w_out, c_w_in, c_w_out):
    batch, seq, d = x.shape
    depth = norm_attn.shape[0]
    x2 = x.reshape(batch * seq, d)
    rope_tab = _rope_table(positions)
    qkv_scale = _qkv_colscale()
    qkv_rope_groups = (ATTN_WIDTH + KDUP_WIDTH) // LANES
    for i in range(depth):
        j, kind = divmod(i, 3)
        if kind == 1:
            proj, wi = _norm_proj(x2, norm_attn[i], _dsa_weight(b_w_in[j]).astype(BF16),
                                  _dsa_colscale(), rope_tab, tn=DSA_TN,
                                  n_rope_groups=DSA_WI_OFF // LANES,
                                  aux_group=(DSA_WI_OFF % DSA_TN) // LANES)
            o = _dsa_attention(proj, wi, _v_transposed(proj, DSA_V_OFF, batch, seq), batch, seq)
            w_out = b_w_out[j]
        else:
            w_in = a_w_in[j] if kind == 0 else c_w_in[j]
            q, k, v = _norm_proj(x2, norm_attn[i], _qkv_weight(w_in).astype(BF16), qkv_scale, rope_tab,
                                 tn=QKV_WIDTH_EXT, n_rope_groups=qkv_rope_groups,
                                 out_widths=(ATTN_WIDTH, KDUP_WIDTH, KV_WIDTH))
            if kind == 0:
                o = _window_attention(q, k, v, a_sinks[j], batch, seq, dil=1, span=SWA_WINDOW, emit_lse=False)
                w_out = a_w_out[j]
            else:
                o = _dilated_attention(q, k, v, batch, seq)
                w_out = c_w_out[j]
        x2 = _out_proj(o, w_out.astype(BF16), x2)
        x2 = _mlp(x2, norm_mlp[i], w_up[i].astype(BF16), w_down[i].astype(BF16))
    return _final_norm(x2, final_norm).reshape(batch, seq, d)
```

```python
import functools

import jax
import jax.numpy as jnp
from jax import lax
from jax.experimental import pallas as pl
from jax.experimental.pallas import tpu as pltpu

HEAD_DIM = 64
N_KV_HEADS = 4
GROUP = 8
N_HEADS = N_KV_HEADS * GROUP
ATTN_WIDTH = N_HEADS * HEAD_DIM
KV_WIDTH = N_KV_HEADS * HEAD_DIM
ROPE_DIM = HEAD_DIM // 4
ROPE_HALF = ROPE_DIM // 2
ROPE_THETA = 500000.0
SCALE = HEAD_DIM ** -0.5
BLOCK = 128
SWA_WINDOW = 128
IDX_HEADS = 16
IDX_DIM = 64
IDX_SCALE = IDX_DIM ** -0.5
IDX_W_SCALE = IDX_HEADS ** -0.5
TOPK_MAX = 256
DILATED_BRANCHES = ((128, 1), (512, 4), (2048, 16))
NORM_EPS = 1e-5
NEG_INF = -1e30

LANES = 128
BF16_ROWS = 16
SUBLANES = 8
MXU_TILE = 256
LOG2E = 1.4426950408889634
VMEM_LIMIT = 52 * 1024 * 1024

BF16 = jnp.bfloat16
F32 = jnp.float32

KDUP_WIDTH = N_KV_HEADS * LANES
QROWS = GROUP * BLOCK
EXT = 2 * LANES
VT_ROWS = HEAD_DIM + BF16_ROWS


def _cparams(sem):
    return pltpu.CompilerParams(dimension_semantics=sem, vmem_limit_bytes=VMEM_LIMIT)


def _norm_proj_kernel(x_ref, g_ref, w_ref, cs_ref, rope_ref, *rest,
                      n_rope_groups, groups_per_tile, out_groups, aux_group):
    o_refs = rest[:len(out_groups)]
    aux_ref = rest[len(out_groups)] if aux_group is not None else None
    h_ref = rest[-1]
    j = pl.program_id(1)

    @pl.when(j == 0)
    def _():
        x = x_ref[...]
        ms = jnp.mean(x * x, axis=-1, keepdims=True)
        h_ref[...] = ((x * lax.rsqrt(ms + NORM_EPS)) * g_ref[...]).astype(BF16)

    cos_t = rope_ref[:, 0:LANES]
    sin_lo = rope_ref[:, LANES:2 * LANES]
    sin_hi = rope_ref[:, 2 * LANES:3 * LANES]
    h = h_ref[...]
    sub_groups = MXU_TILE // LANES
    n_sub = groups_per_tile // sub_groups
    dest = [(o_ref, k) for o_ref, cnt in zip(o_refs, out_groups) for k in range(cnt)]

    def project(s):
        cols = slice(s * MXU_TILE, (s + 1) * MXU_TILE)
        return jnp.dot(h, w_ref[:, cols], preferred_element_type=F32) * cs_ref[:, cols]

    def finish(s, acc):
        for gg in range(sub_groups):
            g = s * sub_groups + gg
            a = acc[:, gg * LANES:(gg + 1) * LANES]
            r = (a * cos_t + pltpu.roll(a, LANES - ROPE_HALF, 1) * sin_lo
                 + pltpu.roll(a, ROPE_HALF, 1) * sin_hi)
            is_rope = (j * groups_per_tile + g) < n_rope_groups
            o_ref, k = dest[g]
            o_ref[:, k * LANES:(k + 1) * LANES] = jnp.where(is_rope, r, a).astype(o_ref.dtype)
            if aux_ref is not None and g == aux_group:
                aux_ref[...] = a

    acc = project(0)
    for s in range(1, n_sub):
        nxt = project(s)
        finish(s - 1, acc)
        acc = nxt
    finish(n_sub - 1, acc)


def _norm_proj(x2, g, w, colscale, rope_tab, *, tn, n_rope_groups, aux_group=None, out_widths=None, tm=512):
    n, d = x2.shape
    width = w.shape[1]
    gpt = tn // LANES
    if out_widths is None:
        out_widths = (tn,)
        out_shape = [jax.ShapeDtypeStruct((n, width), BF16)]
        out_specs = [pl.BlockSpec((tm, tn), lambda i, j: (i, j))]
    else:
        assert tn == width == sum(out_widths)
        out_shape = [jax.ShapeDtypeStruct((n, ow), BF16) for ow in out_widths]
        out_specs = [pl.BlockSpec((tm, ow), lambda i, j: (i, 0)) for ow in out_widths]
    kern = functools.partial(_norm_proj_kernel, n_rope_groups=n_rope_groups, groups_per_tile=gpt,
                             out_groups=tuple(ow // LANES for ow in out_widths), aux_group=aux_group)
    if aux_group is not None:
        out_shape.append(jax.ShapeDtypeStruct((n, LANES), F32))
        out_specs.append(pl.BlockSpec((tm, LANES), lambda i, j: (i, 0)))
    res = pl.pallas_call(
        kern,
        out_shape=out_shape,
        grid=(n // tm, width // tn),
        in_specs=[
            pl.BlockSpec((tm, d), lambda i, j: (i, 0)),
            pl.BlockSpec((1, d), lambda i, j: (0, 0)),
            pl.BlockSpec((d, tn), lambda i, j: (0, j)),
            pl.BlockSpec((1, tn), lambda i, j: (0, j)),
            pl.BlockSpec((tm, 3 * LANES), lambda i, j: (i, 0)),
        ],
        out_specs=out_specs,
        scratch_shapes=[pltpu.VMEM((tm, d), BF16)],
        compiler_params=_cparams(("parallel", "arbitrary")),
        name="norm_proj",
    )(x2, g.reshape(1, d), w, colscale, rope_tab)
    return res


def _out_proj_kernel(o_ref, w_ref, x_ref, y_ref):
    y_ref[...] = x_ref[...] + jnp.dot(o_ref[...], w_ref[...], preferred_element_type=F32)


def _out_proj(o, w, x2, *, tm=512, tn=2048):
    n, k = o.shape
    d = w.shape[1]
    return pl.pallas_call(
        _out_proj_kernel,
        out_shape=jax.ShapeDtypeStruct((n, d), F32),
        grid=(n // tm, d // tn),
        in_specs=[
            pl.BlockSpec((tm, k), lambda i, j: (i, 0)),
            pl.BlockSpec((k, tn), lambda i, j: (0, j)),
            pl.BlockSpec((tm, tn), lambda i, j: (i, j)),
        ],
        out_specs=pl.BlockSpec((tm, tn), lambda i, j: (i, j)),
        compiler_params=_cparams(("parallel", "arbitrary")),
        name="out_proj",
    )(o, w, x2)


def _rms(x, g):
    ms = jnp.mean(x * x, axis=-1, keepdims=True)
    return (x * lax.rsqrt(ms + NORM_EPS)) * g


def _mlp_kernel(x_ref, g_ref, wu_ref, wd_ref, *rest, out_norm):
    y_ref, h_ref = rest[-2:]
    f = pl.program_id(1)

    @pl.when(f == 0)
    def _():
        x = x_ref[...]
        h_ref[...] = _rms(x, g_ref[...]).astype(BF16)
        y_ref[...] = x

    u = jnp.dot(h_ref[...], wu_ref[...], preferred_element_type=F32)
    u = jnp.maximum(u, 0.0)
    a = (u * u).astype(BF16)
    y_ref[...] += jnp.dot(a, wd_ref[...], preferred_element_type=F32)

    if out_norm:
        @pl.when(f == pl.num_programs(1) - 1)
        def _():
            y_ref[...] = _rms(y_ref[...], rest[0][...])


def _mlp(x2, g, w_up, w_down, out_gain=None, *, tm=512, tf=1024):
    n, d = x2.shape
    d_ff = w_up.shape[1]
    vec = pl.BlockSpec((1, d), lambda i, f: (0, 0))
    in_specs = [
        pl.BlockSpec((tm, d), lambda i, f: (i, 0)),
        vec,
        pl.BlockSpec((d, tf), lambda i, f: (0, f)),
        pl.BlockSpec((tf, d), lambda i, f: (f, 0)),
    ]
    args = [x2, g.reshape(1, d), w_up, w_down]
    if out_gain is not None:
        in_specs.append(vec)
        args.append(out_gain.reshape(1, d))
    return pl.pallas_call(
        functools.partial(_mlp_kernel, out_norm=out_gain is not None),
        out_shape=jax.ShapeDtypeStruct((n, d), F32),
        grid=(n // tm, d_ff // tf),
        in_specs=in_specs,
        out_specs=pl.BlockSpec((tm, d), lambda i, f: (i, 0)),
        scratch_shapes=[pltpu.VMEM((tm, d), BF16)],
        compiler_params=_cparams(("parallel", "arbitrary")),
        name="mlp",
    )(*args)


def _fill_q_ext(q_ref, qx_ref, row0=0):
    lane = lax.broadcasted_iota(jnp.int32, (BLOCK, LANES), 1)
    row = lax.broadcasted_iota(jnp.int32, (BLOCK, LANES), 0)
    eye = jnp.where(lane == row, 1.0, 0.0).astype(BF16)
    low = lane < HEAD_DIM
    for kv in range(N_KV_HEADS):
        for g in range(GROUP):
            h = kv * GROUP + g
            tile = q_ref[row0:row0 + BLOCK, (h // 2) * LANES:(h // 2 + 1) * LANES]
            keep = low if h % 2 == 0 else jnp.logical_not(low)
            qx_ref[kv, g * BLOCK:(g + 1) * BLOCK, 0:LANES] = jnp.where(keep, tile, jnp.zeros_like(tile))
            qx_ref[kv, g * BLOCK:(g + 1) * BLOCK, LANES:EXT] = eye


def _scores_t(k_tile, bias, qx):
    k_ext = jnp.concatenate([k_tile, bias], axis=1)
    return lax.dot_general(k_ext, qx, (((1,), (1,)), ((), ())), preferred_element_type=F32)


def _vt_ext(vt):
    return jnp.concatenate([vt, jnp.ones((BF16_ROWS, vt.shape[1]), BF16)], axis=0)


def _flash_chunks(c_lo, c_end, tc, make_bias, k_ref, vt_ref, refs):
    qx_ref, s_ref, cm_ref, m_ref, acc_ref, bias_ref = refs
    m_ref[...] = jnp.full(m_ref.shape, NEG_INF, F32)
    acc_ref[...] = jnp.zeros(acc_ref.shape, F32)
    c_last = c_end - 1

    def put_bias(c):
        bias_ref[c & 1] = make_bias(c)

    def issue_scores(c, kv):
        start = pl.multiple_of(c * tc, tc)
        k_tile = k_ref[pl.ds(start, tc), kv * LANES:(kv + 1) * LANES]
        s_t = _scores_t(k_tile, bias_ref[c & 1], qx_ref[kv])
        s_ref[kv % 2] = s_t
        cm_ref[kv % 2] = jnp.max(s_t, axis=0, keepdims=True)

    def consume(c, kv):
        slot = kv % 2
        start = pl.multiple_of(c * tc, tc)
        vt = _vt_ext(vt_ref[kv * HEAD_DIM:(kv + 1) * HEAD_DIM, pl.ds(start, tc)])
        m_old = m_ref[kv]
        m_new = jnp.maximum(m_old, cm_ref[slot])
        alpha = jnp.exp2(m_old - m_new)
        m_ref[kv] = m_new
        for n in range(QROWS // MXU_TILE):
            cols = slice(n * MXU_TILE, (n + 1) * MXU_TILE)
            part = alpha[:, cols] * acc_ref[kv, :, cols]
            for kk in range(tc // MXU_TILE):
                rows = slice(kk * MXU_TILE, (kk + 1) * MXU_TILE)
                p = jnp.exp2(s_ref[slot, rows, cols] - m_new[:, cols])
                part = part + jnp.dot(vt[:, rows], p.astype(BF16), preferred_element_type=F32)
            acc_ref[kv, :, cols] = part

    put_bias(c_lo)
    issue_scores(c_lo, 0)

    def body(c, carry):
        c_next = jnp.minimum(c + 1, c_last)
        put_bias(c_next)
        for kv in range(N_KV_HEADS):
            if kv + 1 < N_KV_HEADS:
                issue_scores(c, kv + 1)
            else:
                issue_scores(c_next, 0)
            consume(c, kv)
        return carry

    lax.fori_loop(c_lo, c_end, body, 0)


def _store_out(o_ref, kv, o_t, row0=0):
    for gp in range(GROUP // 2):
        pair = jnp.concatenate([o_t[:, (2 * gp) * BLOCK:(2 * gp + 1) * BLOCK],
                                o_t[:, (2 * gp + 1) * BLOCK:(2 * gp + 2) * BLOCK]], axis=0)
        col = (kv * GROUP + 2 * gp) * HEAD_DIM
        o_ref[row0:row0 + BLOCK, col:col + LANES] = pair.T.astype(o_ref.dtype)


def _finish_flash(o_ref, acc_ref):
    for kv in range(N_KV_HEADS):
        acc = acc_ref[kv]
        _store_out(o_ref, kv, acc[0:HEAD_DIM, :] / acc[HEAD_DIM:HEAD_DIM + 1, :])


def _bias_of(valid):
    return jnp.where(valid, 0.0, NEG_INF).astype(BF16)


def _qx_scratch():
    return pltpu.VMEM((N_KV_HEADS, QROWS, EXT), BF16)


def _flash_scratch(tc):
    return [
        _qx_scratch(),
        pltpu.VMEM((2, tc, QROWS), F32),
        pltpu.VMEM((2, 1, QROWS), F32),
        pltpu.VMEM((N_KV_HEADS, 1, QROWS), F32),
        pltpu.VMEM((N_KV_HEADS, VT_ROWS, QROWS), F32),
        pltpu.VMEM((2, tc, BLOCK), BF16),
    ]


WIN_TILES = 1


def _window_kernel(*refs, span, with_sinks, emit_lse):
    refs = list(refs)
    sink_ref = refs.pop(0) if with_sinks else None
    q_ref, kp_ref, kc_ref, vp_ref, vc_ref, o_ref = refs[:6]
    lse_ref = refs[6] if emit_lse else None
    qx_ref = refs[-1]
    first = pl.program_id(2) == 0
    key = lax.broadcasted_iota(jnp.int32, (2 * BLOCK, BLOCK), 0)
    qry = lax.broadcasted_iota(jnp.int32, (2 * BLOCK, BLOCK), 1)
    dist = qry + BLOCK - key
    band = (dist >= 0) & (dist < span)
    bias_any = _bias_of(band)
    bias_first = _bias_of(band & ((key >= BLOCK) | jnp.logical_not(first)))

    def v_t(block):
        return block.astype(F32).T.astype(BF16)

    ones = jnp.ones((BF16_ROWS, 2 * BLOCK), BF16)
    vts = [v_t(vp_ref[...])] + [v_t(vc_ref[j * BLOCK:(j + 1) * BLOCK, :]) for j in range(WIN_TILES)]

    def issue_scores(j):
        _fill_q_ext(q_ref, qx_ref.at[j % 2], j * BLOCK)
        out = []
        for kv in range(N_KV_HEADS):
            lanes = slice(kv * LANES, (kv + 1) * LANES)
            k_prev = kp_ref[:, lanes] if j == 0 else kc_ref[(j - 1) * BLOCK:j * BLOCK, lanes]
            k_tile = jnp.concatenate([k_prev, kc_ref[j * BLOCK:(j + 1) * BLOCK, lanes]], axis=0)
            out.append(_scores_t(k_tile, bias_first if j == 0 else bias_any, qx_ref[j % 2, kv]))
        return out

    def consume(j, scores):
        vt2 = jnp.concatenate([vts[j], vts[j + 1]], axis=1)
        lse_rows = []
        for kv in range(N_KV_HEADS):
            s_t = scores[kv]
            m = jnp.max(s_t, axis=0, keepdims=True)
            if with_sinks:
                sink = jnp.concatenate(
                    [jnp.full((1, BLOCK), sink_ref[kv * GROUP + g] * LOG2E, F32) for g in range(GROUP)], axis=1)
                m = jnp.maximum(m, sink)
            p = jnp.exp2(s_t - m).astype(BF16)
            vt = jnp.concatenate([vt2[kv * HEAD_DIM:(kv + 1) * HEAD_DIM, :], ones], axis=0)
            acc = jnp.dot(vt, p, preferred_element_type=F32)
            den = acc[HEAD_DIM:HEAD_DIM + 1, :]
            if with_sinks:
                den = den + jnp.exp2(sink - m)
            _store_out(o_ref, kv, acc[0:HEAD_DIM, :] / den, j * BLOCK)
            if emit_lse:
                lse = m + jnp.log2(den)
                lse_rows += [lse[:, g * BLOCK:(g + 1) * BLOCK] for g in range(GROUP)]
        if emit_lse:
            pad = jnp.zeros((BLOCK - N_HEADS, BLOCK), F32)
            lse_ref[j * BLOCK:(j + 1) * BLOCK, :] = jnp.concatenate(lse_rows + [pad], axis=0).T

    scores = issue_scores(0)
    for j in range(WIN_TILES):
        nxt = issue_scores(j + 1) if j + 1 < WIN_TILES else None
        consume(j, scores)
        scores = nxt


def _window_attention(q, k, v, sinks, batch, seq, *, dil, span, emit_lse):
    sub = seq // dil
    rows = WIN_TILES * BLOCK
    n = batch * seq
    with_sinks = sinks is not None
    view = lambda a: a.reshape(batch, sub, dil * a.shape[1])
    cur = lambda b, r, u: (b, u, r)
    prev = lambda b, r, u: (b, jnp.maximum(u * WIN_TILES - 1, 0), r)
    in_specs = [
        pl.BlockSpec((None, rows, ATTN_WIDTH), cur),
        pl.BlockSpec((None, BLOCK, KDUP_WIDTH), prev),
        pl.BlockSpec((None, rows, KDUP_WIDTH), cur),
        pl.BlockSpec((None, BLOCK, KV_WIDTH), prev),
        pl.BlockSpec((None, rows, KV_WIDTH), cur),
    ]
    args = [view(q), view(k), view(k), view(v), view(v)]
    if with_sinks:
        in_specs.insert(0, pl.BlockSpec(memory_space=pltpu.SMEM))
        args.insert(0, sinks)
    out_shape = [jax.ShapeDtypeStruct((batch, sub, dil * ATTN_WIDTH), BF16)]
    out_specs = [pl.BlockSpec((None, rows, ATTN_WIDTH), cur)]
    if emit_lse:
        out_shape.append(jax.ShapeDtypeStruct((batch, sub, dil * LANES), F32))
        out_specs.append(pl.BlockSpec((None, rows, LANES), cur))
    res = pl.pallas_call(
        functools.partial(_window_kernel, span=span, with_sinks=with_sinks, emit_lse=emit_lse),
        out_shape=out_shape,
        grid=(batch, dil, sub // rows),
        in_specs=in_specs,
        out_specs=out_specs,
        scratch_shapes=[pltpu.VMEM((2, N_KV_HEADS, QROWS, EXT), BF16)],
        compiler_params=_cparams(("parallel", "parallel", "arbitrary")),
        name="window_attention",
    )(*args)
    o = res[0].reshape(n, ATTN_WIDTH)
    return (o, res[1].reshape(n, LANES)) if emit_lse else o


def _merge_kernel(*refs):
    n_br = (len(refs) - 2) // 2
    o_refs, l_refs, e_ref, out_ref = refs[:n_br], refs[n_br:2 * n_br], refs[-2], refs[-1]
    lses = [l[...] for l in l_refs]
    top = functools.reduce(jnp.maximum, lses)
    ws = [jnp.exp2(l - top) for l in lses]
    tot = functools.reduce(lambda a, b: a + b, ws)
    out = None
    for w, o_ref in zip(ws, o_refs):
        wn = w / tot
        hi = wn.astype(BF16)
        lo = (wn - hi.astype(F32)).astype(BF16)
        spread = (jnp.dot(hi, e_ref[...], preferred_element_type=F32)
                  + jnp.dot(lo, e_ref[...], preferred_element_type=F32))
        term = spread * o_ref[...].astype(F32)
        out = term if out is None else out + term
    out_ref[...] = out.astype(out_ref.dtype)


def _merge_branches(outs, lses, *, tm=512):
    n = outs[0].shape[0]
    head_of_lane = jnp.arange(ATTN_WIDTH, dtype=jnp.int32) // HEAD_DIM
    expand = (jnp.arange(LANES, dtype=jnp.int32)[:, None] == head_of_lane[None, :]).astype(BF16)
    row = lambda width: pl.BlockSpec((tm, width), lambda i: (i, 0))
    return pl.pallas_call(
        _merge_kernel,
        out_shape=jax.ShapeDtypeStruct((n, ATTN_WIDTH), BF16),
        grid=(n // tm,),
        in_specs=[row(ATTN_WIDTH)] * len(outs) + [row(LANES)] * len(lses)
                 + [pl.BlockSpec((LANES, ATTN_WIDTH), lambda i: (0, 0))],
        out_specs=row(ATTN_WIDTH),
        compiler_params=_cparams(("parallel",)),
        name="merge_branches",
    )(*outs, *lses, expand)


def _dilated_attention(q, k, v, batch, seq):
    outs, lses = [], []
    for window, dil in DILATED_BRANCHES:
        o, lse = _window_attention(q, k, v, None, batch, seq, dil=dil, span=window // dil + 1, emit_lse=True)
        outs.append(o)
        lses.append(lse)
    return _merge_branches(outs, lses)


IDX_CHUNK = 512
ATT_CHUNK = 512
INT_MIN = -2 ** 31
F32_BITS = 32
BITS_PER_CHECK = 4


def _sortable_to_f32(t):
    bits = jnp.where(t >= 0, t, t ^ jnp.int32(0x7FFFFFFF))
    return lax.bitcast_convert_type(bits, F32)


def _dsa_kernel(q_ref, qi_ref, wi_ref, ki_ref, k_ref, vt_ref, o_ref, sc_ref, *refs, topk):
    qx_ref, acc_ref = refs[0], refs[4]
    i = pl.program_id(1)
    t0 = i * BLOCK
    n_idx = (t0 + BLOCK + IDX_CHUNK - 1) // IDX_CHUNK
    n_att = (t0 + BLOCK + ATT_CHUNK - 1) // ATT_CHUNK

    qis = jnp.concatenate(
        [qi_ref[:, h * IDX_DIM:(h + 1) * IDX_DIM] for h in range(IDX_HEADS)], axis=0)
    w_t = (wi_ref[...] * IDX_W_SCALE).T
    key = lax.broadcasted_iota(jnp.int32, (IDX_CHUNK, BLOCK), 0)
    qry = lax.broadcasted_iota(jnp.int32, (IDX_CHUNK, BLOCK), 1)

    def idx_body(c, carry):
        start = pl.multiple_of(c * IDX_CHUNK, IDX_CHUNK)
        kic = ki_ref[pl.ds(start, IDX_CHUNK), 0:IDX_DIM]
        rel = jnp.maximum(
            lax.dot_general(kic, qis, (((1,), (1,)), ((), ())), preferred_element_type=F32), 0.0)
        score = jnp.zeros((IDX_CHUNK, BLOCK), F32)
        for h in range(IDX_HEADS):
            score = score + rel[:, h * BLOCK:(h + 1) * BLOCK] * w_t[h:h + 1, :]
        sc_ref[pl.ds(start, IDX_CHUNK), :] = jnp.where(start + key <= t0 + qry, score, NEG_INF)
        return carry

    lax.fori_loop(0, n_idx, idx_body, 0)

    def count_ge(cand_f):
        def cbody(c, cnt):
            start = pl.multiple_of(c * IDX_CHUNK, IDX_CHUNK)
            hit = jnp.where(sc_ref[pl.ds(start, IDX_CHUNK), :] >= cand_f, 1.0, 0.0)
            parts = [hit[r * SUBLANES:(r + 1) * SUBLANES, :] for r in range(IDX_CHUNK // SUBLANES)]
            while len(parts) > 1:
                parts = [a + b for a, b in zip(parts[0::2], parts[1::2])]
            return cnt + parts[0]
        cnt = lax.fori_loop(0, n_idx, cbody, jnp.zeros((SUBLANES, BLOCK), F32))
        return jnp.sum(cnt, axis=0, keepdims=True)

    def bit_cond(state):
        b, _, _, n_open = state
        return (b < F32_BITS) & (n_open > 0)

    def bit_body(state):
        b0, t, done, _ = state
        for k in range(BITS_PER_CHECK):
            b = b0 + k
            bit = lax.shift_left(jnp.int32(1), F32_BITS - 1 - b)
            cand = jnp.where(b == 0, jnp.zeros_like(t), t | bit)
            cnt = count_ge(_sortable_to_f32(cand))
            take = (cnt >= float(topk)) & (done == 0)
            t = jnp.where(take, cand, t)
            done = jnp.where(take & (cnt == float(topk)), 1, done)
        n_open = jnp.sum(1 - done)
        return b0 + BITS_PER_CHECK, t, done, n_open

    state = (jnp.int32(0), jnp.full((1, BLOCK), INT_MIN, jnp.int32),
             jnp.zeros((1, BLOCK), jnp.int32), jnp.int32(BLOCK))
    _, t_int, _, _ = lax.while_loop(bit_cond, bit_body, state)
    thr = jnp.maximum(_sortable_to_f32(t_int), jnp.float32(NEG_INF * 0.5))

    _fill_q_ext(q_ref, qx_ref)

    def make_bias(c):
        start = pl.multiple_of(c * ATT_CHUNK, ATT_CHUNK)
        return _bias_of(sc_ref[pl.ds(start, ATT_CHUNK), :] >= thr)

    _flash_chunks(0, n_att, ATT_CHUNK, make_bias, k_ref, vt_ref, refs)
    _finish_flash(o_ref, acc_ref)


DSA_QI_OFF = ATTN_WIDTH
DSA_K_OFF = DSA_QI_OFF + IDX_HEADS * IDX_DIM
DSA_KI_OFF = DSA_K_OFF + KDUP_WIDTH
DSA_WI_OFF = DSA_KI_OFF + LANES
DSA_V_OFF = DSA_WI_OFF + LANES
DSA_WIDTH = DSA_V_OFF + KV_WIDTH
DSA_TN = 2048


def _dsa_attention(proj, wi, vt, batch, seq):
    nb = seq // BLOCK
    n = batch * seq
    topk = min(TOPK_MAX, seq // 4)
    qi_w = IDX_HEADS * IDX_DIM
    chunk = max(IDX_CHUNK, ATT_CHUNK)
    seq_pad = -(-seq // chunk) * chunk
    return pl.pallas_call(
        functools.partial(_dsa_kernel, topk=topk),
        out_shape=jax.ShapeDtypeStruct((n, ATTN_WIDTH), BF16),
        grid=(batch, nb),
        in_specs=[
            pl.BlockSpec((BLOCK, ATTN_WIDTH), lambda b, i: (b * nb + i, 0)),
            pl.BlockSpec((BLOCK, qi_w), lambda b, i: (b * nb + i, DSA_QI_OFF // qi_w)),
            pl.BlockSpec((BLOCK, LANES), lambda b, i: (b * nb + i, 0)),
            pl.BlockSpec((seq, LANES), lambda b, i: (b, DSA_KI_OFF // LANES)),
            pl.BlockSpec((seq, KDUP_WIDTH), lambda b, i: (b, DSA_K_OFF // KDUP_WIDTH)),
            pl.BlockSpec((None, KV_WIDTH, seq), lambda b, i: (b, 0, 0)),
        ],
        out_specs=pl.BlockSpec((BLOCK, ATTN_WIDTH), lambda b, i: (b * nb + i, 0)),
        scratch_shapes=[pltpu.VMEM((seq_pad, BLOCK), F32)] + _flash_scratch(ATT_CHUNK),
        compiler_params=_cparams(("parallel", "arbitrary")),
        name="dsa_attention",
    )(proj, proj, wi, proj, proj, vt)


def _rope_table(positions):
    inv = ROPE_THETA ** (-jnp.arange(0, ROPE_DIM, 2, dtype=F32) / ROPE_DIM)
    ang = positions.astype(F32).reshape(-1, 1) * inv[None, :]
    cos, sin = jnp.cos(ang), jnp.sin(ang)
    n = ang.shape[0]
    pad = HEAD_DIM - ROPE_DIM
    cos_h = jnp.concatenate([cos, cos, jnp.ones((n, pad), F32)], axis=1)
    lo_h = jnp.concatenate([-sin, jnp.zeros((n, HEAD_DIM - ROPE_HALF), F32)], axis=1)
    hi_h = jnp.concatenate([jnp.zeros((n, ROPE_HALF), F32), sin, jnp.zeros((n, pad), F32)], axis=1)
    reps = LANES // HEAD_DIM
    return jnp.concatenate([jnp.tile(cos_h, (1, reps)), jnp.tile(lo_h, (1, reps)),
                            jnp.tile(hi_h, (1, reps))], axis=1)


def _dup_heads(wk):
    d = wk.shape[0]
    w4 = wk.reshape(d, N_KV_HEADS, 1, HEAD_DIM)
    return jnp.broadcast_to(w4, (d, N_KV_HEADS, LANES // HEAD_DIM, HEAD_DIM)).reshape(d, KDUP_WIDTH)


QKV_WIDTH_EXT = ATTN_WIDTH + KDUP_WIDTH + KV_WIDTH


def _qkv_weight(w_in):
    o = ATTN_WIDTH
    return jnp.concatenate([w_in[:, :o], _dup_heads(w_in[:, o:o + KV_WIDTH]),
                            w_in[:, o + KV_WIDTH:o + 2 * KV_WIDTH]], axis=1)


def _qkv_colscale():
    return jnp.concatenate([jnp.full((1, ATTN_WIDTH), SCALE * LOG2E, F32),
                            jnp.ones((1, KDUP_WIDTH + KV_WIDTH), F32)], axis=1)


def _dsa_weight(w_in):
    d = w_in.shape[0]
    o = ATTN_WIDTH
    wq = w_in[:, :o]
    wk = w_in[:, o:o + KV_WIDTH]
    wv = w_in[:, o + KV_WIDTH:o + 2 * KV_WIDTH]
    o += 2 * KV_WIDTH
    wqi = w_in[:, o:o + IDX_HEADS * IDX_DIM]
    o += IDX_HEADS * IDX_DIM
    wki = w_in[:, o:o + IDX_DIM]
    o += IDX_DIM
    wwi = w_in[:, o:o + IDX_HEADS]
    z = lambda c: jnp.zeros((d, c), w_in.dtype)
    return jnp.concatenate([wq, wqi, _dup_heads(wk), wki, z(LANES - IDX_DIM),
                            wwi, z(LANES - IDX_HEADS), wv], axis=1)


def _dsa_colscale():
    return jnp.concatenate([jnp.full((1, ATTN_WIDTH), SCALE * LOG2E, F32),
                            jnp.full((1, IDX_HEADS * IDX_DIM), IDX_SCALE, F32),
                            jnp.ones((1, DSA_WIDTH - DSA_K_OFF), F32)], axis=1)


def _v_transposed(proj, v_off, batch, seq):
    v = proj[:, v_off:v_off + KV_WIDTH].reshape(batch, seq, KV_WIDTH)
    return jnp.swapaxes(v, 1, 2)


def kernel(x, positions, norm_attn, norm_mlp, w_up, w_down, final_norm,
           a_w_in, a_sinks, a_w_out, b_w_in, b_w_out, c_w_in, c_w_out):
    batch, seq, d = x.shape
    depth = norm_attn.shape[0]
    x2 = x.reshape(batch * seq, d)
    rope_tab = _rope_table(positions)
    qkv_scale = _qkv_colscale()
    qkv_rope_groups = (ATTN_WIDTH + KDUP_WIDTH) // LANES
    for i in range(depth):
        j, kind = divmod(i, 3)
        if kind == 1:
            proj, wi = _norm_proj(x2, norm_attn[i], _dsa_weight(b_w_in[j]).astype(BF16),
                                  _dsa_colscale(), rope_tab, tn=DSA_TN,
                                  n_rope_groups=DSA_WI_OFF // LANES,
                                  aux_group=(DSA_WI_OFF % DSA_TN) // LANES)
            o = _dsa_attention(proj, wi, _v_transposed(proj, DSA_V_OFF, batch, seq), batch, seq)
            w_out = b_w_out[j]
        else:
            w_in = a_w_in[j] if kind == 0 else c_w_in[j]
            q, k, v = _norm_proj(x2, norm_attn[i], _qkv_weight(w_in).astype(BF16), qkv_scale, rope_tab,
                                 tn=QKV_WIDTH_EXT, n_rope_groups=qkv_rope_groups,
                                 out_widths=(ATTN_WIDTH, KDUP_WIDTH, KV_WIDTH))
            if kind == 0:
                o = _window_attention(q, k, v, a_sinks[j], batch, seq, dil=1, span=SWA_WINDOW, emit_lse=False)
                w_out = a_w_out[j]
            else:
                o = _dilated_attention(q, k, v, batch, seq)
                w_out = c_w_out[j]
        x2 = _out_proj(o, w_out.astype(BF16), x2)
        x2 = _mlp(x2, norm_mlp[i], w_up[i].astype(BF16), w_down[i].astype(BF16),
                  final_norm if i == depth - 1 else None)
    return x2.reshape(batch, seq, d)
```

```python
import functools

import jax
import jax.numpy as jnp
from jax import lax
from jax.experimental import pallas as pl
from jax.experimental.pallas import tpu as pltpu

HEAD_DIM = 64
N_KV_HEADS = 4
GROUP = 8
N_HEADS = N_KV_HEADS * GROUP
ATTN_WIDTH = N_HEADS * HEAD_DIM
KV_WIDTH = N_KV_HEADS * HEAD_DIM
ROPE_DIM = HEAD_DIM // 4
ROPE_HALF = ROPE_DIM // 2
ROPE_THETA = 500000.0
SCALE = HEAD_DIM ** -0.5
BLOCK = 128
SWA_WINDOW = 128
IDX_HEADS = 16
IDX_DIM = 64
IDX_SCALE = IDX_DIM ** -0.5
IDX_W_SCALE = IDX_HEADS ** -0.5
TOPK_MAX = 256
DILATED_BRANCHES = ((128, 1), (512, 4), (2048, 16))
NORM_EPS = 1e-5
NEG_INF = -1e30

LANES = 128
BF16_ROWS = 16
SUBLANES = 8
MXU_TILE = 256
LOG2E = 1.4426950408889634
VMEM_LIMIT = 52 * 1024 * 1024

BF16 = jnp.bfloat16
F32 = jnp.float32

KDUP_WIDTH = N_KV_HEADS * LANES
QROWS = GROUP * BLOCK
EXT = 2 * LANES
VT_ROWS = HEAD_DIM + BF16_ROWS


def _cparams(sem):
    return pltpu.CompilerParams(dimension_semantics=sem, vmem_limit_bytes=VMEM_LIMIT)


def _norm_proj_kernel(x_ref, g_ref, w_ref, cs_ref, rope_ref, *rest,
                      n_rope_groups, groups_per_tile, out_groups, aux_group):
    o_refs = rest[:len(out_groups)]
    aux_ref = rest[len(out_groups)] if aux_group is not None else None
    h_ref = rest[-1]
    j = pl.program_id(1)

    @pl.when(j == 0)
    def _():
        x = x_ref[...]
        ms = jnp.mean(x * x, axis=-1, keepdims=True)
        h_ref[...] = ((x * lax.rsqrt(ms + NORM_EPS)) * g_ref[...]).astype(BF16)

    cos_t = rope_ref[:, 0:LANES]
    sin_lo = rope_ref[:, LANES:2 * LANES]
    sin_hi = rope_ref[:, 2 * LANES:3 * LANES]
    h = h_ref[...]
    sub_groups = MXU_TILE // LANES
    n_sub = groups_per_tile // sub_groups
    dest = [(o_ref, k) for o_ref, cnt in zip(o_refs, out_groups) for k in range(cnt)]

    def project(s):
        cols = slice(s * MXU_TILE, (s + 1) * MXU_TILE)
        return jnp.dot(h, w_ref[:, cols], preferred_element_type=F32) * cs_ref[:, cols]

    def finish(s, acc):
        for gg in range(sub_groups):
            g = s * sub_groups + gg
            a = acc[:, gg * LANES:(gg + 1) * LANES]
            r = (a * cos_t + pltpu.roll(a, LANES - ROPE_HALF, 1) * sin_lo
                 + pltpu.roll(a, ROPE_HALF, 1) * sin_hi)
            is_rope = (j * groups_per_tile + g) < n_rope_groups
            o_ref, k = dest[g]
            o_ref[:, k * LANES:(k + 1) * LANES] = jnp.where(is_rope, r, a).astype(o_ref.dtype)
            if aux_ref is not None and g == aux_group:
                aux_ref[...] = a

    acc = project(0)
    for s in range(1, n_sub):
        nxt = project(s)
        finish(s - 1, acc)
        acc = nxt
    finish(n_sub - 1, acc)


def _norm_proj(x2, g, w, colscale, rope_tab, *, tn, n_rope_groups, aux_group=None, out_widths=None, tm=512):
    n, d = x2.shape
    width = w.shape[1]
    gpt = tn // LANES
    if out_widths is None:
        out_widths = (tn,)
        out_shape = [jax.ShapeDtypeStruct((n, width), BF16)]
        out_specs = [pl.BlockSpec((tm, tn), lambda i, j: (i, j))]
    else:
        assert tn == width == sum(out_widths)
        out_shape = [jax.ShapeDtypeStruct((n, ow), BF16) for ow in out_widths]
        out_specs = [pl.BlockSpec((tm, ow), lambda i, j: (i, 0)) for ow in out_widths]
    kern = functools.partial(_norm_proj_kernel, n_rope_groups=n_rope_groups, groups_per_tile=gpt,
                             out_groups=tuple(ow // LANES for ow in out_widths), aux_group=aux_group)
    if aux_group is not None:
        out_shape.append(jax.ShapeDtypeStruct((n, LANES), F32))
        out_specs.append(pl.BlockSpec((tm, LANES), lambda i, j: (i, 0)))
    res = pl.pallas_call(
        kern,
        out_shape=out_shape,
        grid=(n // tm, width // tn),
        in_specs=[
            pl.BlockSpec((tm, d), lambda i, j: (i, 0)),
            pl.BlockSpec((1, d), lambda i, j: (0, 0)),
            pl.BlockSpec((d, tn), lambda i, j: (0, j)),
            pl.BlockSpec((1, tn), lambda i, j: (0, j)),
            pl.BlockSpec((tm, 3 * LANES), lambda i, j: (i, 0)),
        ],
        out_specs=out_specs,
        scratch_shapes=[pltpu.VMEM((tm, d), BF16)],
        compiler_params=_cparams(("parallel", "arbitrary")),
        name="norm_proj",
    )(x2, g.reshape(1, d), w, colscale, rope_tab)
    return res


def _out_proj_kernel(o_ref, w_ref, x_ref, y_ref):
    y_ref[...] = x_ref[...] + jnp.dot(o_ref[...], w_ref[...], preferred_element_type=F32)


def _out_proj(o, w, x2, *, tm=512, tn=2048):
    n, k = o.shape
    d = w.shape[1]
    return pl.pallas_call(
        _out_proj_kernel,
        out_shape=jax.ShapeDtypeStruct((n, d), F32),
        grid=(n // tm, d // tn),
        in_specs=[
            pl.BlockSpec((tm, k), lambda i, j: (i, 0)),
            pl.BlockSpec((k, tn), lambda i, j: (0, j)),
            pl.BlockSpec((tm, tn), lambda i, j: (i, j)),
        ],
        out_specs=pl.BlockSpec((tm, tn), lambda i, j: (i, j)),
        compiler_params=_cparams(("parallel", "arbitrary")),
        name="out_proj",
    )(o, w, x2)


def _rms(x, g):
    ms = jnp.mean(x * x, axis=-1, keepdims=True)
    return (x * lax.rsqrt(ms + NORM_EPS)) * g


def _mlp_kernel(x_ref, g_ref, wu_ref, wd_ref, *rest, out_norm):
    y_ref, h_ref = rest[-2:]
    f = pl.program_id(1)

    @pl.when(f == 0)
    def _():
        x = x_ref[...]
        h_ref[...] = _rms(x, g_ref[...]).astype(BF16)
        y_ref[...] = x

    u = jnp.dot(h_ref[...], wu_ref[...], preferred_element_type=F32)
    u = jnp.maximum(u, 0.0)
    a = (u * u).astype(BF16)
    y_ref[...] += jnp.dot(a, wd_ref[...], preferred_element_type=F32)

    if out_norm:
        @pl.when(f == pl.num_programs(1) - 1)
        def _():
            y_ref[...] = _rms(y_ref[...], rest[0][...])


def _mlp(x2, g, w_up, w_down, out_gain=None, *, tm=512, tf=1024):
    n, d = x2.shape
    d_ff = w_up.shape[1]
    vec = pl.BlockSpec((1, d), lambda i, f: (0, 0))
    in_specs = [
        pl.BlockSpec((tm, d), lambda i, f: (i, 0)),
        vec,
        pl.BlockSpec((d, tf), lambda i, f: (0, f)),
        pl.BlockSpec((tf, d), lambda i, f: (f, 0)),
    ]
    args = [x2, g.reshape(1, d), w_up, w_down]
    if out_gain is not None:
        in_specs.append(vec)
        args.append(out_gain.reshape(1, d))
    return pl.pallas_call(
        functools.partial(_mlp_kernel, out_norm=out_gain is not None),
        out_shape=jax.ShapeDtypeStruct((n, d), F32),
        grid=(n // tm, d_ff // tf),
        in_specs=in_specs,
        out_specs=pl.BlockSpec((tm, d), lambda i, f: (i, 0)),
        scratch_shapes=[pltpu.VMEM((tm, d), BF16)],
        compiler_params=_cparams(("parallel", "arbitrary")),
        name="mlp",
    )(*args)


def _fill_q_ext(q_ref, qx_ref, row0=0):
    lane = lax.broadcasted_iota(jnp.int32, (BLOCK, LANES), 1)
    row = lax.broadcasted_iota(jnp.int32, (BLOCK, LANES), 0)
    eye = jnp.where(lane == row, 1.0, 0.0).astype(BF16)
    low = lane < HEAD_DIM
    for kv in range(N_KV_HEADS):
        for g in range(GROUP):
            h = kv * GROUP + g
            tile = q_ref[row0:row0 + BLOCK, (h // 2) * LANES:(h // 2 + 1) * LANES]
            keep = low if h % 2 == 0 else jnp.logical_not(low)
            qx_ref[kv, g * BLOCK:(g + 1) * BLOCK, 0:LANES] = jnp.where(keep, tile, jnp.zeros_like(tile))
            qx_ref[kv, g * BLOCK:(g + 1) * BLOCK, LANES:EXT] = eye


def _scores_t(k_tile, bias, qx):
    k_ext = jnp.concatenate([k_tile, bias], axis=1)
    return lax.dot_general(k_ext, qx, (((1,), (1,)), ((), ())), preferred_element_type=F32)


def _vt_ext(vt):
    return jnp.concatenate([vt, jnp.ones((BF16_ROWS, vt.shape[1]), BF16)], axis=0)


def _flash_chunks(c_lo, c_end, tc, make_bias, k_ref, vt_ref, refs):
    qx_ref, s_ref, cm_ref, m_ref, acc_ref, bias_ref = refs
    m_ref[...] = jnp.full(m_ref.shape, NEG_INF, F32)
    acc_ref[...] = jnp.zeros(acc_ref.shape, F32)
    c_last = c_end - 1

    def put_bias(c):
        bias_ref[c & 1] = make_bias(c)

    def issue_scores(c, kv):
        start = pl.multiple_of(c * tc, tc)
        k_tile = k_ref[pl.ds(start, tc), kv * LANES:(kv + 1) * LANES]
        s_t = _scores_t(k_tile, bias_ref[c & 1], qx_ref[kv])
        s_ref[kv % 2] = s_t
        cm_ref[kv % 2] = jnp.max(s_t, axis=0, keepdims=True)

    def consume(c, kv):
        slot = kv % 2
        start = pl.multiple_of(c * tc, tc)
        vt = _vt_ext(vt_ref[kv * HEAD_DIM:(kv + 1) * HEAD_DIM, pl.ds(start, tc)])
        m_old = m_ref[kv]
        m_new = jnp.maximum(m_old, cm_ref[slot])
        alpha = jnp.exp2(m_old - m_new)
        m_ref[kv] = m_new
        for n in range(QROWS // MXU_TILE):
            cols = slice(n * MXU_TILE, (n + 1) * MXU_TILE)
            part = alpha[:, cols] * acc_ref[kv, :, cols]
            for kk in range(tc // MXU_TILE):
                rows = slice(kk * MXU_TILE, (kk + 1) * MXU_TILE)
                p = jnp.exp2(s_ref[slot, rows, cols] - m_new[:, cols])
                part = part + jnp.dot(vt[:, rows], p.astype(BF16), preferred_element_type=F32)
            acc_ref[kv, :, cols] = part

    put_bias(c_lo)
    issue_scores(c_lo, 0)

    def body(c, carry):
        c_next = jnp.minimum(c + 1, c_last)
        put_bias(c_next)
        for kv in range(N_KV_HEADS):
            if kv + 1 < N_KV_HEADS:
                issue_scores(c, kv + 1)
            else:
                issue_scores(c_next, 0)
            consume(c, kv)
        return carry

    lax.fori_loop(c_lo, c_end, body, 0)


def _store_out(o_ref, kv, o_t, row0=0):
    for gp in range(GROUP // 2):
        pair = jnp.concatenate([o_t[:, (2 * gp) * BLOCK:(2 * gp + 1) * BLOCK],
                                o_t[:, (2 * gp + 1) * BLOCK:(2 * gp + 2) * BLOCK]], axis=0)
        col = (kv * GROUP + 2 * gp) * HEAD_DIM
        o_ref[row0:row0 + BLOCK, col:col + LANES] = pair.T.astype(o_ref.dtype)


def _finish_flash(o_ref, acc_ref):
    for kv in range(N_KV_HEADS):
        acc = acc_ref[kv]
        _store_out(o_ref, kv, acc[0:HEAD_DIM, :] / acc[HEAD_DIM:HEAD_DIM + 1, :])


def _bias_of(valid):
    return jnp.where(valid, 0.0, NEG_INF).astype(BF16)


def _qx_scratch():
    return pltpu.VMEM((N_KV_HEADS, QROWS, EXT), BF16)


def _flash_scratch(tc):
    return [
        _qx_scratch(),
        pltpu.VMEM((2, tc, QROWS), F32),
        pltpu.VMEM((2, 1, QROWS), F32),
        pltpu.VMEM((N_KV_HEADS, 1, QROWS), F32),
        pltpu.VMEM((N_KV_HEADS, VT_ROWS, QROWS), F32),
        pltpu.VMEM((2, tc, BLOCK), BF16),
    ]


WIN_TILES = 1


def _window_kernel(*refs, span, with_sinks, emit_lse):
    refs = list(refs)
    sink_ref = refs.pop(0) if with_sinks else None
    q_ref, kp_ref, kc_ref, vp_ref, vc_ref, o_ref = refs[:6]
    lse_ref = refs[6] if emit_lse else None
    qx_ref = refs[-1]
    first = pl.program_id(2) == 0
    key = lax.broadcasted_iota(jnp.int32, (2 * BLOCK, BLOCK), 0)
    qry = lax.broadcasted_iota(jnp.int32, (2 * BLOCK, BLOCK), 1)
    dist = qry + BLOCK - key
    band = (dist >= 0) & (dist < span)
    bias_any = _bias_of(band)
    bias_first = _bias_of(band & ((key >= BLOCK) | jnp.logical_not(first)))

    def v_t(block):
        return block.astype(F32).T.astype(BF16)

    ones = jnp.ones((BF16_ROWS, 2 * BLOCK), BF16)
    vts = [v_t(vp_ref[...])] + [v_t(vc_ref[j * BLOCK:(j + 1) * BLOCK, :]) for j in range(WIN_TILES)]

    def issue_scores(j):
        _fill_q_ext(q_ref, qx_ref.at[j % 2], j * BLOCK)
        out = []
        for kv in range(N_KV_HEADS):
            lanes = slice(kv * LANES, (kv + 1) * LANES)
            k_prev = kp_ref[:, lanes] if j == 0 else kc_ref[(j - 1) * BLOCK:j * BLOCK, lanes]
            k_tile = jnp.concatenate([k_prev, kc_ref[j * BLOCK:(j + 1) * BLOCK, lanes]], axis=0)
            out.append(_scores_t(k_tile, bias_first if j == 0 else bias_any, qx_ref[j % 2, kv]))
        return out

    def consume(j, scores):
        vt2 = jnp.concatenate([vts[j], vts[j + 1]], axis=1)
        lse_rows = []
        for kv in range(N_KV_HEADS):
            s_t = scores[kv]
            m = jnp.max(s_t, axis=0, keepdims=True)
            if with_sinks:
                sink = jnp.concatenate(
                    [jnp.full((1, BLOCK), sink_ref[kv * GROUP + g] * LOG2E, F32) for g in range(GROUP)], axis=1)
                m = jnp.maximum(m, sink)
            p = jnp.exp2(s_t - m).astype(BF16)
            vt = jnp.concatenate([vt2[kv * HEAD_DIM:(kv + 1) * HEAD_DIM, :], ones], axis=0)
            acc = jnp.dot(vt, p, preferred_element_type=F32)
            den = acc[HEAD_DIM:HEAD_DIM + 1, :]
            if with_sinks:
                den = den + jnp.exp2(sink - m)
            _store_out(o_ref, kv, acc[0:HEAD_DIM, :] / den, j * BLOCK)
            if emit_lse:
                lse = m + jnp.log2(den)
                lse_rows += [lse[:, g * BLOCK:(g + 1) * BLOCK] for g in range(GROUP)]
        if emit_lse:
            pad = jnp.zeros((BLOCK - N_HEADS, BLOCK), F32)
            lse_ref[j * BLOCK:(j + 1) * BLOCK, :] = jnp.concatenate(lse_rows + [pad], axis=0).T

    scores = issue_scores(0)
    for j in range(WIN_TILES):
        nxt = issue_scores(j + 1) if j + 1 < WIN_TILES else None
        consume(j, scores)
        scores = nxt


def _by_residue(a, batch, seq, dil):
    w = a.shape[1]
    if dil == 1:
        return a.reshape(batch, 1, seq, w)
    return jnp.swapaxes(a.reshape(batch, seq // dil, dil, w), 1, 2)


def _by_token(a, batch, seq, dil):
    w = a.shape[-1]
    if dil == 1:
        return a.reshape(batch * seq, w)
    return jnp.swapaxes(a, 1, 2).reshape(batch * seq, w)


def _window_attention(q, k, v, sinks, batch, seq, *, dil, span, emit_lse):
    sub = seq // dil
    rows = WIN_TILES * BLOCK
    with_sinks = sinks is not None
    cur = lambda b, r, u: (b, r, u, 0)
    prev = lambda b, r, u: (b, r, jnp.maximum(u * WIN_TILES - 1, 0), 0)
    in_specs = [
        pl.BlockSpec((None, None, rows, ATTN_WIDTH), cur),
        pl.BlockSpec((None, None, BLOCK, KDUP_WIDTH), prev),
        pl.BlockSpec((None, None, rows, KDUP_WIDTH), cur),
        pl.BlockSpec((None, None, BLOCK, KV_WIDTH), prev),
        pl.BlockSpec((None, None, rows, KV_WIDTH), cur),
    ]
    qr, kr, vr = (_by_residue(a, batch, seq, dil) for a in (q, k, v))
    args = [qr, kr, kr, vr, vr]
    if with_sinks:
        in_specs.insert(0, pl.BlockSpec(memory_space=pltpu.SMEM))
        args.insert(0, sinks)
    out_shape = [jax.ShapeDtypeStruct((batch, dil, sub, ATTN_WIDTH), BF16)]
    out_specs = [pl.BlockSpec((None, None, rows, ATTN_WIDTH), cur)]
    if emit_lse:
        out_shape.append(jax.ShapeDtypeStruct((batch, dil, sub, LANES), F32))
        out_specs.append(pl.BlockSpec((None, None, rows, LANES), cur))
    res = pl.pallas_call(
        functools.partial(_window_kernel, span=span, with_sinks=with_sinks, emit_lse=emit_lse),
        out_shape=out_shape,
        grid=(batch, dil, sub // rows),
        in_specs=in_specs,
        out_specs=out_specs,
        scratch_shapes=[pltpu.VMEM((2, N_KV_HEADS, QROWS, EXT), BF16)],
        compiler_params=_cparams(("parallel", "parallel", "arbitrary")),
        name="window_attention",
    )(*args)
    o = _by_token(res[0], batch, seq, dil)
    return (o, _by_token(res[1], batch, seq, dil)) if emit_lse else o


def _merge_kernel(*refs):
    n_br = (len(refs) - 2) // 2
    o_refs, l_refs, e_ref, out_ref = refs[:n_br], refs[n_br:2 * n_br], refs[-2], refs[-1]
    lses = [l[...] for l in l_refs]
    top = functools.reduce(jnp.maximum, lses)
    ws = [jnp.exp2(l - top) for l in lses]
    tot = functools.reduce(lambda a, b: a + b, ws)
    out = None
    for w, o_ref in zip(ws, o_refs):
        wn = w / tot
        hi = wn.astype(BF16)
        lo = (wn - hi.astype(F32)).astype(BF16)
        spread = (jnp.dot(hi, e_ref[...], preferred_element_type=F32)
                  + jnp.dot(lo, e_ref[...], preferred_element_type=F32))
        term = spread * o_ref[...].astype(F32)
        out = term if out is None else out + term
    out_ref[...] = out.astype(out_ref.dtype)


def _merge_branches(outs, lses, *, tm=512):
    n = outs[0].shape[0]
    head_of_lane = jnp.arange(ATTN_WIDTH, dtype=jnp.int32) // HEAD_DIM
    expand = (jnp.arange(LANES, dtype=jnp.int32)[:, None] == head_of_lane[None, :]).astype(BF16)
    row = lambda width: pl.BlockSpec((tm, width), lambda i: (i, 0))
    return pl.pallas_call(
        _merge_kernel,
        out_shape=jax.ShapeDtypeStruct((n, ATTN_WIDTH), BF16),
        grid=(n // tm,),
        in_specs=[row(ATTN_WIDTH)] * len(outs) + [row(LANES)] * len(lses)
                 + [pl.BlockSpec((LANES, ATTN_WIDTH), lambda i: (0, 0))],
        out_specs=row(ATTN_WIDTH),
        compiler_params=_cparams(("parallel",)),
        name="merge_branches",
    )(*outs, *lses, expand)


def _dilated_attention(q, k, v, batch, seq):
    outs, lses = [], []
    for window, dil in DILATED_BRANCHES:
        o, lse = _window_attention(q, k, v, None, batch, seq, dil=dil, span=window // dil + 1, emit_lse=True)
        outs.append(o)
        lses.append(lse)
    return _merge_branches(outs, lses)


IDX_CHUNK = 512
ATT_CHUNK = 512
INT_MIN = -2 ** 31
F32_BITS = 32
BITS_PER_CHECK = 4


def _sortable_to_f32(t):
    bits = jnp.where(t >= 0, t, t ^ jnp.int32(0x7FFFFFFF))
    return lax.bitcast_convert_type(bits, F32)


def _dsa_kernel(q_ref, qi_ref, wi_ref, ki_ref, k_ref, vt_ref, o_ref, sc_ref, *refs, topk):
    qx_ref, acc_ref = refs[0], refs[4]
    i = pl.program_id(1)
    t0 = i * BLOCK
    n_idx = (t0 + BLOCK + IDX_CHUNK - 1) // IDX_CHUNK
    n_att = (t0 + BLOCK + ATT_CHUNK - 1) // ATT_CHUNK

    qis = jnp.concatenate(
        [qi_ref[:, h * IDX_DIM:(h + 1) * IDX_DIM] for h in range(IDX_HEADS)], axis=0)
    w_t = (wi_ref[...] * IDX_W_SCALE).T
    key = lax.broadcasted_iota(jnp.int32, (IDX_CHUNK, BLOCK), 0)
    qry = lax.broadcasted_iota(jnp.int32, (IDX_CHUNK, BLOCK), 1)

    def idx_body(c, carry):
        start = pl.multiple_of(c * IDX_CHUNK, IDX_CHUNK)
        kic = ki_ref[pl.ds(start, IDX_CHUNK), 0:IDX_DIM]
        rel = jnp.maximum(
            lax.dot_general(kic, qis, (((1,), (1,)), ((), ())), preferred_element_type=F32), 0.0)
        score = jnp.zeros((IDX_CHUNK, BLOCK), F32)
        for h in range(IDX_HEADS):
            score = score + rel[:, h * BLOCK:(h + 1) * BLOCK] * w_t[h:h + 1, :]
        sc_ref[pl.ds(start, IDX_CHUNK), :] = jnp.where(start + key <= t0 + qry, score, NEG_INF)
        return carry

    lax.fori_loop(0, n_idx, idx_body, 0)

    def count_ge(cand_f):
        def cbody(c, cnt):
            start = pl.multiple_of(c * IDX_CHUNK, IDX_CHUNK)
            hit = jnp.where(sc_ref[pl.ds(start, IDX_CHUNK), :] >= cand_f, 1.0, 0.0)
            parts = [hit[r * SUBLANES:(r + 1) * SUBLANES, :] for r in range(IDX_CHUNK // SUBLANES)]
            while len(parts) > 1:
                parts = [a + b for a, b in zip(parts[0::2], parts[1::2])]
            return cnt + parts[0]
        cnt = lax.fori_loop(0, n_idx, cbody, jnp.zeros((SUBLANES, BLOCK), F32))
        return jnp.sum(cnt, axis=0, keepdims=True)

    def bit_cond(state):
        b, _, _, n_open = state
        return (b < F32_BITS) & (n_open > 0)

    def bit_body(state):
        b0, t, done, _ = state
        for k in range(BITS_PER_CHECK):
            b = b0 + k
            bit = lax.shift_left(jnp.int32(1), F32_BITS - 1 - b)
            cand = jnp.where(b == 0, jnp.zeros_like(t), t | bit)
            cnt = count_ge(_sortable_to_f32(cand))
            take = (cnt >= float(topk)) & (done == 0)
            t = jnp.where(take, cand, t)
            done = jnp.where(take & (cnt == float(topk)), 1, done)
        n_open = jnp.sum(1 - done)
        return b0 + BITS_PER_CHECK, t, done, n_open

    state = (jnp.int32(0), jnp.full((1, BLOCK), INT_MIN, jnp.int32),
             jnp.zeros((1, BLOCK), jnp.int32), jnp.int32(BLOCK))
    _, t_int, _, _ = lax.while_loop(bit_cond, bit_body, state)
    thr = jnp.maximum(_sortable_to_f32(t_int), jnp.float32(NEG_INF * 0.5))

    _fill_q_ext(q_ref, qx_ref)

    def make_bias(c):
        start = pl.multiple_of(c * ATT_CHUNK, ATT_CHUNK)
        return _bias_of(sc_ref[pl.ds(start, ATT_CHUNK), :] >= thr)

    _flash_chunks(0, n_att, ATT_CHUNK, make_bias, k_ref, vt_ref, refs)
    _finish_flash(o_ref, acc_ref)


DSA_QI_OFF = ATTN_WIDTH
DSA_K_OFF = DSA_QI_OFF + IDX_HEADS * IDX_DIM
DSA_KI_OFF = DSA_K_OFF + KDUP_WIDTH
DSA_WI_OFF = DSA_KI_OFF + LANES
DSA_V_OFF = DSA_WI_OFF + LANES
DSA_WIDTH = DSA_V_OFF + KV_WIDTH
DSA_TN = 2048


def _dsa_attention(proj, wi, vt, batch, seq):
    nb = seq // BLOCK
    n = batch * seq
    topk = min(TOPK_MAX, seq // 4)
    qi_w = IDX_HEADS * IDX_DIM
    chunk = max(IDX_CHUNK, ATT_CHUNK)
    seq_pad = -(-seq // chunk) * chunk
    return pl.pallas_call(
        functools.partial(_dsa_kernel, topk=topk),
        out_shape=jax.ShapeDtypeStruct((n, ATTN_WIDTH), BF16),
        grid=(batch, nb),
        in_specs=[
            pl.BlockSpec((BLOCK, ATTN_WIDTH), lambda b, i: (b * nb + i, 0)),
            pl.BlockSpec((BLOCK, qi_w), lambda b, i: (b * nb + i, DSA_QI_OFF // qi_w)),
            pl.BlockSpec((BLOCK, LANES), lambda b, i: (b * nb + i, 0)),
            pl.BlockSpec((seq, LANES), lambda b, i: (b, DSA_KI_OFF // LANES)),
            pl.BlockSpec((seq, KDUP_WIDTH), lambda b, i: (b, DSA_K_OFF // KDUP_WIDTH)),
            pl.BlockSpec((None, KV_WIDTH, seq), lambda b, i: (b, 0, 0)),
        ],
        out_specs=pl.BlockSpec((BLOCK, ATTN_WIDTH), lambda b, i: (b * nb + i, 0)),
        scratch_shapes=[pltpu.VMEM((seq_pad, BLOCK), F32)] + _flash_scratch(ATT_CHUNK),
        compiler_params=_cparams(("parallel", "arbitrary")),
        name="dsa_attention",
    )(proj, proj, wi, proj, proj, vt)


def _rope_table(positions):
    inv = ROPE_THETA ** (-jnp.arange(0, ROPE_DIM, 2, dtype=F32) / ROPE_DIM)
    ang = positions.astype(F32).reshape(-1, 1) * inv[None, :]
    cos, sin = jnp.cos(ang), jnp.sin(ang)
    n = ang.shape[0]
    pad = HEAD_DIM - ROPE_DIM
    cos_h = jnp.concatenate([cos, cos, jnp.ones((n, pad), F32)], axis=1)
    lo_h = jnp.concatenate([-sin, jnp.zeros((n, HEAD_DIM - ROPE_HALF), F32)], axis=1)
    hi_h = jnp.concatenate([jnp.zeros((n, ROPE_HALF), F32), sin, jnp.zeros((n, pad), F32)], axis=1)
    reps = LANES // HEAD_DIM
    return jnp.concatenate([jnp.tile(cos_h, (1, reps)), jnp.tile(lo_h, (1, reps)),
                            jnp.tile(hi_h, (1, reps))], axis=1)


def _dup_heads(wk):
    d = wk.shape[0]
    w4 = wk.reshape(d, N_KV_HEADS, 1, HEAD_DIM)
    return jnp.broadcast_to(w4, (d, N_KV_HEADS, LANES // HEAD_DIM, HEAD_DIM)).reshape(d, KDUP_WIDTH)


QKV_WIDTH_EXT = ATTN_WIDTH + KDUP_WIDTH + KV_WIDTH


def _qkv_weight(w_in):
    o = ATTN_WIDTH
    return jnp.concatenate([w_in[:, :o], _dup_heads(w_in[:, o:o + KV_WIDTH]),
                            w_in[:, o + KV_WIDTH:o + 2 * KV_WIDTH]], axis=1)


def _qkv_colscale():
    return jnp.concatenate([jnp.full((1, ATTN_WIDTH), SCALE * LOG2E, F32),
                            jnp.ones((1, KDUP_WIDTH + KV_WIDTH), F32)], axis=1)


def _dsa_weight(w_in):
    d = w_in.shape[0]
    o = ATTN_WIDTH
    wq = w_in[:, :o]
    wk = w_in[:, o:o + KV_WIDTH]
    wv = w_in[:, o + KV_WIDTH:o + 2 * KV_WIDTH]
    o += 2 * KV_WIDTH
    wqi = w_in[:, o:o + IDX_HEADS * IDX_DIM]
    o += IDX_HEADS * IDX_DIM
    wki = w_in[:, o:o + IDX_DIM]
    o += IDX_DIM
    wwi = w_in[:, o:o + IDX_HEADS]
    z = lambda c: jnp.zeros((d, c), w_in.dtype)
    return jnp.concatenate([wq, wqi, _dup_heads(wk), wki, z(LANES - IDX_DIM),
                            wwi, z(LANES - IDX_HEADS), wv], axis=1)


def _dsa_colscale():
    return jnp.concatenate([jnp.full((1, ATTN_WIDTH), SCALE * LOG2E, F32),
                            jnp.full((1, IDX_HEADS * IDX_DIM), IDX_SCALE, F32),
                            jnp.ones((1, DSA_WIDTH - DSA_K_OFF), F32)], axis=1)


def _v_transposed(proj, v_off, batch, seq):
    v = proj[:, v_off:v_off + KV_WIDTH].reshape(batch, seq, KV_WIDTH)
    return jnp.swapaxes(v, 1, 2)


def kernel(x, positions, norm_attn, norm_mlp, w_up, w_down, final_norm,
           a_w_in, a_sinks, a_w_out, b_w_in, b_w_out, c_w_in, c_w_out):
    batch, seq, d = x.shape
    depth = norm_attn.shape[0]
    x2 = x.reshape(batch * seq, d)
    rope_tab = _rope_table(positions)
    qkv_scale = _qkv_colscale()
    qkv_rope_groups = (ATTN_WIDTH + KDUP_WIDTH) // LANES
    for i in range(depth):
        j, kind = divmod(i, 3)
        if kind == 1:
            proj, wi = _norm_proj(x2, norm_attn[i], _dsa_weight(b_w_in[j]).astype(BF16),
                                  _dsa_colscale(), rope_tab, tn=DSA_TN,
                                  n_rope_groups=DSA_WI_OFF // LANES,
                                  aux_group=(DSA_WI_OFF % DSA_TN) // LANES)
            o = _dsa_attention(proj, wi, _v_transposed(proj, DSA_V_OFF, batch, seq), batch, seq)
            w_out = b_w_out[j]
        else:
            w_in = a_w_in[j] if kind == 0 else c_w_in[j]
            q, k, v = _norm_proj(x2, norm_attn[i], _qkv_weight(w_in).astype(BF16), qkv_scale, rope_tab,
                                 tn=QKV_WIDTH_EXT, n_rope_groups=qkv_rope_groups,
                                 out_widths=(ATTN_WIDTH, KDUP_WIDTH, KV_WIDTH))
            if kind == 0:
                o = _window_attention(q, k, v, a_sinks[j], batch, seq, dil=1, span=SWA_WINDOW, emit_lse=False)
                w_out = a_w_out[j]
            else:
                o = _dilated_attention(q, k, v, batch, seq)
                w_out = c_w_out[j]
        x2 = _out_proj(o, w_out.astype(BF16), x2)
        x2 = _mlp(x2, norm_mlp[i], w_up[i].astype(BF16), w_down[i].astype(BF16),
                  final_norm if i == depth - 1 else None)
    return x2.reshape(batch, seq, d)
```

```python
import functools

import jax
import jax.numpy as jnp
from jax import lax
from jax.experimental import pallas as pl
from jax.experimental.pallas import tpu as pltpu

HEAD_DIM = 64
N_KV_HEADS = 4
GROUP = 8
N_HEADS = N_KV_HEADS * GROUP
ATTN_WIDTH = N_HEADS * HEAD_DIM
KV_WIDTH = N_KV_HEADS * HEAD_DIM
ROPE_DIM = HEAD_DIM // 4
ROPE_HALF = ROPE_DIM // 2
ROPE_THETA = 500000.0
SCALE = HEAD_DIM ** -0.5
BLOCK = 128
SWA_WINDOW = 128
IDX_HEADS = 16
IDX_DIM = 64
IDX_SCALE = IDX_DIM ** -0.5
IDX_W_SCALE = IDX_HEADS ** -0.5
TOPK_MAX = 256
DILATED_BRANCHES = ((128, 1), (512, 4), (2048, 16))
NORM_EPS = 1e-5
NEG_INF = -1e30

LANES = 128
BF16_ROWS = 16
SUBLANES = 8
MXU_TILE = 256
LOG2E = 1.4426950408889634
VMEM_LIMIT = 52 * 1024 * 1024

BF16 = jnp.bfloat16
F32 = jnp.float32

KDUP_WIDTH = N_KV_HEADS * LANES
QROWS = GROUP * BLOCK
EXT = 2 * LANES
VT_ROWS = HEAD_DIM + BF16_ROWS


def _cparams(sem):
    return pltpu.CompilerParams(dimension_semantics=sem, vmem_limit_bytes=VMEM_LIMIT)


def _norm_proj_kernel(x_ref, g_ref, w_ref, cs_ref, rope_ref, *rest,
                      n_rope_groups, groups_per_tile, out_groups, aux_group):
    o_refs = rest[:len(out_groups)]
    aux_ref = rest[len(out_groups)] if aux_group is not None else None
    h_ref = rest[-1]
    j = pl.program_id(1)

    @pl.when(j == 0)
    def _():
        x = x_ref[...]
        ms = jnp.mean(x * x, axis=-1, keepdims=True)
        h_ref[...] = ((x * lax.rsqrt(ms + NORM_EPS)) * g_ref[...]).astype(BF16)

    cos_t = rope_ref[:, 0:LANES]
    sin_lo = rope_ref[:, LANES:2 * LANES]
    sin_hi = rope_ref[:, 2 * LANES:3 * LANES]
    h = h_ref[...]
    sub_groups = MXU_TILE // LANES
    n_sub = groups_per_tile // sub_groups
    dest = [(o_ref, k) for o_ref, cnt in zip(o_refs, out_groups) for k in range(cnt)]

    def project(s):
        cols = slice(s * MXU_TILE, (s + 1) * MXU_TILE)
        return jnp.dot(h, w_ref[:, cols], preferred_element_type=F32) * cs_ref[:, cols]

    def finish(s, acc):
        for gg in range(sub_groups):
            g = s * sub_groups + gg
            a = acc[:, gg * LANES:(gg + 1) * LANES]
            r = (a * cos_t + pltpu.roll(a, LANES - ROPE_HALF, 1) * sin_lo
                 + pltpu.roll(a, ROPE_HALF, 1) * sin_hi)
            is_rope = (j * groups_per_tile + g) < n_rope_groups
            o_ref, k = dest[g]
            o_ref[:, k * LANES:(k + 1) * LANES] = jnp.where(is_rope, r, a).astype(o_ref.dtype)
            if aux_ref is not None and g == aux_group:
                aux_ref[...] = a

    acc = project(0)
    for s in range(1, n_sub):
        nxt = project(s)
        finish(s - 1, acc)
        acc = nxt
    finish(n_sub - 1, acc)


def _norm_proj(x2, g, w, colscale, rope_tab, *, tn, n_rope_groups, aux_group=None, out_widths=None, tm=512):
    n, d = x2.shape
    width = w.shape[1]
    gpt = tn // LANES
    if out_widths is None:
        out_widths = (tn,)
        out_shape = [jax.ShapeDtypeStruct((n, width), BF16)]
        out_specs = [pl.BlockSpec((tm, tn), lambda i, j: (i, j))]
    else:
        assert tn == width == sum(out_widths)
        out_shape = [jax.ShapeDtypeStruct((n, ow), BF16) for ow in out_widths]
        out_specs = [pl.BlockSpec((tm, ow), lambda i, j: (i, 0)) for ow in out_widths]
    kern = functools.partial(_norm_proj_kernel, n_rope_groups=n_rope_groups, groups_per_tile=gpt,
                             out_groups=tuple(ow // LANES for ow in out_widths), aux_group=aux_group)
    if aux_group is not None:
        out_shape.append(jax.ShapeDtypeStruct((n, LANES), F32))
        out_specs.append(pl.BlockSpec((tm, LANES), lambda i, j: (i, 0)))
    res = pl.pallas_call(
        kern,
        out_shape=out_shape,
        grid=(n // tm, width // tn),
        in_specs=[
            pl.BlockSpec((tm, d), lambda i, j: (i, 0)),
            pl.BlockSpec((1, d), lambda i, j: (0, 0)),
            pl.BlockSpec((d, tn), lambda i, j: (0, j)),
            pl.BlockSpec((1, tn), lambda i, j: (0, j)),
            pl.BlockSpec((tm, 3 * LANES), lambda i, j: (i, 0)),
        ],
        out_specs=out_specs,
        scratch_shapes=[pltpu.VMEM((tm, d), BF16)],
        compiler_params=_cparams(("parallel", "arbitrary")),
        name="norm_proj",
    )(x2, g.reshape(1, d), w, colscale, rope_tab)
    return res


def _out_proj_kernel(o_ref, w_ref, x_ref, y_ref):
    y_ref[...] = x_ref[...] + jnp.dot(o_ref[...], w_ref[...], preferred_element_type=F32)


def _out_proj(o, w, x2, *, tm=512, tn=2048):
    n, k = o.shape
    d = w.shape[1]
    return pl.pallas_call(
        _out_proj_kernel,
        out_shape=jax.ShapeDtypeStruct((n, d), F32),
        grid=(n // tm, d // tn),
        in_specs=[
            pl.BlockSpec((tm, k), lambda i, j: (i, 0)),
            pl.BlockSpec((k, tn), lambda i, j: (0, j)),
            pl.BlockSpec((tm, tn), lambda i, j: (i, j)),
        ],
        out_specs=pl.BlockSpec((tm, tn), lambda i, j: (i, j)),
        compiler_params=_cparams(("parallel", "arbitrary")),
        name="out_proj",
    )(o, w, x2)


def _rms(x, g):
    ms = jnp.mean(x * x, axis=-1, keepdims=True)
    return (x * lax.rsqrt(ms + NORM_EPS)) * g


def _mlp_kernel(x_ref, g_ref, wu_ref, wd_ref, *rest, out_norm):
    y_ref, h_ref = rest[-2:]
    f = pl.program_id(1)

    @pl.when(f == 0)
    def _():
        x = x_ref[...]
        h_ref[...] = _rms(x, g_ref[...]).astype(BF16)
        y_ref[...] = x

    u = jnp.dot(h_ref[...], wu_ref[...], preferred_element_type=F32)
    u = jnp.maximum(u, 0.0)
    a = (u * u).astype(BF16)
    y_ref[...] += jnp.dot(a, wd_ref[...], preferred_element_type=F32)

    if out_norm:
        @pl.when(f == pl.num_programs(1) - 1)
        def _():
            y_ref[...] = _rms(y_ref[...], rest[0][...])


def _mlp(x2, g, w_up, w_down, out_gain=None, *, tm=512, tf=1024):
    n, d = x2.shape
    d_ff = w_up.shape[1]
    vec = pl.BlockSpec((1, d), lambda i, f: (0, 0))
    in_specs = [
        pl.BlockSpec((tm, d), lambda i, f: (i, 0)),
        vec,
        pl.BlockSpec((d, tf), lambda i, f: (0, f)),
        pl.BlockSpec((tf, d), lambda i, f: (f, 0)),
    ]
    args = [x2, g.reshape(1, d), w_up, w_down]
    if out_gain is not None:
        in_specs.append(vec)
        args.append(out_gain.reshape(1, d))
    return pl.pallas_call(
        functools.partial(_mlp_kernel, out_norm=out_gain is not None),
        out_shape=jax.ShapeDtypeStruct((n, d), F32),
        grid=(n // tm, d_ff // tf),
        in_specs=in_specs,
        out_specs=pl.BlockSpec((tm, d), lambda i, f: (i, 0)),
        scratch_shapes=[pltpu.VMEM((tm, d), BF16)],
        compiler_params=_cparams(("parallel", "arbitrary")),
        name="mlp",
    )(*args)


def _fill_q_ext(q_ref, qx_ref, row0=0, slot=()):
    lane = lax.broadcasted_iota(jnp.int32, (BLOCK, LANES), 1)
    row = lax.broadcasted_iota(jnp.int32, (BLOCK, LANES), 0)
    eye = jnp.where(lane == row, 1.0, 0.0).astype(BF16)
    low = lane < HEAD_DIM
    for kv in range(N_KV_HEADS):
        for g in range(GROUP):
            h = kv * GROUP + g
            tile = q_ref[row0:row0 + BLOCK, (h // 2) * LANES:(h // 2 + 1) * LANES]
            keep = low if h % 2 == 0 else jnp.logical_not(low)
            rows = slice(g * BLOCK, (g + 1) * BLOCK)
            qx_ref[(*slot, kv, rows, slice(0, LANES))] = jnp.where(keep, tile, jnp.zeros_like(tile))
            qx_ref[(*slot, kv, rows, slice(LANES, EXT))] = eye


def _scores_t(k_tile, bias, qx):
    k_ext = jnp.concatenate([k_tile, bias], axis=1)
    return lax.dot_general(k_ext, qx, (((1,), (1,)), ((), ())), preferred_element_type=F32)


def _vt_ext(vt):
    return jnp.concatenate([vt, jnp.ones((BF16_ROWS, vt.shape[1]), BF16)], axis=0)


def _flash_chunks(c_lo, c_end, tc, make_bias, k_ref, vt_ref, refs):
    qx_ref, s_ref, cm_ref, m_ref, acc_ref, bias_ref = refs
    m_ref[...] = jnp.full(m_ref.shape, NEG_INF, F32)
    acc_ref[...] = jnp.zeros(acc_ref.shape, F32)
    c_last = c_end - 1

    def put_bias(c):
        bias_ref[c & 1] = make_bias(c)

    def issue_scores(c, kv):
        start = pl.multiple_of(c * tc, tc)
        k_tile = k_ref[pl.ds(start, tc), kv * LANES:(kv + 1) * LANES]
        s_t = _scores_t(k_tile, bias_ref[c & 1], qx_ref[kv])
        s_ref[kv % 2] = s_t
        cm_ref[kv % 2] = jnp.max(s_t, axis=0, keepdims=True)

    def consume(c, kv):
        slot = kv % 2
        start = pl.multiple_of(c * tc, tc)
        vt = _vt_ext(vt_ref[kv * HEAD_DIM:(kv + 1) * HEAD_DIM, pl.ds(start, tc)])
        m_old = m_ref[kv]
        m_new = jnp.maximum(m_old, cm_ref[slot])
        alpha = jnp.exp2(m_old - m_new)
        m_ref[kv] = m_new
        for n in range(QROWS // MXU_TILE):
            cols = slice(n * MXU_TILE, (n + 1) * MXU_TILE)
            part = alpha[:, cols] * acc_ref[kv, :, cols]
            for kk in range(tc // MXU_TILE):
                rows = slice(kk * MXU_TILE, (kk + 1) * MXU_TILE)
                p = jnp.exp2(s_ref[slot, rows, cols] - m_new[:, cols])
                part = part + jnp.dot(vt[:, rows], p.astype(BF16), preferred_element_type=F32)
            acc_ref[kv, :, cols] = part

    put_bias(c_lo)
    issue_scores(c_lo, 0)

    def body(c, carry):
        c_next = jnp.minimum(c + 1, c_last)
        put_bias(c_next)
        for kv in range(N_KV_HEADS):
            if kv + 1 < N_KV_HEADS:
                issue_scores(c, kv + 1)
            else:
                issue_scores(c_next, 0)
            consume(c, kv)
        return carry

    lax.fori_loop(c_lo, c_end, body, 0)


def _store_out(o_ref, kv, o_t, row0=0):
    for gp in range(GROUP // 2):
        pair = jnp.concatenate([o_t[:, (2 * gp) * BLOCK:(2 * gp + 1) * BLOCK],
                                o_t[:, (2 * gp + 1) * BLOCK:(2 * gp + 2) * BLOCK]], axis=0)
        col = (kv * GROUP + 2 * gp) * HEAD_DIM
        o_ref[row0:row0 + BLOCK, col:col + LANES] = pair.T.astype(o_ref.dtype)


def _finish_flash(o_ref, acc_ref):
    for kv in range(N_KV_HEADS):
        acc = acc_ref[kv]
        _store_out(o_ref, kv, acc[0:HEAD_DIM, :] / acc[HEAD_DIM:HEAD_DIM + 1, :])


def _bias_of(valid):
    return jnp.where(valid, 0.0, NEG_INF).astype(BF16)


def _qx_scratch():
    return pltpu.VMEM((N_KV_HEADS, QROWS, EXT), BF16)


def _flash_scratch(tc):
    return [
        _qx_scratch(),
        pltpu.VMEM((2, tc, QROWS), F32),
        pltpu.VMEM((2, 1, QROWS), F32),
        pltpu.VMEM((N_KV_HEADS, 1, QROWS), F32),
        pltpu.VMEM((N_KV_HEADS, VT_ROWS, QROWS), F32),
        pltpu.VMEM((2, tc, BLOCK), BF16),
    ]


WIN_TILES = 4


def _window_kernel(*refs, span, tiles, with_sinks, emit_lse):
    refs = list(refs)
    sink_ref = refs.pop(0) if with_sinks else None
    q_ref, kp_ref, kc_ref, vp_ref, vc_ref, o_ref = refs[:6]
    lse_ref = refs[6] if emit_lse else None
    qx_ref, sw_ref = refs[-2:]
    first = pl.program_id(2) == 0
    key = lax.broadcasted_iota(jnp.int32, (2 * BLOCK, BLOCK), 0)
    qry = lax.broadcasted_iota(jnp.int32, (2 * BLOCK, BLOCK), 1)
    dist = qry + BLOCK - key
    band = (dist >= 0) & (dist < span)
    bias_any = _bias_of(band)
    bias_first = _bias_of(band & ((key >= BLOCK) | jnp.logical_not(first)))

    def v_t(block):
        return block.astype(F32).T.astype(BF16)

    ones = jnp.ones((BF16_ROWS, 2 * BLOCK), BF16)
    vts = [v_t(vp_ref[...])] + [v_t(vc_ref[j * BLOCK:(j + 1) * BLOCK, :]) for j in range(tiles)]

    def issue_scores(j):
        _fill_q_ext(q_ref, qx_ref, j * BLOCK, (j,))
        for kv in range(N_KV_HEADS):
            lanes = slice(kv * LANES, (kv + 1) * LANES)
            k_prev = kp_ref[:, lanes] if j == 0 else kc_ref[(j - 1) * BLOCK:j * BLOCK, lanes]
            k_tile = jnp.concatenate([k_prev, kc_ref[j * BLOCK:(j + 1) * BLOCK, lanes]], axis=0)
            sw_ref[j, kv] = _scores_t(k_tile, bias_first if j == 0 else bias_any, qx_ref[j, kv])

    def consume(j):
        vt2 = jnp.concatenate([vts[j], vts[j + 1]], axis=1)
        lse_rows = []
        for kv in range(N_KV_HEADS):
            s_t = sw_ref[j, kv]
            m = jnp.max(s_t, axis=0, keepdims=True)
            if with_sinks:
                sink = jnp.concatenate(
                    [jnp.full((1, BLOCK), sink_ref[kv * GROUP + g] * LOG2E, F32) for g in range(GROUP)], axis=1)
                m = jnp.maximum(m, sink)
            p = jnp.exp2(s_t - m).astype(BF16)
            vt = jnp.concatenate([vt2[kv * HEAD_DIM:(kv + 1) * HEAD_DIM, :], ones], axis=0)
            acc = jnp.dot(vt, p, preferred_element_type=F32)
            den = acc[HEAD_DIM:HEAD_DIM + 1, :]
            if with_sinks:
                den = den + jnp.exp2(sink - m)
            _store_out(o_ref, kv, acc[0:HEAD_DIM, :] / den, j * BLOCK)
            if emit_lse:
                lse = m + jnp.log2(den)
                lse_rows += [lse[:, g * BLOCK:(g + 1) * BLOCK] for g in range(GROUP)]
        if emit_lse:
            pad = jnp.zeros((BLOCK - N_HEADS, BLOCK), F32)
            lse_ref[j * BLOCK:(j + 1) * BLOCK, :] = jnp.concatenate(lse_rows + [pad], axis=0).T

    issue_scores(0)
    for j in range(tiles):
        if j + 1 < tiles:
            issue_scores(j + 1)
        consume(j)


def _by_residue(a, batch, seq, dil):
    w = a.shape[1]
    if dil == 1:
        return a.reshape(batch, 1, seq, w)
    return jnp.swapaxes(a.reshape(batch, seq // dil, dil, w), 1, 2)


def _by_token(a, batch, seq, dil):
    w = a.shape[-1]
    if dil == 1:
        return a.reshape(batch * seq, w)
    return jnp.swapaxes(a, 1, 2).reshape(batch * seq, w)


def _window_attention(q, k, v, sinks, batch, seq, *, dil, span, emit_lse):
    sub = seq // dil
    tiles = max(1, min(WIN_TILES, sub // BLOCK // 2))
    rows = tiles * BLOCK
    with_sinks = sinks is not None
    cur = lambda b, r, u: (b, r, u, 0)
    prev = lambda b, r, u: (b, r, jnp.maximum(u * tiles - 1, 0), 0)
    in_specs = [
        pl.BlockSpec((None, None, rows, ATTN_WIDTH), cur),
        pl.BlockSpec((None, None, BLOCK, KDUP_WIDTH), prev),
        pl.BlockSpec((None, None, rows, KDUP_WIDTH), cur),
        pl.BlockSpec((None, None, BLOCK, KV_WIDTH), prev),
        pl.BlockSpec((None, None, rows, KV_WIDTH), cur),
    ]
    qr, kr, vr = (_by_residue(a, batch, seq, dil) for a in (q, k, v))
    args = [qr, kr, kr, vr, vr]
    if with_sinks:
        in_specs.insert(0, pl.BlockSpec(memory_space=pltpu.SMEM))
        args.insert(0, sinks)
    out_shape = [jax.ShapeDtypeStruct((batch, dil, sub, ATTN_WIDTH), BF16)]
    out_specs = [pl.BlockSpec((None, None, rows, ATTN_WIDTH), cur)]
    if emit_lse:
        out_shape.append(jax.ShapeDtypeStruct((batch, dil, sub, LANES), F32))
        out_specs.append(pl.BlockSpec((None, None, rows, LANES), cur))
    res = pl.pallas_call(
        functools.partial(_window_kernel, span=span, tiles=tiles, with_sinks=with_sinks, emit_lse=emit_lse),
        out_shape=out_shape,
        grid=(batch, dil, sub // rows),
        in_specs=in_specs,
        out_specs=out_specs,
        scratch_shapes=[pltpu.VMEM((tiles, N_KV_HEADS, QROWS, EXT), BF16),
                        pltpu.VMEM((tiles, N_KV_HEADS, 2 * BLOCK, QROWS), F32)],
        compiler_params=_cparams(("parallel", "parallel", "arbitrary")),
        name="window_attention",
    )(*args)
    o = _by_token(res[0], batch, seq, dil)
    return (o, _by_token(res[1], batch, seq, dil)) if emit_lse else o


def _merge_kernel(*refs):
    n_br = (len(refs) - 2) // 2
    o_refs, l_refs, e_ref, out_ref = refs[:n_br], refs[n_br:2 * n_br], refs[-2], refs[-1]
    lses = [l[...] for l in l_refs]
    top = functools.reduce(jnp.maximum, lses)
    ws = [jnp.exp2(l - top) for l in lses]
    tot = functools.reduce(lambda a, b: a + b, ws)
    out = None
    for w, o_ref in zip(ws, o_refs):
        wn = w / tot
        hi = wn.astype(BF16)
        lo = (wn - hi.astype(F32)).astype(BF16)
        spread = (jnp.dot(hi, e_ref[...], preferred_element_type=F32)
                  + jnp.dot(lo, e_ref[...], preferred_element_type=F32))
        term = spread * o_ref[...].astype(F32)
        out = term if out is None else out + term
    out_ref[...] = out.astype(out_ref.dtype)


def _merge_branches(outs, lses, *, tm=512):
    n = outs[0].shape[0]
    head_of_lane = jnp.arange(ATTN_WIDTH, dtype=jnp.int32) // HEAD_DIM
    expand = (jnp.arange(LANES, dtype=jnp.int32)[:, None] == head_of_lane[None, :]).astype(BF16)
    row = lambda width: pl.BlockSpec((tm, width), lambda i: (i, 0))
    return pl.pallas_call(
        _merge_kernel,
        out_shape=jax.ShapeDtypeStruct((n, ATTN_WIDTH), BF16),
        grid=(n // tm,),
        in_specs=[row(ATTN_WIDTH)] * len(outs) + [row(LANES)] * len(lses)
                 + [pl.BlockSpec((LANES, ATTN_WIDTH), lambda i: (0, 0))],
        out_specs=row(ATTN_WIDTH),
        compiler_params=_cparams(("parallel",)),
        name="merge_branches",
    )(*outs, *lses, expand)


def _dilated_attention(q, k, v, batch, seq):
    outs, lses = [], []
    for window, dil in DILATED_BRANCHES:
        o, lse = _window_attention(q, k, v, None, batch, seq, dil=dil, span=window // dil + 1, emit_lse=True)
        outs.append(o)
        lses.append(lse)
    return _merge_branches(outs, lses)


IDX_CHUNK = 512
ATT_CHUNK = 512
INT_MIN = -2 ** 31
F32_BITS = 32
BITS_PER_CHECK = 4


HIGH_BITS = 16


def _truncate_low_bits(x):
    bits = lax.bitcast_convert_type(x, jnp.int32) & jnp.int32(-(1 << (F32_BITS - HIGH_BITS)))
    return lax.bitcast_convert_type(bits, F32)


def _sortable_to_f32(t):
    bits = jnp.where(t >= 0, t, t ^ jnp.int32(0x7FFFFFFF))
    return lax.bitcast_convert_type(bits, F32)


def _dsa_kernel(q_ref, qi_ref, wi_ref, ki_ref, k_ref, vt_ref, o_ref, sc_ref, sc16_ref, *refs, topk):
    qx_ref, acc_ref = refs[0], refs[4]
    i = pl.program_id(1)
    t0 = i * BLOCK
    n_idx = (t0 + BLOCK + IDX_CHUNK - 1) // IDX_CHUNK
    n_att = (t0 + BLOCK + ATT_CHUNK - 1) // ATT_CHUNK

    qis = jnp.concatenate(
        [qi_ref[:, h * IDX_DIM:(h + 1) * IDX_DIM] for h in range(IDX_HEADS)], axis=0)
    w_t = (wi_ref[...] * IDX_W_SCALE).T
    key = lax.broadcasted_iota(jnp.int32, (IDX_CHUNK, BLOCK), 0)
    qry = lax.broadcasted_iota(jnp.int32, (IDX_CHUNK, BLOCK), 1)

    def idx_body(c, carry):
        start = pl.multiple_of(c * IDX_CHUNK, IDX_CHUNK)
        kic = ki_ref[pl.ds(start, IDX_CHUNK), 0:IDX_DIM]
        rel = jnp.maximum(
            lax.dot_general(kic, qis, (((1,), (1,)), ((), ())), preferred_element_type=F32), 0.0)
        score = jnp.zeros((IDX_CHUNK, BLOCK), F32)
        for h in range(IDX_HEADS):
            score = score + rel[:, h * BLOCK:(h + 1) * BLOCK] * w_t[h:h + 1, :]
        masked = jnp.where(start + key <= t0 + qry, score, NEG_INF)
        sc_ref[pl.ds(start, IDX_CHUNK), :] = masked
        sc16_ref[pl.ds(start, IDX_CHUNK), :] = _truncate_low_bits(masked).astype(BF16)
        return carry

    lax.fori_loop(0, n_idx, idx_body, 0)

    def count_ge(cand_f):
        def cbody(c, cnt):
            start = pl.multiple_of(c * IDX_CHUNK, IDX_CHUNK)
            hit = jnp.where(sc_ref[pl.ds(start, IDX_CHUNK), :] >= cand_f, 1.0, 0.0)
            parts = [hit[r * SUBLANES:(r + 1) * SUBLANES, :] for r in range(IDX_CHUNK // SUBLANES)]
            while len(parts) > 1:
                parts = [a + b for a, b in zip(parts[0::2], parts[1::2])]
            return cnt + parts[0]
        cnt = lax.fori_loop(0, n_idx, cbody, jnp.zeros((SUBLANES, BLOCK), F32))
        return jnp.sum(cnt, axis=0, keepdims=True)

    def count_ge16(cand_f):
        cand16 = _truncate_low_bits(cand_f).astype(BF16)
        one, zero = jnp.ones((), BF16), jnp.zeros((), BF16)

        def cbody(c, cnt):
            start = pl.multiple_of(c * IDX_CHUNK, IDX_CHUNK)
            hit = jnp.where(sc16_ref[pl.ds(start, IDX_CHUNK), :] >= cand16, one, zero)
            parts = [hit[r * BF16_ROWS:(r + 1) * BF16_ROWS, :] for r in range(IDX_CHUNK // BF16_ROWS)]
            while len(parts) > 1:
                parts = [a + b for a, b in zip(parts[0::2], parts[1::2])]
            return cnt + parts[0].astype(F32)
        cnt = lax.fori_loop(0, n_idx, cbody, jnp.zeros((BF16_ROWS, BLOCK), F32))
        return jnp.sum(cnt, axis=0, keepdims=True)

    def bit_step(b, t, done, count):
        bit = lax.shift_left(jnp.int32(1), F32_BITS - 1 - b)
        cand = jnp.where(b == 0, jnp.zeros_like(t), t | bit)
        cnt = count(_sortable_to_f32(cand))
        take = (cnt >= float(topk)) & (done == 0)
        t = jnp.where(take, cand, t)
        done = jnp.where(take & (cnt == float(topk)), 1, done)
        return t, done

    def high_body(b, state):
        return bit_step(b, *state, count_ge16)

    def bit_cond(state):
        b, _, _, n_open = state
        return (b < F32_BITS) & (n_open > 0)

    def bit_body(state):
        b0, t, done, _ = state
        for k in range(BITS_PER_CHECK):
            t, done = bit_step(b0 + k, t, done, count_ge)
        n_open = jnp.sum(1 - done)
        return b0 + BITS_PER_CHECK, t, done, n_open

    t_hi, done_hi = lax.fori_loop(0, HIGH_BITS, high_body,
                                  (jnp.full((1, BLOCK), INT_MIN, jnp.int32), jnp.zeros((1, BLOCK), jnp.int32)))
    state = (jnp.int32(HIGH_BITS), t_hi, done_hi, jnp.sum(1 - done_hi))
    _, t_int, _, _ = lax.while_loop(bit_cond, bit_body, state)
    thr = jnp.maximum(_sortable_to_f32(t_int), jnp.float32(NEG_INF * 0.5))

    _fill_q_ext(q_ref, qx_ref)

    def make_bias(c):
        start = pl.multiple_of(c * ATT_CHUNK, ATT_CHUNK)
        return _bias_of(sc_ref[pl.ds(start, ATT_CHUNK), :] >= thr)

    _flash_chunks(0, n_att, ATT_CHUNK, make_bias, k_ref, vt_ref, refs)
    _finish_flash(o_ref, acc_ref)


DSA_QI_OFF = ATTN_WIDTH
DSA_K_OFF = DSA_QI_OFF + IDX_HEADS * IDX_DIM
DSA_KI_OFF = DSA_K_OFF + KDUP_WIDTH
DSA_WI_OFF = DSA_KI_OFF + LANES
DSA_V_OFF = DSA_WI_OFF + LANES
DSA_WIDTH = DSA_V_OFF + KV_WIDTH
DSA_TN = 2048


def _dsa_attention(proj, wi, vt, batch, seq):
    nb = seq // BLOCK
    n = batch * seq
    topk = min(TOPK_MAX, seq // 4)
    qi_w = IDX_HEADS * IDX_DIM
    chunk = max(IDX_CHUNK, ATT_CHUNK)
    seq_pad = -(-seq // chunk) * chunk
    return pl.pallas_call(
        functools.partial(_dsa_kernel, topk=topk),
        out_shape=jax.ShapeDtypeStruct((n, ATTN_WIDTH), BF16),
        grid=(batch, nb),
        in_specs=[
            pl.BlockSpec((BLOCK, ATTN_WIDTH), lambda b, i: (b * nb + i, 0)),
            pl.BlockSpec((BLOCK, qi_w), lambda b, i: (b * nb + i, DSA_QI_OFF // qi_w)),
            pl.BlockSpec((BLOCK, LANES), lambda b, i: (b * nb + i, 0)),
            pl.BlockSpec((seq, LANES), lambda b, i: (b, DSA_KI_OFF // LANES)),
            pl.BlockSpec((seq, KDUP_WIDTH), lambda b, i: (b, DSA_K_OFF // KDUP_WIDTH)),
            pl.BlockSpec((None, KV_WIDTH, seq), lambda b, i: (b, 0, 0)),
        ],
        out_specs=pl.BlockSpec((BLOCK, ATTN_WIDTH), lambda b, i: (b * nb + i, 0)),
        scratch_shapes=[pltpu.VMEM((seq_pad, BLOCK), F32), pltpu.VMEM((seq_pad, BLOCK), BF16)]
                       + _flash_scratch(ATT_CHUNK),
        compiler_params=_cparams(("parallel", "arbitrary")),
        name="dsa_attention",
    )(proj, proj, wi, proj, proj, vt)


def _rope_table(positions):
    inv = ROPE_THETA ** (-jnp.arange(0, ROPE_DIM, 2, dtype=F32) / ROPE_DIM)
    ang = positions.astype(F32).reshape(-1, 1) * inv[None, :]
    cos, sin = jnp.cos(ang), jnp.sin(ang)
    n = ang.shape[0]
    pad = HEAD_DIM - ROPE_DIM
    cos_h = jnp.concatenate([cos, cos, jnp.ones((n, pad), F32)], axis=1)
    lo_h = jnp.concatenate([-sin, jnp.zeros((n, HEAD_DIM - ROPE_HALF), F32)], axis=1)
    hi_h = jnp.concatenate([jnp.zeros((n, ROPE_HALF), F32), sin, jnp.zeros((n, pad), F32)], axis=1)
    reps = LANES // HEAD_DIM
    return jnp.concatenate([jnp.tile(cos_h, (1, reps)), jnp.tile(lo_h, (1, reps)),
                            jnp.tile(hi_h, (1, reps))], axis=1)


def _dup_heads(wk):
    d = wk.shape[0]
    w4 = wk.reshape(d, N_KV_HEADS, 1, HEAD_DIM)
    return jnp.broadcast_to(w4, (d, N_KV_HEADS, LANES // HEAD_DIM, HEAD_DIM)).reshape(d, KDUP_WIDTH)


QKV_WIDTH_EXT = ATTN_WIDTH + KDUP_WIDTH + KV_WIDTH


def _qkv_weight(w_in):
    o = ATTN_WIDTH
    return jnp.concatenate([w_in[:, :o], _dup_heads(w_in[:, o:o + KV_WIDTH]),
                            w_in[:, o + KV_WIDTH:o + 2 * KV_WIDTH]], axis=1)


def _qkv_colscale():
    return jnp.concatenate([jnp.full((1, ATTN_WIDTH), SCALE * LOG2E, F32),
                            jnp.ones((1, KDUP_WIDTH + KV_WIDTH), F32)], axis=1)


def _dsa_weight(w_in):
    d = w_in.shape[0]
    o = ATTN_WIDTH
    wq = w_in[:, :o]
    wk = w_in[:, o:o + KV_WIDTH]
    wv = w_in[:, o + KV_WIDTH:o + 2 * KV_WIDTH]
    o += 2 * KV_WIDTH
    wqi = w_in[:, o:o + IDX_HEADS * IDX_DIM]
    o += IDX_HEADS * IDX_DIM
    wki = w_in[:, o:o + IDX_DIM]
    o += IDX_DIM
    wwi = w_in[:, o:o + IDX_HEADS]
    z = lambda c: jnp.zeros((d, c), w_in.dtype)
    return jnp.concatenate([wq, wqi, _dup_heads(wk), wki, z(LANES - IDX_DIM),
                            wwi, z(LANES - IDX_HEADS), wv], axis=1)


def _dsa_colscale():
    return jnp.concatenate([jnp.full((1, ATTN_WIDTH), SCALE * LOG2E, F32),
                            jnp.full((1, IDX_HEADS * IDX_DIM), IDX_SCALE, F32),
                            jnp.ones((1, DSA_WIDTH - DSA_K_OFF), F32)], axis=1)


def _v_transposed(proj, v_off, batch, seq):
    v = proj[:, v_off:v_off + KV_WIDTH].reshape(batch, seq, KV_WIDTH)
    return jnp.swapaxes(v, 1, 2)


def kernel(x, positions, norm_attn, norm_mlp, w_up, w_down, final_norm,
           a_w_in, a_sinks, a_w_out, b_w_in, b_w_out, c_w_in, c_w_out):
    batch, seq, d = x.shape
    depth = norm_attn.shape[0]
    x2 = x.reshape(batch * seq, d)
    rope_tab = _rope_table(positions)
    qkv_scale = _qkv_colscale()
    qkv_rope_groups = (ATTN_WIDTH + KDUP_WIDTH) // LANES
    for i in range(depth):
        j, kind = divmod(i, 3)
        if kind == 1:
            proj, wi = _norm_proj(x2, norm_attn[i], _dsa_weight(b_w_in[j]).astype(BF16),
                                  _dsa_colscale(), rope_tab, tn=DSA_TN,
                                  n_rope_groups=DSA_WI_OFF // LANES,
                                  aux_group=(DSA_WI_OFF % DSA_TN) // LANES)
            o = _dsa_attention(proj, wi, _v_transposed(proj, DSA_V_OFF, batch, seq), batch, seq)
            w_out = b_w_out[j]
        else:
            w_in = a_w_in[j] if kind == 0 else c_w_in[j]
            q, k, v = _norm_proj(x2, norm_attn[i], _qkv_weight(w_in).astype(BF16), qkv_scale, rope_tab,
                                 tn=QKV_WIDTH_EXT, n_rope_groups=qkv_rope_groups,
                                 out_widths=(ATTN_WIDTH, KDUP_WIDTH, KV_WIDTH))
            if kind == 0:
                o = _window_attention(q, k, v, a_sinks[j], batch, seq, dil=1, span=SWA_WINDOW, emit_lse=False)
                w_out = a_w_out[j]
            else:
                o = _dilated_attention(q, k, v, batch, seq)
                w_out = c_w_out[j]
        x2 = _out_proj(o, w_out.astype(BF16), x2)
        x2 = _mlp(x2, norm_mlp[i], w_up[i].astype(BF16), w_down[i].astype(BF16),
                  final_norm if i == depth - 1 else None)
    return x2.reshape(batch, seq, d)
```

```python
import functools

import jax
import jax.numpy as jnp
from jax import lax
from jax.experimental import pallas as pl
from jax.experimental.pallas import tpu as pltpu

HEAD_DIM = 64
N_KV_HEADS = 4
GROUP = 8
N_HEADS = N_KV_HEADS * GROUP
ATTN_WIDTH = N_HEADS * HEAD_DIM
KV_WIDTH = N_KV_HEADS * HEAD_DIM
ROPE_DIM = HEAD_DIM // 4
ROPE_HALF = ROPE_DIM // 2
ROPE_THETA = 500000.0
SCALE = HEAD_DIM ** -0.5
BLOCK = 128
SWA_WINDOW = 128
IDX_HEADS = 16
IDX_DIM = 64
IDX_SCALE = IDX_DIM ** -0.5
IDX_W_SCALE = IDX_HEADS ** -0.5
TOPK_MAX = 256
DILATED_BRANCHES = ((128, 1), (512, 4), (2048, 16))
NORM_EPS = 1e-5
NEG_INF = -1e30

LANES = 128
BF16_ROWS = 16
SUBLANES = 8
MXU_TILE = 256
LOG2E = 1.4426950408889634
VMEM_LIMIT = 52 * 1024 * 1024

BF16 = jnp.bfloat16
F32 = jnp.float32

KDUP_WIDTH = N_KV_HEADS * LANES
QROWS = GROUP * BLOCK
EXT = 2 * LANES
VT_ROWS = HEAD_DIM + BF16_ROWS


def _cparams(sem):
    return pltpu.CompilerParams(dimension_semantics=sem, vmem_limit_bytes=VMEM_LIMIT)


def _norm_proj_kernel(x_ref, g_ref, w_ref, cs_ref, rope_ref, *rest,
                      n_rope_groups, groups_per_tile, out_groups, aux_group):
    o_refs = rest[:len(out_groups)]
    aux_ref = rest[len(out_groups)] if aux_group is not None else None
    h_ref = rest[-1]
    j = pl.program_id(1)

    @pl.when(j == 0)
    def _():
        x = x_ref[...]
        ms = jnp.mean(x * x, axis=-1, keepdims=True)
        h_ref[...] = ((x * lax.rsqrt(ms + NORM_EPS)) * g_ref[...]).astype(BF16)

    cos_t = rope_ref[:, 0:LANES]
    sin_lo = rope_ref[:, LANES:2 * LANES]
    sin_hi = rope_ref[:, 2 * LANES:3 * LANES]
    h = h_ref[...]
    sub_groups = MXU_TILE // LANES
    n_sub = groups_per_tile // sub_groups
    dest = [(o_ref, k) for o_ref, cnt in zip(o_refs, out_groups) for k in range(cnt)]

    def project(s):
        cols = slice(s * MXU_TILE, (s + 1) * MXU_TILE)
        return jnp.dot(h, w_ref[:, cols], preferred_element_type=F32) * cs_ref[:, cols]

    def finish(s, acc):
        for gg in range(sub_groups):
            g = s * sub_groups + gg
            a = acc[:, gg * LANES:(gg + 1) * LANES]
            r = (a * cos_t + pltpu.roll(a, LANES - ROPE_HALF, 1) * sin_lo
                 + pltpu.roll(a, ROPE_HALF, 1) * sin_hi)
            is_rope = (j * groups_per_tile + g) < n_rope_groups
            o_ref, k = dest[g]
            o_ref[:, k * LANES:(k + 1) * LANES] = jnp.where(is_rope, r, a).astype(o_ref.dtype)
            if aux_ref is not None and g == aux_group:
                aux_ref[...] = a

    acc = project(0)
    for s in range(1, n_sub):
        nxt = project(s)
        finish(s - 1, acc)
        acc = nxt
    finish(n_sub - 1, acc)


def _norm_proj(x2, g, w, colscale, rope_tab, *, tn, n_rope_groups, aux_group=None, out_widths=None, tm=512):
    n, d = x2.shape
    width = w.shape[1]
    gpt = tn // LANES
    if out_widths is None:
        out_widths = (tn,)
        out_shape = [jax.ShapeDtypeStruct((n, width), BF16)]
        out_specs = [pl.BlockSpec((tm, tn), lambda i, j: (i, j))]
    else:
        assert tn == width == sum(out_widths)
        out_shape = [jax.ShapeDtypeStruct((n, ow), BF16) for ow in out_widths]
        out_specs = [pl.BlockSpec((tm, ow), lambda i, j: (i, 0)) for ow in out_widths]
    kern = functools.partial(_norm_proj_kernel, n_rope_groups=n_rope_groups, groups_per_tile=gpt,
                             out_groups=tuple(ow // LANES for ow in out_widths), aux_group=aux_group)
    if aux_group is not None:
        out_shape.append(jax.ShapeDtypeStruct((n, LANES), F32))
        out_specs.append(pl.BlockSpec((tm, LANES), lambda i, j: (i, 0)))
    res = pl.pallas_call(
        kern,
        out_shape=out_shape,
        grid=(n // tm, width // tn),
        in_specs=[
            pl.BlockSpec((tm, d), lambda i, j: (i, 0)),
            pl.BlockSpec((1, d), lambda i, j: (0, 0)),
            pl.BlockSpec((d, tn), lambda i, j: (0, j)),
            pl.BlockSpec((1, tn), lambda i, j: (0, j)),
            pl.BlockSpec((tm, 3 * LANES), lambda i, j: (i, 0)),
        ],
        out_specs=out_specs,
        scratch_shapes=[pltpu.VMEM((tm, d), BF16)],
        compiler_params=_cparams(("parallel", "arbitrary")),
        name="norm_proj",
    )(x2, g.reshape(1, d), w, colscale, rope_tab)
    return res


def _out_proj_kernel(o_ref, w_ref, x_ref, y_ref):
    y_ref[...] = x_ref[...] + jnp.dot(o_ref[...], w_ref[...], preferred_element_type=F32)


def _out_proj(o, w, x2, *, tm=512, tn=2048):
    n, k = o.shape
    d = w.shape[1]
    return pl.pallas_call(
        _out_proj_kernel,
        out_shape=jax.ShapeDtypeStruct((n, d), F32),
        grid=(n // tm, d // tn),
        in_specs=[
            pl.BlockSpec((tm, k), lambda i, j: (i, 0)),
            pl.BlockSpec((k, tn), lambda i, j: (0, j)),
            pl.BlockSpec((tm, tn), lambda i, j: (i, j)),
        ],
        out_specs=pl.BlockSpec((tm, tn), lambda i, j: (i, j)),
        compiler_params=_cparams(("parallel", "arbitrary")),
        name="out_proj",
    )(o, w, x2)


def _rms(x, g):
    ms = jnp.mean(x * x, axis=-1, keepdims=True)
    return (x * lax.rsqrt(ms + NORM_EPS)) * g


def _mlp_kernel(x_ref, g_ref, wu_ref, wd_ref, *rest, out_norm):
    y_ref, h_ref = rest[-2:]
    f = pl.program_id(1)

    @pl.when(f == 0)
    def _():
        x = x_ref[...]
        h_ref[...] = _rms(x, g_ref[...]).astype(BF16)
        y_ref[...] = x

    u = jnp.dot(h_ref[...], wu_ref[...], preferred_element_type=F32)
    u = jnp.maximum(u, 0.0)
    a = (u * u).astype(BF16)
    y_ref[...] += jnp.dot(a, wd_ref[...], preferred_element_type=F32)

    if out_norm:
        @pl.when(f == pl.num_programs(1) - 1)
        def _():
            y_ref[...] = _rms(y_ref[...], rest[0][...])


def _mlp(x2, g, w_up, w_down, out_gain=None, *, tm=512, tf=1024):
    n, d = x2.shape
    d_ff = w_up.shape[1]
    vec = pl.BlockSpec((1, d), lambda i, f: (0, 0))
    in_specs = [
        pl.BlockSpec((tm, d), lambda i, f: (i, 0)),
        vec,
        pl.BlockSpec((d, tf), lambda i, f: (0, f)),
        pl.BlockSpec((tf, d), lambda i, f: (f, 0)),
    ]
    args = [x2, g.reshape(1, d), w_up, w_down]
    if out_gain is not None:
        in_specs.append(vec)
        args.append(out_gain.reshape(1, d))
    return pl.pallas_call(
        functools.partial(_mlp_kernel, out_norm=out_gain is not None),
        out_shape=jax.ShapeDtypeStruct((n, d), F32),
        grid=(n // tm, d_ff // tf),
        in_specs=in_specs,
        out_specs=pl.BlockSpec((tm, d), lambda i, f: (i, 0)),
        scratch_shapes=[pltpu.VMEM((tm, d), BF16)],
        compiler_params=_cparams(("parallel", "arbitrary")),
        name="mlp",
    )(*args)


def _fill_q_ext(q_ref, qx_ref, row0=0, slot=()):
    lane = lax.broadcasted_iota(jnp.int32, (BLOCK, LANES), 1)
    row = lax.broadcasted_iota(jnp.int32, (BLOCK, LANES), 0)
    eye = jnp.where(lane == row, 1.0, 0.0).astype(BF16)
    low = lane < HEAD_DIM
    for kv in range(N_KV_HEADS):
        for g in range(GROUP):
            h = kv * GROUP + g
            tile = q_ref[row0:row0 + BLOCK, (h // 2) * LANES:(h // 2 + 1) * LANES]
            keep = low if h % 2 == 0 else jnp.logical_not(low)
            rows = slice(g * BLOCK, (g + 1) * BLOCK)
            qx_ref[(*slot, kv, rows, slice(0, LANES))] = jnp.where(keep, tile, jnp.zeros_like(tile))
            qx_ref[(*slot, kv, rows, slice(LANES, EXT))] = eye


def _scores_t(k_tile, bias, qx):
    k_ext = jnp.concatenate([k_tile, bias], axis=1)
    return lax.dot_general(k_ext, qx, (((1,), (1,)), ((), ())), preferred_element_type=F32)


def _vt_ext(vt):
    return jnp.concatenate([vt, jnp.ones((BF16_ROWS, vt.shape[1]), BF16)], axis=0)


def _flash_chunks(c_lo, c_end, tc, make_bias, k_ref, vt_ref, refs):
    qx_ref, s_ref, cm_ref, m_ref, acc_ref, bias_ref = refs
    m_ref[...] = jnp.full(m_ref.shape, NEG_INF, F32)
    acc_ref[...] = jnp.zeros(acc_ref.shape, F32)
    c_last = c_end - 1

    def put_bias(c):
        bias_ref[c & 1] = make_bias(c)

    def issue_scores(c, kv):
        start = pl.multiple_of(c * tc, tc)
        k_tile = k_ref[pl.ds(start, tc), kv * LANES:(kv + 1) * LANES]
        s_t = _scores_t(k_tile, bias_ref[c & 1], qx_ref[kv])
        s_ref[kv % 2] = s_t
        cm_ref[kv % 2] = jnp.max(s_t, axis=0, keepdims=True)

    def consume(c, kv):
        slot = kv % 2
        start = pl.multiple_of(c * tc, tc)
        vt = _vt_ext(vt_ref[kv * HEAD_DIM:(kv + 1) * HEAD_DIM, pl.ds(start, tc)])
        m_old = m_ref[kv]
        m_new = jnp.maximum(m_old, cm_ref[slot])
        alpha = jnp.exp2(m_old - m_new)
        m_ref[kv] = m_new
        for n in range(QROWS // MXU_TILE):
            cols = slice(n * MXU_TILE, (n + 1) * MXU_TILE)
            part = alpha[:, cols] * acc_ref[kv, :, cols]
            for kk in range(tc // MXU_TILE):
                rows = slice(kk * MXU_TILE, (kk + 1) * MXU_TILE)
                p = jnp.exp2(s_ref[slot, rows, cols] - m_new[:, cols])
                part = part + jnp.dot(vt[:, rows], p.astype(BF16), preferred_element_type=F32)
            acc_ref[kv, :, cols] = part

    put_bias(c_lo)
    issue_scores(c_lo, 0)

    def body(c, carry):
        c_next = jnp.minimum(c + 1, c_last)
        put_bias(c_next)
        for kv in range(N_KV_HEADS):
            if kv + 1 < N_KV_HEADS:
                issue_scores(c, kv + 1)
            else:
                issue_scores(c_next, 0)
            consume(c, kv)
        return carry

    lax.fori_loop(c_lo, c_end, body, 0)


def _store_out(o_ref, kv, o_t, row0=0):
    for gp in range(GROUP // 2):
        pair = jnp.concatenate([o_t[:, (2 * gp) * BLOCK:(2 * gp + 1) * BLOCK],
                                o_t[:, (2 * gp + 1) * BLOCK:(2 * gp + 2) * BLOCK]], axis=0)
        col = (kv * GROUP + 2 * gp) * HEAD_DIM
        o_ref[row0:row0 + BLOCK, col:col + LANES] = pair.T.astype(o_ref.dtype)


def _finish_flash(o_ref, acc_ref):
    for kv in range(N_KV_HEADS):
        acc = acc_ref[kv]
        _store_out(o_ref, kv, acc[0:HEAD_DIM, :] / acc[HEAD_DIM:HEAD_DIM + 1, :])


def _bias_of(valid):
    return jnp.where(valid, 0.0, NEG_INF).astype(BF16)


def _qx_scratch():
    return pltpu.VMEM((N_KV_HEADS, QROWS, EXT), BF16)


def _flash_scratch(tc):
    return [
        _qx_scratch(),
        pltpu.VMEM((2, tc, QROWS), F32),
        pltpu.VMEM((2, 1, QROWS), F32),
        pltpu.VMEM((N_KV_HEADS, 1, QROWS), F32),
        pltpu.VMEM((N_KV_HEADS, VT_ROWS, QROWS), F32),
        pltpu.VMEM((2, tc, BLOCK), BF16),
    ]


WIN_TILES = 4


def _window_kernel(*refs, span, tiles, with_sinks, emit_lse):
    refs = list(refs)
    sink_ref = refs.pop(0) if with_sinks else None
    q_ref, kp_ref, kc_ref, vp_ref, vc_ref, o_ref = refs[:6]
    lse_ref = refs[6] if emit_lse else None
    qx_ref, sw_ref = refs[-2:]
    first = pl.program_id(2) == 0
    key = lax.broadcasted_iota(jnp.int32, (2 * BLOCK, BLOCK), 0)
    qry = lax.broadcasted_iota(jnp.int32, (2 * BLOCK, BLOCK), 1)
    dist = qry + BLOCK - key
    band = (dist >= 0) & (dist < span)
    bias_any = _bias_of(band)
    bias_first = _bias_of(band & ((key >= BLOCK) | jnp.logical_not(first)))

    def v_t(block):
        return block.astype(F32).T.astype(BF16)

    ones = jnp.ones((BF16_ROWS, 2 * BLOCK), BF16)
    vts = [v_t(vp_ref[...])] + [v_t(vc_ref[j * BLOCK:(j + 1) * BLOCK, :]) for j in range(tiles)]

    def issue_scores(j):
        _fill_q_ext(q_ref, qx_ref, j * BLOCK, (j,))
        for kv in range(N_KV_HEADS):
            lanes = slice(kv * LANES, (kv + 1) * LANES)
            k_prev = kp_ref[:, lanes] if j == 0 else kc_ref[(j - 1) * BLOCK:j * BLOCK, lanes]
            k_tile = jnp.concatenate([k_prev, kc_ref[j * BLOCK:(j + 1) * BLOCK, lanes]], axis=0)
            sw_ref[j, kv] = _scores_t(k_tile, bias_first if j == 0 else bias_any, qx_ref[j, kv])

    def consume(j):
        vt2 = jnp.concatenate([vts[j], vts[j + 1]], axis=1)
        lse_rows = []
        for kv in range(N_KV_HEADS):
            s_t = sw_ref[j, kv]
            m = jnp.max(s_t, axis=0, keepdims=True)
            if with_sinks:
                sink = jnp.concatenate(
                    [jnp.full((1, BLOCK), sink_ref[kv * GROUP + g] * LOG2E, F32) for g in range(GROUP)], axis=1)
                m = jnp.maximum(m, sink)
            p = jnp.exp2(s_t - m).astype(BF16)
            vt = jnp.concatenate([vt2[kv * HEAD_DIM:(kv + 1) * HEAD_DIM, :], ones], axis=0)
            acc = jnp.dot(vt, p, preferred_element_type=F32)
            den = acc[HEAD_DIM:HEAD_DIM + 1, :]
            if with_sinks:
                den = den + jnp.exp2(sink - m)
            _store_out(o_ref, kv, acc[0:HEAD_DIM, :] / den, j * BLOCK)
            if emit_lse:
                lse = m + jnp.log2(den)
                lse_rows += [lse[:, g * BLOCK:(g + 1) * BLOCK] for g in range(GROUP)]
        if emit_lse:
            pad = jnp.zeros((BLOCK - N_HEADS, BLOCK), F32)
            lse_ref[j * BLOCK:(j + 1) * BLOCK, :] = jnp.concatenate(lse_rows + [pad], axis=0).T

    issue_scores(0)
    for j in range(tiles):
        if j + 1 < tiles:
            issue_scores(j + 1)
        consume(j)


def _by_residue(a, batch, seq, dil):
    w = a.shape[1]
    if dil == 1:
        return a.reshape(batch, 1, seq, w)
    return jnp.swapaxes(a.reshape(batch, seq // dil, dil, w), 1, 2)


def _by_token(a, batch, seq, dil):
    w = a.shape[-1]
    if dil == 1:
        return a.reshape(batch * seq, w)
    return jnp.swapaxes(a, 1, 2).reshape(batch * seq, w)


def _window_attention(q, k, v, sinks, batch, seq, *, dil, span, emit_lse):
    sub = seq // dil
    tiles = max(1, min(WIN_TILES, sub // BLOCK // 2))
    rows = tiles * BLOCK
    with_sinks = sinks is not None
    cur = lambda b, r, u: (b, r, u, 0)
    prev = lambda b, r, u: (b, r, jnp.maximum(u * tiles - 1, 0), 0)
    in_specs = [
        pl.BlockSpec((None, None, rows, ATTN_WIDTH), cur),
        pl.BlockSpec((None, None, BLOCK, KDUP_WIDTH), prev),
        pl.BlockSpec((None, None, rows, KDUP_WIDTH), cur),
        pl.BlockSpec((None, None, BLOCK, KV_WIDTH), prev),
        pl.BlockSpec((None, None, rows, KV_WIDTH), cur),
    ]
    qr, kr, vr = (_by_residue(a, batch, seq, dil) for a in (q, k, v))
    args = [qr, kr, kr, vr, vr]
    if with_sinks:
        in_specs.insert(0, pl.BlockSpec(memory_space=pltpu.SMEM))
        args.insert(0, sinks)
    out_shape = [jax.ShapeDtypeStruct((batch, dil, sub, ATTN_WIDTH), BF16)]
    out_specs = [pl.BlockSpec((None, None, rows, ATTN_WIDTH), cur)]
    if emit_lse:
        out_shape.append(jax.ShapeDtypeStruct((batch, dil, sub, LANES), F32))
        out_specs.append(pl.BlockSpec((None, None, rows, LANES), cur))
    res = pl.pallas_call(
        functools.partial(_window_kernel, span=span, tiles=tiles, with_sinks=with_sinks, emit_lse=emit_lse),
        out_shape=out_shape,
        grid=(batch, dil, sub // rows),
        in_specs=in_specs,
        out_specs=out_specs,
        scratch_shapes=[pltpu.VMEM((tiles, N_KV_HEADS, QROWS, EXT), BF16),
                        pltpu.VMEM((tiles, N_KV_HEADS, 2 * BLOCK, QROWS), F32)],
        compiler_params=_cparams(("parallel", "parallel", "arbitrary")),
        name="window_attention",
    )(*args)
    o = _by_token(res[0], batch, seq, dil)
    return (o, _by_token(res[1], batch, seq, dil)) if emit_lse else o


def _merge_kernel(*refs):
    n_br = (len(refs) - 2) // 2
    o_refs, l_refs, e_ref, out_ref = refs[:n_br], refs[n_br:2 * n_br], refs[-2], refs[-1]
    lses = [l[...] for l in l_refs]
    top = functools.reduce(jnp.maximum, lses)
    ws = [jnp.exp2(l - top) for l in lses]
    tot = functools.reduce(lambda a, b: a + b, ws)
    out = None
    for w, o_ref in zip(ws, o_refs):
        wn = w / tot
        hi = wn.astype(BF16)
        lo = (wn - hi.astype(F32)).astype(BF16)
        spread = (jnp.dot(hi, e_ref[...], preferred_element_type=F32)
                  + jnp.dot(lo, e_ref[...], preferred_element_type=F32))
        term = spread * o_ref[...].astype(F32)
        out = term if out is None else out + term
    out_ref[...] = out.astype(out_ref.dtype)


def _merge_branches(outs, lses, *, tm=512):
    n = outs[0].shape[0]
    head_of_lane = jnp.arange(ATTN_WIDTH, dtype=jnp.int32) // HEAD_DIM
    expand = (jnp.arange(LANES, dtype=jnp.int32)[:, None] == head_of_lane[None, :]).astype(BF16)
    row = lambda width: pl.BlockSpec((tm, width), lambda i: (i, 0))
    return pl.pallas_call(
        _merge_kernel,
        out_shape=jax.ShapeDtypeStruct((n, ATTN_WIDTH), BF16),
        grid=(n // tm,),
        in_specs=[row(ATTN_WIDTH)] * len(outs) + [row(LANES)] * len(lses)
                 + [pl.BlockSpec((LANES, ATTN_WIDTH), lambda i: (0, 0))],
        out_specs=row(ATTN_WIDTH),
        compiler_params=_cparams(("parallel",)),
        name="merge_branches",
    )(*outs, *lses, expand)


def _dilated_attention(q, k, v, batch, seq):
    outs, lses = [], []
    for window, dil in DILATED_BRANCHES:
        o, lse = _window_attention(q, k, v, None, batch, seq, dil=dil, span=window // dil + 1, emit_lse=True)
        outs.append(o)
        lses.append(lse)
    return _merge_branches(outs, lses)


IDX_CHUNK = 512
ATT_CHUNK = 512
INT_MIN = -2 ** 31
F32_BITS = 32
BITS_PER_CHECK = 4


def _sortable_to_f32(t):
    bits = jnp.where(t >= 0, t, t ^ jnp.int32(0x7FFFFFFF))
    return lax.bitcast_convert_type(bits, F32)


def _dsa_kernel(q_ref, qi_ref, wi_ref, ki_ref, k_ref, vt_ref, o_ref, sc_ref, *refs, topk):
    qx_ref, acc_ref = refs[0], refs[4]
    i = pl.program_id(1)
    t0 = i * BLOCK
    n_idx = (t0 + BLOCK + IDX_CHUNK - 1) // IDX_CHUNK
    n_att = (t0 + BLOCK + ATT_CHUNK - 1) // ATT_CHUNK

    qis = jnp.concatenate(
        [qi_ref[:, h * IDX_DIM:(h + 1) * IDX_DIM] for h in range(IDX_HEADS)], axis=0)
    w_t = (wi_ref[...] * IDX_W_SCALE).T
    key = lax.broadcasted_iota(jnp.int32, (IDX_CHUNK, BLOCK), 0)
    qry = lax.broadcasted_iota(jnp.int32, (IDX_CHUNK, BLOCK), 1)

    def idx_body(c, carry):
        start = pl.multiple_of(c * IDX_CHUNK, IDX_CHUNK)
        kic = ki_ref[pl.ds(start, IDX_CHUNK), 0:IDX_DIM]
        rel = jnp.maximum(
            lax.dot_general(kic, qis, (((1,), (1,)), ((), ())), preferred_element_type=F32), 0.0)
        score = jnp.zeros((IDX_CHUNK, BLOCK), F32)
        for h in range(IDX_HEADS):
            score = score + rel[:, h * BLOCK:(h + 1) * BLOCK] * w_t[h:h + 1, :]
        sc_ref[pl.ds(start, IDX_CHUNK), :] = jnp.where(start + key <= t0 + qry, score, NEG_INF)
        return carry

    lax.fori_loop(0, n_idx, idx_body, 0)

    def count_hits(hit_fn):
        def cbody(c, cnt):
            start = pl.multiple_of(c * IDX_CHUNK, IDX_CHUNK)
            hit = hit_fn(start, sc_ref[pl.ds(start, IDX_CHUNK), :])
            parts = [hit[r * SUBLANES:(r + 1) * SUBLANES, :] for r in range(IDX_CHUNK // SUBLANES)]
            while len(parts) > 1:
                parts = [a + b for a, b in zip(parts[0::2], parts[1::2])]
            return cnt + parts[0]
        cnt = lax.fori_loop(0, n_idx, cbody, jnp.zeros((SUBLANES, BLOCK), F32))
        return jnp.sum(cnt, axis=0, keepdims=True)

    def count_ge(cand_f):
        return count_hits(lambda start, blk: jnp.where(blk >= cand_f, 1.0, 0.0))

    def bit_cond(state):
        b, _, _, n_open = state
        return (b < F32_BITS) & (n_open > 0)

    def bit_body(state):
        b0, t, done, _ = state
        for k in range(BITS_PER_CHECK):
            b = b0 + k
            bit = lax.shift_left(jnp.int32(1), F32_BITS - 1 - b)
            cand = jnp.where(b == 0, jnp.zeros_like(t), t | bit)
            cnt = count_ge(_sortable_to_f32(cand))
            take = (cnt >= float(topk)) & (done == 0)
            t = jnp.where(take, cand, t)
            done = jnp.where(take & (cnt == float(topk)), 1, done)
        n_open = jnp.sum(1 - done)
        return b0 + BITS_PER_CHECK, t, done, n_open

    state = (jnp.int32(0), jnp.full((1, BLOCK), INT_MIN, jnp.int32),
             jnp.zeros((1, BLOCK), jnp.int32), jnp.int32(BLOCK))
    _, t_int, _, n_open = lax.while_loop(bit_cond, bit_body, state)
    thr = jnp.maximum(_sortable_to_f32(t_int), jnp.float32(NEG_INF * 0.5))

    def tie_bound():
        need = float(topk) - count_hits(lambda start, blk: jnp.where(blk > thr, 1.0, 0.0))

        def jbody(b, lo):
            cand = lo + lax.shift_left(jnp.int32(1), idx_bits - 1 - b)
            kept = count_hits(lambda start, blk: jnp.where(
                blk == thr, jnp.where(start + key <= cand, 1.0, 0.0), 0.0))
            return jnp.where(kept < need, cand, lo)

        return lax.fori_loop(0, idx_bits, jbody, jnp.full((1, BLOCK), -1, jnp.int32)) + 1

    idx_bits = int(sc_ref.shape[0]).bit_length()
    last_tie = lax.cond(n_open > 0, tie_bound, lambda: jnp.full((1, BLOCK), sc_ref.shape[0], jnp.int32))

    _fill_q_ext(q_ref, qx_ref)

    def make_bias(c):
        start = pl.multiple_of(c * ATT_CHUNK, ATT_CHUNK)
        sc = sc_ref[pl.ds(start, ATT_CHUNK), :]
        tied = jnp.where(start + key <= last_tie, 0.0, NEG_INF)
        return jnp.where(sc > thr, 0.0, jnp.where(sc == thr, tied, NEG_INF)).astype(BF16)

    _flash_chunks(0, n_att, ATT_CHUNK, make_bias, k_ref, vt_ref, refs)
    _finish_flash(o_ref, acc_ref)


DSA_QI_OFF = ATTN_WIDTH
DSA_K_OFF = DSA_QI_OFF + IDX_HEADS * IDX_DIM
DSA_KI_OFF = DSA_K_OFF + KDUP_WIDTH
DSA_WI_OFF = DSA_KI_OFF + LANES
DSA_V_OFF = DSA_WI_OFF + LANES
DSA_WIDTH = DSA_V_OFF + KV_WIDTH
DSA_TN = 2048


def _dsa_attention(proj, wi, vt, batch, seq):
    nb = seq // BLOCK
    n = batch * seq
    topk = min(TOPK_MAX, seq // 4)
    qi_w = IDX_HEADS * IDX_DIM
    chunk = max(IDX_CHUNK, ATT_CHUNK)
    seq_pad = -(-seq // chunk) * chunk
    return pl.pallas_call(
        functools.partial(_dsa_kernel, topk=topk),
        out_shape=jax.ShapeDtypeStruct((n, ATTN_WIDTH), BF16),
        grid=(batch, nb),
        in_specs=[
            pl.BlockSpec((BLOCK, ATTN_WIDTH), lambda b, i: (b * nb + i, 0)),
            pl.BlockSpec((BLOCK, qi_w), lambda b, i: (b * nb + i, DSA_QI_OFF // qi_w)),
            pl.BlockSpec((BLOCK, LANES), lambda b, i: (b * nb + i, 0)),
            pl.BlockSpec((seq, LANES), lambda b, i: (b, DSA_KI_OFF // LANES)),
            pl.BlockSpec((seq, KDUP_WIDTH), lambda b, i: (b, DSA_K_OFF // KDUP_WIDTH)),
            pl.BlockSpec((None, KV_WIDTH, seq), lambda b, i: (b, 0, 0)),
        ],
        out_specs=pl.BlockSpec((BLOCK, ATTN_WIDTH), lambda b, i: (b * nb + i, 0)),
        scratch_shapes=[pltpu.VMEM((seq_pad, BLOCK), F32)] + _flash_scratch(ATT_CHUNK),
        compiler_params=_cparams(("parallel", "arbitrary")),
        name="dsa_attention",
    )(proj, proj, wi, proj, proj, vt)


def _rope_table(positions):
    inv = ROPE_THETA ** (-jnp.arange(0, ROPE_DIM, 2, dtype=F32) / ROPE_DIM)
    ang = positions.astype(F32).reshape(-1, 1) * inv[None, :]
    cos, sin = jnp.cos(ang), jnp.sin(ang)
    n = ang.shape[0]
    pad = HEAD_DIM - ROPE_DIM
    cos_h = jnp.concatenate([cos, cos, jnp.ones((n, pad), F32)], axis=1)
    lo_h = jnp.concatenate([-sin, jnp.zeros((n, HEAD_DIM - ROPE_HALF), F32)], axis=1)
    hi_h = jnp.concatenate([jnp.zeros((n, ROPE_HALF), F32), sin, jnp.zeros((n, pad), F32)], axis=1)
    reps = LANES // HEAD_DIM
    return jnp.concatenate([jnp.tile(cos_h, (1, reps)), jnp.tile(lo_h, (1, reps)),
                            jnp.tile(hi_h, (1, reps))], axis=1)


def _dup_heads(wk):
    d = wk.shape[0]
    w4 = wk.reshape(d, N_KV_HEADS, 1, HEAD_DIM)
    return jnp.broadcast_to(w4, (d, N_KV_HEADS, LANES // HEAD_DIM, HEAD_DIM)).reshape(d, KDUP_WIDTH)


QKV_WIDTH_EXT = ATTN_WIDTH + KDUP_WIDTH + KV_WIDTH


def _qkv_weight(w_in):
    o = ATTN_WIDTH
    return jnp.concatenate([w_in[:, :o], _dup_heads(w_in[:, o:o + KV_WIDTH]),
                            w_in[:, o + KV_WIDTH:o + 2 * KV_WIDTH]], axis=1)


def _qkv_colscale():
    return jnp.concatenate([jnp.full((1, ATTN_WIDTH), SCALE * LOG2E, F32),
                            jnp.ones((1, KDUP_WIDTH + KV_WIDTH), F32)], axis=1)


def _dsa_weight(w_in):
    d = w_in.shape[0]
    o = ATTN_WIDTH
    wq = w_in[:, :o]
    wk = w_in[:, o:o + KV_WIDTH]
    wv = w_in[:, o + KV_WIDTH:o + 2 * KV_WIDTH]
    o += 2 * KV_WIDTH
    wqi = w_in[:, o:o + IDX_HEADS * IDX_DIM]
    o += IDX_HEADS * IDX_DIM
    wki = w_in[:, o:o + IDX_DIM]
    o += IDX_DIM
    wwi = w_in[:, o:o + IDX_HEADS]
    z = lambda c: jnp.zeros((d, c), w_in.dtype)
    return jnp.concatenate([wq, wqi, _dup_heads(wk), wki, z(LANES - IDX_DIM),
                            wwi, z(LANES - IDX_HEADS), wv], axis=1)


def _dsa_colscale():
    return jnp.concatenate([jnp.full((1, ATTN_WIDTH), SCALE * LOG2E, F32),
                            jnp.full((1, IDX_HEADS * IDX_DIM), IDX_SCALE, F32),
                            jnp.ones((1, DSA_WIDTH - DSA_K_OFF), F32)], axis=1)


def _v_transposed(proj, v_off, batch, seq):
    v = proj[:, v_off:v_off + KV_WIDTH].reshape(batch, seq, KV_WIDTH)
    return jnp.swapaxes(v, 1, 2)


def kernel(x, positions, norm_attn, norm_mlp, w_up, w_down, final_norm,
           a_w_in, a_sinks, a_w_out, b_w_in, b_w_out, c_w_in, c_w_out):
    batch, seq, d = x.shape
    depth = norm_attn.shape[0]
    x2 = x.reshape(batch * seq, d)
    rope_tab = _rope_table(positions)
    qkv_scale = _qkv_colscale()
    qkv_rope_groups = (ATTN_WIDTH + KDUP_WIDTH) // LANES
    for i in range(depth):
        j, kind = divmod(i, 3)
        if kind == 1:
            proj, wi = _norm_proj(x2, norm_attn[i], _dsa_weight(b_w_in[j]).astype(BF16),
                                  _dsa_colscale(), rope_tab, tn=DSA_TN,
                                  n_rope_groups=DSA_WI_OFF // LANES,
                                  aux_group=(DSA_WI_OFF % DSA_TN) // LANES)
            o = _dsa_attention(proj, wi, _v_transposed(proj, DSA_V_OFF, batch, seq), batch, seq)
            w_out = b_w_out[j]
        else:
            w_in = a_w_in[j] if kind == 0 else c_w_in[j]
            q, k, v = _norm_proj(x2, norm_attn[i], _qkv_weight(w_in).astype(BF16), qkv_scale, rope_tab,
                                 tn=QKV_WIDTH_EXT, n_rope_groups=qkv_rope_groups,
                                 out_widths=(ATTN_WIDTH, KDUP_WIDTH, KV_WIDTH))
            if kind == 0:
                o = _window_attention(q, k, v, a_sinks[j], batch, seq, dil=1, span=SWA_WINDOW, emit_lse=False)
                w_out = a_w_out[j]
            else:
                o = _dilated_attention(q, k, v, batch, seq)
                w_out = c_w_out[j]
        x2 = _out_proj(o, w_out.astype(BF16), x2)
        x2 = _mlp(x2, norm_mlp[i], w_up[i].astype(BF16), w_down[i].astype(BF16),
                  final_norm if i == depth - 1 else None)
    return x2.reshape(batch, seq, d)
```

```python
import functools

import jax
import jax.numpy as jnp
from jax import lax
from jax.experimental import pallas as pl
from jax.experimental.pallas import tpu as pltpu

HEAD_DIM = 64
N_KV_HEADS = 4
GROUP = 8
N_HEADS = N_KV_HEADS * GROUP
ATTN_WIDTH = N_HEADS * HEAD_DIM
KV_WIDTH = N_KV_HEADS * HEAD_DIM
ROPE_DIM = HEAD_DIM // 4
ROPE_HALF = ROPE_DIM // 2
ROPE_THETA = 500000.0
SCALE = HEAD_DIM ** -0.5
BLOCK = 128
SWA_WINDOW = 128
IDX_HEADS = 16
IDX_DIM = 64
IDX_SCALE = IDX_DIM ** -0.5
IDX_W_SCALE = IDX_HEADS ** -0.5
TOPK_MAX = 256
DILATED_BRANCHES = ((128, 1), (512, 4), (2048, 16))
NORM_EPS = 1e-5
NEG_INF = -1e30

LANES = 128
BF16_ROWS = 16
SUBLANES = 8
MXU_TILE = 256
LOG2E = 1.4426950408889634
VMEM_LIMIT = 52 * 1024 * 1024

BF16 = jnp.bfloat16
F32 = jnp.float32

KDUP_WIDTH = N_KV_HEADS * LANES
QROWS = GROUP * BLOCK
EXT = 2 * LANES
VT_ROWS = HEAD_DIM + BF16_ROWS


def _cparams(sem):
    return pltpu.CompilerParams(dimension_semantics=sem, vmem_limit_bytes=VMEM_LIMIT)


def _norm_proj_kernel(x_ref, g_ref, w_ref, cs_ref, rope_ref, *rest,
                      n_rope_groups, groups_per_tile, out_groups, aux_group):
    o_refs = rest[:len(out_groups)]
    aux_ref = rest[len(out_groups)] if aux_group is not None else None
    h_ref = rest[-1]
    j = pl.program_id(1)

    @pl.when(j == 0)
    def _():
        x = x_ref[...]
        ms = jnp.mean(x * x, axis=-1, keepdims=True)
        h_ref[...] = ((x * lax.rsqrt(ms + NORM_EPS)) * g_ref[...]).astype(BF16)

    cos_t = rope_ref[:, 0:LANES]
    sin_lo = rope_ref[:, LANES:2 * LANES]
    sin_hi = rope_ref[:, 2 * LANES:3 * LANES]
    h = h_ref[...]
    sub_groups = MXU_TILE // LANES
    n_sub = groups_per_tile // sub_groups
    dest = [(o_ref, k) for o_ref, cnt in zip(o_refs, out_groups) for k in range(cnt)]

    def project(s):
        cols = slice(s * MXU_TILE, (s + 1) * MXU_TILE)
        return jnp.dot(h, w_ref[:, cols], preferred_element_type=F32) * cs_ref[:, cols]

    def finish(s, acc):
        for gg in range(sub_groups):
            g = s * sub_groups + gg
            a = acc[:, gg * LANES:(gg + 1) * LANES]
            r = (a * cos_t + pltpu.roll(a, LANES - ROPE_HALF, 1) * sin_lo
                 + pltpu.roll(a, ROPE_HALF, 1) * sin_hi)
            is_rope = (j * groups_per_tile + g) < n_rope_groups
            o_ref, k = dest[g]
            o_ref[:, k * LANES:(k + 1) * LANES] = jnp.where(is_rope, r, a).astype(o_ref.dtype)
            if aux_ref is not None and g == aux_group:
                aux_ref[...] = a

    acc = project(0)
    for s in range(1, n_sub):
        nxt = project(s)
        finish(s - 1, acc)
        acc = nxt
    finish(n_sub - 1, acc)


def _norm_proj(x2, g, w, colscale, rope_tab, *, tn, n_rope_groups, aux_group=None, out_widths=None, tm=512):
    n, d = x2.shape
    width = w.shape[1]
    gpt = tn // LANES
    if out_widths is None:
        out_widths = (tn,)
        out_shape = [jax.ShapeDtypeStruct((n, width), BF16)]
        out_specs = [pl.BlockSpec((tm, tn), lambda i, j: (i, j))]
    else:
        assert tn == width == sum(out_widths)
        out_shape = [jax.ShapeDtypeStruct((n, ow), BF16) for ow in out_widths]
        out_specs = [pl.BlockSpec((tm, ow), lambda i, j: (i, 0)) for ow in out_widths]
    kern = functools.partial(_norm_proj_kernel, n_rope_groups=n_rope_groups, groups_per_tile=gpt,
                             out_groups=tuple(ow // LANES for ow in out_widths), aux_group=aux_group)
    if aux_group is not None:
        out_shape.append(jax.ShapeDtypeStruct((n, LANES), F32))
        out_specs.append(pl.BlockSpec((tm, LANES), lambda i, j: (i, 0)))
    res = pl.pallas_call(
        kern,
        out_shape=out_shape,
        grid=(n // tm, width // tn),
        in_specs=[
            pl.BlockSpec((tm, d), lambda i, j: (i, 0)),
            pl.BlockSpec((1, d), lambda i, j: (0, 0)),
            pl.BlockSpec((d, tn), lambda i, j: (0, j)),
            pl.BlockSpec((1, tn), lambda i, j: (0, j)),
            pl.BlockSpec((tm, 3 * LANES), lambda i, j: (i, 0)),
        ],
        out_specs=out_specs,
        scratch_shapes=[pltpu.VMEM((tm, d), BF16)],
        compiler_params=_cparams(("parallel", "arbitrary")),
        name="norm_proj",
    )(x2, g.reshape(1, d), w, colscale, rope_tab)
    return res


def _out_proj_kernel(o_ref, w_ref, x_ref, y_ref):
    y_ref[...] = x_ref[...] + jnp.dot(o_ref[...], w_ref[...], preferred_element_type=F32)


def _out_proj(o, w, x2, *, tm=512, tn=2048):
    n, k = o.shape
    d = w.shape[1]
    return pl.pallas_call(
        _out_proj_kernel,
        out_shape=jax.ShapeDtypeStruct((n, d), F32),
        grid=(n // tm, d // tn),
        in_specs=[
            pl.BlockSpec((tm, k), lambda i, j: (i, 0)),
            pl.BlockSpec((k, tn), lambda i, j: (0, j)),
            pl.BlockSpec((tm, tn), lambda i, j: (i, j)),
        ],
        out_specs=pl.BlockSpec((tm, tn), lambda i, j: (i, j)),
        compiler_params=_cparams(("parallel", "arbitrary")),
        name="out_proj",
    )(o, w, x2)


def _rms(x, g):
    ms = jnp.mean(x * x, axis=-1, keepdims=True)
    return (x * lax.rsqrt(ms + NORM_EPS)) * g


def _mlp_kernel(x_ref, g_ref, wu_ref, wd_ref, *rest, out_norm):
    y_ref, h_ref = rest[-2:]
    f = pl.program_id(1)

    @pl.when(f == 0)
    def _():
        x = x_ref[...]
        h_ref[...] = _rms(x, g_ref[...]).astype(BF16)
        y_ref[...] = x

    u = jnp.dot(h_ref[...], wu_ref[...], preferred_element_type=F32)
    u = jnp.maximum(u, 0.0)
    a = (u * u).astype(BF16)
    y_ref[...] += jnp.dot(a, wd_ref[...], preferred_element_type=F32)

    if out_norm:
        @pl.when(f == pl.num_programs(1) - 1)
        def _():
            y_ref[...] = _rms(y_ref[...], rest[0][...])


def _mlp(x2, g, w_up, w_down, out_gain=None, *, tm=512, tf=1024):
    n, d = x2.shape
    d_ff = w_up.shape[1]
    vec = pl.BlockSpec((1, d), lambda i, f: (0, 0))
    in_specs = [
        pl.BlockSpec((tm, d), lambda i, f: (i, 0)),
        vec,
        pl.BlockSpec((d, tf), lambda i, f: (0, f)),
        pl.BlockSpec((tf, d), lambda i, f: (f, 0)),
    ]
    args = [x2, g.reshape(1, d), w_up, w_down]
    if out_gain is not None:
        in_specs.append(vec)
        args.append(out_gain.reshape(1, d))
    return pl.pallas_call(
        functools.partial(_mlp_kernel, out_norm=out_gain is not None),
        out_shape=jax.ShapeDtypeStruct((n, d), F32),
        grid=(n // tm, d_ff // tf),
        in_specs=in_specs,
        out_specs=pl.BlockSpec((tm, d), lambda i, f: (i, 0)),
        scratch_shapes=[pltpu.VMEM((tm, d), BF16)],
        compiler_params=_cparams(("parallel", "arbitrary")),
        name="mlp",
    )(*args)


def _fill_q_ext(q_ref, qx_ref, row0=0, slot=()):
    lane = lax.broadcasted_iota(jnp.int32, (BLOCK, LANES), 1)
    row = lax.broadcasted_iota(jnp.int32, (BLOCK, LANES), 0)
    eye = jnp.where(lane == row, 1.0, 0.0).astype(BF16)
    low = lane < HEAD_DIM
    for kv in range(N_KV_HEADS):
        for g in range(GROUP):
            h = kv * GROUP + g
            tile = q_ref[row0:row0 + BLOCK, (h // 2) * LANES:(h // 2 + 1) * LANES]
            keep = low if h % 2 == 0 else jnp.logical_not(low)
            rows = slice(g * BLOCK, (g + 1) * BLOCK)
            qx_ref[(*slot, kv, rows, slice(0, LANES))] = jnp.where(keep, tile, jnp.zeros_like(tile))
            qx_ref[(*slot, kv, rows, slice(LANES, EXT))] = eye


def _scores_t(k_tile, bias, qx):
    k_ext = jnp.concatenate([k_tile, bias], axis=1)
    return lax.dot_general(k_ext, qx, (((1,), (1,)), ((), ())), preferred_element_type=F32)


def _vt_ext(vt):
    return jnp.concatenate([vt, jnp.ones((BF16_ROWS, vt.shape[1]), BF16)], axis=0)


def _flash_chunks(c_lo, c_end, tc, make_bias, k_ref, vt_ref, refs):
    qx_ref, s_ref, cm_ref, m_ref, acc_ref, bias_ref = refs
    m_ref[...] = jnp.full(m_ref.shape, NEG_INF, F32)
    acc_ref[...] = jnp.zeros(acc_ref.shape, F32)
    c_last = c_end - 1

    def put_bias(c):
        bias_ref[c & 1] = make_bias(c)

    def issue_scores(c, kv):
        start = pl.multiple_of(c * tc, tc)
        k_tile = k_ref[pl.ds(start, tc), kv * LANES:(kv + 1) * LANES]
        s_t = _scores_t(k_tile, bias_ref[c & 1], qx_ref[kv])
        s_ref[kv % 2] = s_t
        cm_ref[kv % 2] = jnp.max(s_t, axis=0, keepdims=True)

    def consume(c, kv):
        slot = kv % 2
        start = pl.multiple_of(c * tc, tc)
        vt = _vt_ext(vt_ref[kv * HEAD_DIM:(kv + 1) * HEAD_DIM, pl.ds(start, tc)])
        m_old = m_ref[kv]
        m_new = jnp.maximum(m_old, cm_ref[slot])
        alpha = jnp.exp2(m_old - m_new)
        m_ref[kv] = m_new
        for n in range(QROWS // MXU_TILE):
            cols = slice(n * MXU_TILE, (n + 1) * MXU_TILE)
            part = alpha[:, cols] * acc_ref[kv, :, cols]
            for kk in range(tc // MXU_TILE):
                rows = slice(kk * MXU_TILE, (kk + 1) * MXU_TILE)
                p = jnp.exp2(s_ref[slot, rows, cols] - m_new[:, cols])
                part = part + jnp.dot(vt[:, rows], p.astype(BF16), preferred_element_type=F32)
            acc_ref[kv, :, cols] = part

    put_bias(c_lo)
    issue_scores(c_lo, 0)

    def body(c, carry):
        c_next = jnp.minimum(c + 1, c_last)
        put_bias(c_next)
        for kv in range(N_KV_HEADS):
            if kv + 1 < N_KV_HEADS:
                issue_scores(c, kv + 1)
            else:
                issue_scores(c_next, 0)
            consume(c, kv)
        return carry

    lax.fori_loop(c_lo, c_end, body, 0)


def _store_out(o_ref, kv, o_t, row0=0):
    for gp in range(GROUP // 2):
        pair = jnp.concatenate([o_t[:, (2 * gp) * BLOCK:(2 * gp + 1) * BLOCK],
                                o_t[:, (2 * gp + 1) * BLOCK:(2 * gp + 2) * BLOCK]], axis=0)
        col = (kv * GROUP + 2 * gp) * HEAD_DIM
        o_ref[row0:row0 + BLOCK, col:col + LANES] = pair.T.astype(o_ref.dtype)


def _finish_flash(o_ref, acc_ref):
    for kv in range(N_KV_HEADS):
        acc = acc_ref[kv]
        _store_out(o_ref, kv, acc[0:HEAD_DIM, :] / acc[HEAD_DIM:HEAD_DIM + 1, :])


def _bias_of(valid):
    return jnp.where(valid, 0.0, NEG_INF).astype(BF16)


def _qx_scratch():
    return pltpu.VMEM((N_KV_HEADS, QROWS, EXT), BF16)


def _flash_scratch(tc):
    return [
        _qx_scratch(),
        pltpu.VMEM((2, tc, QROWS), F32),
        pltpu.VMEM((2, 1, QROWS), F32),
        pltpu.VMEM((N_KV_HEADS, 1, QROWS), F32),
        pltpu.VMEM((N_KV_HEADS, VT_ROWS, QROWS), F32),
        pltpu.VMEM((2, tc, BLOCK), BF16),
    ]


WIN_TILES = 4


def _window_kernel(*refs, span, tiles, with_sinks, emit_lse):
    refs = list(refs)
    sink_ref = refs.pop(0) if with_sinks else None
    q_ref, kp_ref, kc_ref, vp_ref, vc_ref, o_ref = refs[:6]
    lse_ref = refs[6] if emit_lse else None
    qx_ref, sw_ref = refs[-2:]
    first = pl.program_id(2) == 0
    key = lax.broadcasted_iota(jnp.int32, (2 * BLOCK, BLOCK), 0)
    qry = lax.broadcasted_iota(jnp.int32, (2 * BLOCK, BLOCK), 1)
    dist = qry + BLOCK - key
    band = (dist >= 0) & (dist < span)
    bias_any = _bias_of(band)
    bias_first = _bias_of(band & ((key >= BLOCK) | jnp.logical_not(first)))

    def v_t(block):
        return block.astype(F32).T.astype(BF16)

    ones = jnp.ones((BF16_ROWS, 2 * BLOCK), BF16)
    vts = [v_t(vp_ref[...])] + [v_t(vc_ref[j * BLOCK:(j + 1) * BLOCK, :]) for j in range(tiles)]

    def issue_scores(j):
        _fill_q_ext(q_ref, qx_ref, j * BLOCK, (j,))
        for kv in range(N_KV_HEADS):
            lanes = slice(kv * LANES, (kv + 1) * LANES)
            k_prev = kp_ref[:, lanes] if j == 0 else kc_ref[(j - 1) * BLOCK:j * BLOCK, lanes]
            k_tile = jnp.concatenate([k_prev, kc_ref[j * BLOCK:(j + 1) * BLOCK, lanes]], axis=0)
            sw_ref[j, kv] = _scores_t(k_tile, bias_first if j == 0 else bias_any, qx_ref[j, kv])

    def consume(j):
        vt2 = jnp.concatenate([vts[j], vts[j + 1]], axis=1)
        lse_rows = []
        for kv in range(N_KV_HEADS):
            s_t = sw_ref[j, kv]
            m = jnp.max(s_t, axis=0, keepdims=True)
            if with_sinks:
                sink = jnp.concatenate(
                    [jnp.full((1, BLOCK), sink_ref[kv * GROUP + g] * LOG2E, F32) for g in range(GROUP)], axis=1)
                m = jnp.maximum(m, sink)
            p = jnp.exp2(s_t - m).astype(BF16)
            vt = jnp.concatenate([vt2[kv * HEAD_DIM:(kv + 1) * HEAD_DIM, :], ones], axis=0)
            acc = jnp.dot(vt, p, preferred_element_type=F32)
            den = acc[HEAD_DIM:HEAD_DIM + 1, :]
            if with_sinks:
                den = den + jnp.exp2(sink - m)
            _store_out(o_ref, kv, acc[0:HEAD_DIM, :] / den, j * BLOCK)
            if emit_lse:
                lse = m + jnp.log2(den)
                lse_rows += [lse[:, g * BLOCK:(g + 1) * BLOCK] for g in range(GROUP)]
        if emit_lse:
            pad = jnp.zeros((BLOCK - N_HEADS, BLOCK), F32)
            lse_ref[j * BLOCK:(j + 1) * BLOCK, :] = jnp.concatenate(lse_rows + [pad], axis=0).T

    issue_scores(0)
    for j in range(tiles):
        if j + 1 < tiles:
            issue_scores(j + 1)
        consume(j)


RELAYOUT_ROWS = 512


def _to_residue_kernel(x_ref, o_ref, stage_ref, *, dil):
    groups = x_ref.shape[1] // LANES
    for g in range(groups):
        stage_ref[g] = x_ref[:, g * LANES:(g + 1) * LANES].astype(F32)
    for r in range(dil):
        for g in range(groups):
            rows = stage_ref[g, pl.ds(r, RELAYOUT_ROWS // dil, stride=dil), :]
            o_ref[r, :, g * LANES:(g + 1) * LANES] = rows.astype(o_ref.dtype)


def _to_token_kernel(x_ref, o_ref, stage_ref, *, dil):
    groups = o_ref.shape[1] // LANES
    for r in range(dil):
        for g in range(groups):
            stage_ref[g, pl.ds(r, RELAYOUT_ROWS // dil, stride=dil), :] = (
                x_ref[r, :, g * LANES:(g + 1) * LANES].astype(F32))
    for g in range(groups):
        o_ref[:, g * LANES:(g + 1) * LANES] = stage_ref[g].astype(o_ref.dtype)


def _relayout_specs(batch, seq, dil, w):
    steps = seq // RELAYOUT_ROWS
    token = pl.BlockSpec((RELAYOUT_ROWS, w), lambda b, u: (b * steps + u, 0))
    residue = pl.BlockSpec((None, dil, RELAYOUT_ROWS // dil, w), lambda b, u: (b, 0, u, 0))
    return (batch, steps), token, residue


def _by_residue(a, batch, seq, dil):
    w = a.shape[1]
    if dil == 1:
        return a.reshape(batch, 1, seq, w)
    grid, token, residue = _relayout_specs(batch, seq, dil, w)
    return pl.pallas_call(
        functools.partial(_to_residue_kernel, dil=dil),
        out_shape=jax.ShapeDtypeStruct((batch, dil, seq // dil, w), a.dtype),
        grid=grid, in_specs=[token], out_specs=residue,
        scratch_shapes=[pltpu.VMEM((w // LANES, RELAYOUT_ROWS, LANES), F32)],
        compiler_params=_cparams(("parallel", "parallel")),
        name="to_residue",
    )(a)


def _by_token(a, batch, seq, dil):
    w = a.shape[-1]
    if dil == 1:
        return a.reshape(batch * seq, w)
    grid, token, residue = _relayout_specs(batch, seq, dil, w)
    return pl.pallas_call(
        functools.partial(_to_token_kernel, dil=dil),
        out_shape=jax.ShapeDtypeStruct((batch * seq, w), a.dtype),
        grid=grid, in_specs=[residue], out_specs=token,
        scratch_shapes=[pltpu.VMEM((w // LANES, RELAYOUT_ROWS, LANES), F32)],
        compiler_params=_cparams(("parallel", "parallel")),
        name="to_token",
    )(a)


def _window_attention(q, k, v, sinks, batch, seq, *, dil, span, emit_lse):
    sub = seq // dil
    tiles = max(1, min(WIN_TILES, sub // BLOCK // 2))
    rows = tiles * BLOCK
    with_sinks = sinks is not None
    cur = lambda b, r, u: (b, r, u, 0)
    prev = lambda b, r, u: (b, r, jnp.maximum(u * tiles - 1, 0), 0)
    in_specs = [
        pl.BlockSpec((None, None, rows, ATTN_WIDTH), cur),
        pl.BlockSpec((None, None, BLOCK, KDUP_WIDTH), prev),
        pl.BlockSpec((None, None, rows, KDUP_WIDTH), cur),
        pl.BlockSpec((None, None, BLOCK, KV_WIDTH), prev),
        pl.BlockSpec((None, None, rows, KV_WIDTH), cur),
    ]
    qr, kr, vr = (_by_residue(a, batch, seq, dil) for a in (q, k, v))
    args = [qr, kr, kr, vr, vr]
    if with_sinks:
        in_specs.insert(0, pl.BlockSpec(memory_space=pltpu.SMEM))
        args.insert(0, sinks)
    out_shape = [jax.ShapeDtypeStruct((batch, dil, sub, ATTN_WIDTH), BF16)]
    out_specs = [pl.BlockSpec((None, None, rows, ATTN_WIDTH), cur)]
    if emit_lse:
        out_shape.append(jax.ShapeDtypeStruct((batch, dil, sub, LANES), F32))
        out_specs.append(pl.BlockSpec((None, None, rows, LANES), cur))
    res = pl.pallas_call(
        functools.partial(_window_kernel, span=span, tiles=tiles, with_sinks=with_sinks, emit_lse=emit_lse),
        out_shape=out_shape,
        grid=(batch, dil, sub // rows),
        in_specs=in_specs,
        out_specs=out_specs,
        scratch_shapes=[pltpu.VMEM((tiles, N_KV_HEADS, QROWS, EXT), BF16),
                        pltpu.VMEM((tiles, N_KV_HEADS, 2 * BLOCK, QROWS), F32)],
        compiler_params=_cparams(("parallel", "parallel", "arbitrary")),
        name="window_attention",
    )(*args)
    o = _by_token(res[0], batch, seq, dil)
    return (o, _by_token(res[1], batch, seq, dil)) if emit_lse else o


def _merge_kernel(*refs):
    n_br = (len(refs) - 2) // 2
    o_refs, l_refs, e_ref, out_ref = refs[:n_br], refs[n_br:2 * n_br], refs[-2], refs[-1]
    lses = [l[...] for l in l_refs]
    top = functools.reduce(jnp.maximum, lses)
    ws = [jnp.exp2(l - top) for l in lses]
    tot = functools.reduce(lambda a, b: a + b, ws)
    out = None
    for w, o_ref in zip(ws, o_refs):
        wn = w / tot
        hi = wn.astype(BF16)
        lo = (wn - hi.astype(F32)).astype(BF16)
        spread = (jnp.dot(hi, e_ref[...], preferred_element_type=F32)
                  + jnp.dot(lo, e_ref[...], preferred_element_type=F32))
        term = spread * o_ref[...].astype(F32)
        out = term if out is None else out + term
    out_ref[...] = out.astype(out_ref.dtype)


def _merge_branches(outs, lses, *, tm=512):
    n = outs[0].shape[0]
    head_of_lane = jnp.arange(ATTN_WIDTH, dtype=jnp.int32) // HEAD_DIM
    expand = (jnp.arange(LANES, dtype=jnp.int32)[:, None] == head_of_lane[None, :]).astype(BF16)
    row = lambda width: pl.BlockSpec((tm, width), lambda i: (i, 0))
    return pl.pallas_call(
        _merge_kernel,
        out_shape=jax.ShapeDtypeStruct((n, ATTN_WIDTH), BF16),
        grid=(n // tm,),
        in_specs=[row(ATTN_WIDTH)] * len(outs) + [row(LANES)] * len(lses)
                 + [pl.BlockSpec((LANES, ATTN_WIDTH), lambda i: (0, 0))],
        out_specs=row(ATTN_WIDTH),
        compiler_params=_cparams(("parallel",)),
        name="merge_branches",
    )(*outs, *lses, expand)


def _dilated_attention(q, k, v, batch, seq):
    outs, lses = [], []
    for window, dil in DILATED_BRANCHES:
        o, lse = _window_attention(q, k, v, None, batch, seq, dil=dil, span=window // dil + 1, emit_lse=True)
        outs.append(o)
        lses.append(lse)
    return _merge_branches(outs, lses)


IDX_CHUNK = 512
ATT_CHUNK = 512
INT_MIN = -2 ** 31
F32_BITS = 32
BITS_PER_CHECK = 4


def _sortable_to_f32(t):
    bits = jnp.where(t >= 0, t, t ^ jnp.int32(0x7FFFFFFF))
    return lax.bitcast_convert_type(bits, F32)


def _dsa_kernel(q_ref, qi_ref, wi_ref, ki_ref, k_ref, vt_ref, o_ref, sc_ref, *refs, topk):
    qx_ref, acc_ref = refs[0], refs[4]
    i = pl.program_id(1)
    t0 = i * BLOCK
    n_idx = (t0 + BLOCK + IDX_CHUNK - 1) // IDX_CHUNK
    n_att = (t0 + BLOCK + ATT_CHUNK - 1) // ATT_CHUNK

    qis = jnp.concatenate(
        [qi_ref[:, h * IDX_DIM:(h + 1) * IDX_DIM] for h in range(IDX_HEADS)], axis=0)
    w_t = (wi_ref[...] * IDX_W_SCALE).T
    key = lax.broadcasted_iota(jnp.int32, (IDX_CHUNK, BLOCK), 0)
    qry = lax.broadcasted_iota(jnp.int32, (IDX_CHUNK, BLOCK), 1)

    def idx_body(c, carry):
        start = pl.multiple_of(c * IDX_CHUNK, IDX_CHUNK)
        kic = ki_ref[pl.ds(start, IDX_CHUNK), 0:IDX_DIM]
        rel = jnp.maximum(
            lax.dot_general(kic, qis, (((1,), (1,)), ((), ())), preferred_element_type=F32), 0.0)
        score = jnp.zeros((IDX_CHUNK, BLOCK), F32)
        for h in range(IDX_HEADS):
            score = score + rel[:, h * BLOCK:(h + 1) * BLOCK] * w_t[h:h + 1, :]
        sc_ref[pl.ds(start, IDX_CHUNK), :] = jnp.where(start + key <= t0 + qry, score, NEG_INF)
        return carry

    lax.fori_loop(0, n_idx, idx_body, 0)

    def count_hits(hit_fn):
        def cbody(c, cnt):
            start = pl.multiple_of(c * IDX_CHUNK, IDX_CHUNK)
            hit = hit_fn(start, sc_ref[pl.ds(start, IDX_CHUNK), :])
            parts = [hit[r * SUBLANES:(r + 1) * SUBLANES, :] for r in range(IDX_CHUNK // SUBLANES)]
            while len(parts) > 1:
                parts = [a + b for a, b in zip(parts[0::2], parts[1::2])]
            return cnt + parts[0]
        cnt = lax.fori_loop(0, n_idx, cbody, jnp.zeros((SUBLANES, BLOCK), F32))
        return jnp.sum(cnt, axis=0, keepdims=True)

    def count_ge(cand_f):
        return count_hits(lambda start, blk: jnp.where(blk >= cand_f, 1.0, 0.0))

    def bit_cond(state):
        b, _, _, n_open = state
        return (b < F32_BITS) & (n_open > 0)

    def bit_body(state):
        b0, t, done, _ = state
        for k in range(BITS_PER_CHECK):
            b = b0 + k
            bit = lax.shift_left(jnp.int32(1), F32_BITS - 1 - b)
            cand = jnp.where(b == 0, jnp.zeros_like(t), t | bit)
            cnt = count_ge(_sortable_to_f32(cand))
            take = (cnt >= float(topk)) & (done == 0)
            t = jnp.where(take, cand, t)
            done = jnp.where(take & (cnt == float(topk)), 1, done)
        n_open = jnp.sum(1 - done)
        return b0 + BITS_PER_CHECK, t, done, n_open

    state = (jnp.int32(0), jnp.full((1, BLOCK), INT_MIN, jnp.int32),
             jnp.zeros((1, BLOCK), jnp.int32), jnp.int32(BLOCK))
    _, t_int, _, n_open = lax.while_loop(bit_cond, bit_body, state)
    thr = jnp.maximum(_sortable_to_f32(t_int), jnp.float32(NEG_INF * 0.5))

    def tie_bound():
        need = float(topk) - count_hits(lambda start, blk: jnp.where(blk > thr, 1.0, 0.0))

        def jbody(b, lo):
            cand = lo + lax.shift_left(jnp.int32(1), idx_bits - 1 - b)
            kept = count_hits(lambda start, blk: jnp.where(
                blk == thr, jnp.where(start + key <= cand, 1.0, 0.0), 0.0))
            return jnp.where(kept < need, cand, lo)

        return lax.fori_loop(0, idx_bits, jbody, jnp.full((1, BLOCK), -1, jnp.int32)) + 1

    idx_bits = int(sc_ref.shape[0]).bit_length()
    last_tie = lax.cond(n_open > 0, tie_bound, lambda: jnp.full((1, BLOCK), sc_ref.shape[0], jnp.int32))

    _fill_q_ext(q_ref, qx_ref)

    def make_bias(c):
        start = pl.multiple_of(c * ATT_CHUNK, ATT_CHUNK)
        sc = sc_ref[pl.ds(start, ATT_CHUNK), :]
        tied = jnp.where(start + key <= last_tie, 0.0, NEG_INF)
        return jnp.where(sc > thr, 0.0, jnp.where(sc == thr, tied, NEG_INF)).astype(BF16)

    _flash_chunks(0, n_att, ATT_CHUNK, make_bias, k_ref, vt_ref, refs)
    _finish_flash(o_ref, acc_ref)


DSA_QI_OFF = ATTN_WIDTH
DSA_K_OFF = DSA_QI_OFF + IDX_HEADS * IDX_DIM
DSA_KI_OFF = DSA_K_OFF + KDUP_WIDTH
DSA_WI_OFF = DSA_KI_OFF + LANES
DSA_V_OFF = DSA_WI_OFF + LANES
DSA_WIDTH = DSA_V_OFF + KV_WIDTH
DSA_TN = 2048


def _dsa_attention(proj, wi, vt, batch, seq):
    nb = seq // BLOCK
    n = batch * seq
    topk = min(TOPK_MAX, seq // 4)
    qi_w = IDX_HEADS * IDX_DIM
    chunk = max(IDX_CHUNK, ATT_CHUNK)
    seq_pad = -(-seq // chunk) * chunk
    return pl.pallas_call(
        functools.partial(_dsa_kernel, topk=topk),
        out_shape=jax.ShapeDtypeStruct((n, ATTN_WIDTH), BF16),
        grid=(batch, nb),
        in_specs=[
            pl.BlockSpec((BLOCK, ATTN_WIDTH), lambda b, i: (b * nb + i, 0)),
            pl.BlockSpec((BLOCK, qi_w), lambda b, i: (b * nb + i, DSA_QI_OFF // qi_w)),
            pl.BlockSpec((BLOCK, LANES), lambda b, i: (b * nb + i, 0)),
            pl.BlockSpec((seq, LANES), lambda b, i: (b, DSA_KI_OFF // LANES)),
            pl.BlockSpec((seq, KDUP_WIDTH), lambda b, i: (b, DSA_K_OFF // KDUP_WIDTH)),
            pl.BlockSpec((None, KV_WIDTH, seq), lambda b, i: (b, 0, 0)),
        ],
        out_specs=pl.BlockSpec((BLOCK, ATTN_WIDTH), lambda b, i: (b * nb + i, 0)),
        scratch_shapes=[pltpu.VMEM((seq_pad, BLOCK), F32)] + _flash_scratch(ATT_CHUNK),
        compiler_params=_cparams(("parallel", "arbitrary")),
        name="dsa_attention",
    )(proj, proj, wi, proj, proj, vt)


def _rope_table(positions):
    inv = ROPE_THETA ** (-jnp.arange(0, ROPE_DIM, 2, dtype=F32) / ROPE_DIM)
    ang = positions.astype(F32).reshape(-1, 1) * inv[None, :]
    cos, sin = jnp.cos(ang), jnp.sin(ang)
    n = ang.shape[0]
    pad = HEAD_DIM - ROPE_DIM
    cos_h = jnp.concatenate([cos, cos, jnp.ones((n, pad), F32)], axis=1)
    lo_h = jnp.concatenate([-sin, jnp.zeros((n, HEAD_DIM - ROPE_HALF), F32)], axis=1)
    hi_h = jnp.concatenate([jnp.zeros((n, ROPE_HALF), F32), sin, jnp.zeros((n, pad), F32)], axis=1)
    reps = LANES // HEAD_DIM
    return jnp.concatenate([jnp.tile(cos_h, (1, reps)), jnp.tile(lo_h, (1, reps)),
                            jnp.tile(hi_h, (1, reps))], axis=1)


def _dup_heads(wk):
    d = wk.shape[0]
    w4 = wk.reshape(d, N_KV_HEADS, 1, HEAD_DIM)
    return jnp.broadcast_to(w4, (d, N_KV_HEADS, LANES // HEAD_DIM, HEAD_DIM)).reshape(d, KDUP_WIDTH)


QKV_WIDTH_EXT = ATTN_WIDTH + KDUP_WIDTH + KV_WIDTH


def _qkv_weight(w_in):
    o = ATTN_WIDTH
    return jnp.concatenate([w_in[:, :o], _dup_heads(w_in[:, o:o + KV_WIDTH]),
                            w_in[:, o + KV_WIDTH:o + 2 * KV_WIDTH]], axis=1)


def _qkv_colscale():
    return jnp.concatenate([jnp.full((1, ATTN_WIDTH), SCALE * LOG2E, F32),
                            jnp.ones((1, KDUP_WIDTH + KV_WIDTH), F32)], axis=1)


def _dsa_weight(w_in):
    d = w_in.shape[0]
    o = ATTN_WIDTH
    wq = w_in[:, :o]
    wk = w_in[:, o:o + KV_WIDTH]
    wv = w_in[:, o + KV_WIDTH:o + 2 * KV_WIDTH]
    o += 2 * KV_WIDTH
    wqi = w_in[:, o:o + IDX_HEADS * IDX_DIM]
    o += IDX_HEADS * IDX_DIM
    wki = w_in[:, o:o + IDX_DIM]
    o += IDX_DIM
    wwi = w_in[:, o:o + IDX_HEADS]
    z = lambda c: jnp.zeros((d, c), w_in.dtype)
    return jnp.concatenate([wq, wqi, _dup_heads(wk), wki, z(LANES - IDX_DIM),
                            wwi, z(LANES - IDX_HEADS), wv], axis=1)


def _dsa_colscale():
    return jnp.concatenate([jnp.full((1, ATTN_WIDTH), SCALE * LOG2E, F32),
                            jnp.full((1, IDX_HEADS * IDX_DIM), IDX_SCALE, F32),
                            jnp.ones((1, DSA_WIDTH - DSA_K_OFF), F32)], axis=1)


def _v_transposed(proj, v_off, batch, seq):
    v = proj[:, v_off:v_off + KV_WIDTH].reshape(batch, seq, KV_WIDTH)
    return jnp.swapaxes(v, 1, 2)


def kernel(x, positions, norm_attn, norm_mlp, w_up, w_down, final_norm,
           a_w_in, a_sinks, a_w_out, b_w_in, b_w_out, c_w_in, c_w_out):
    batch, seq, d = x.shape
    depth = norm_attn.shape[0]
    x2 = x.reshape(batch * seq, d)
    rope_tab = _rope_table(positions)
    qkv_scale = _qkv_colscale()
    qkv_rope_groups = (ATTN_WIDTH + KDUP_WIDTH) // LANES
    for i in range(depth):
        j, kind = divmod(i, 3)
        if kind == 1:
            proj, wi = _norm_proj(x2, norm_attn[i], _dsa_weight(b_w_in[j]).astype(BF16),
                                  _dsa_colscale(), rope_tab, tn=DSA_TN,
                                  n_rope_groups=DSA_WI_OFF // LANES,
                                  aux_group=(DSA_WI_OFF % DSA_TN) // LANES)
            o = _dsa_attention(proj, wi, _v_transposed(proj, DSA_V_OFF, batch, seq), batch, seq)
            w_out = b_w_out[j]
        else:
            w_in = a_w_in[j] if kind == 0 else c_w_in[j]
            q, k, v = _norm_proj(x2, norm_attn[i], _qkv_weight(w_in).astype(BF16), qkv_scale, rope_tab,
                                 tn=QKV_WIDTH_EXT, n_rope_groups=qkv_rope_groups,
                                 out_widths=(ATTN_WIDTH, KDUP_WIDTH, KV_WIDTH))
            if kind == 0:
                o = _window_attention(q, k, v, a_sinks[j], batch, seq, dil=1, span=SWA_WINDOW, emit_lse=False)
                w_out = a_w_out[j]
            else:
                o = _dilated_attention(q, k, v, batch, seq)
                w_out = c_w_out[j]
        x2 = _out_proj(o, w_out.astype(BF16), x2)
        x2 = _mlp(x2, norm_mlp[i], w_up[i].astype(BF16), w_down[i].astype(BF16),
                  final_norm if i == depth - 1 else None)
    return x2.reshape(batch, seq, d)
```

```python
import functools

import jax
import jax.numpy as jnp
from jax import lax
from jax.experimental import pallas as pl
from jax.experimental.pallas import tpu as pltpu

HEAD_DIM = 64
N_KV_HEADS = 4
GROUP = 8
N_HEADS = N_KV_HEADS * GROUP
ATTN_WIDTH = N_HEADS * HEAD_DIM
KV_WIDTH = N_KV_HEADS * HEAD_DIM
ROPE_DIM = HEAD_DIM // 4
ROPE_HALF = ROPE_DIM // 2
ROPE_THETA = 500000.0
SCALE = HEAD_DIM ** -0.5
BLOCK = 128
SWA_WINDOW = 128
IDX_HEADS = 16
IDX_DIM = 64
IDX_SCALE = IDX_DIM ** -0.5
IDX_W_SCALE = IDX_HEADS ** -0.5
TOPK_MAX = 256
DILATED_BRANCHES = ((128, 1), (512, 4), (2048, 16))
NORM_EPS = 1e-5
NEG_INF = -1e30

LANES = 128
BF16_ROWS = 16
SUBLANES = 8
MXU_TILE = 256
LOG2E = 1.4426950408889634
VMEM_LIMIT = 52 * 1024 * 1024

BF16 = jnp.bfloat16
F32 = jnp.float32

KDUP_WIDTH = N_KV_HEADS * LANES
QROWS = GROUP * BLOCK
EXT = 2 * LANES
VT_ROWS = HEAD_DIM + BF16_ROWS


def _cparams(sem):
    return pltpu.CompilerParams(dimension_semantics=sem, vmem_limit_bytes=VMEM_LIMIT)


def _norm_proj_kernel(x_ref, g_ref, w_ref, cs_ref, rope_ref, *rest,
                      n_rope_groups, groups_per_tile, out_groups, aux_group):
    o_refs = rest[:len(out_groups)]
    aux_ref = rest[len(out_groups)] if aux_group is not None else None
    h_ref = rest[-1]
    j = pl.program_id(1)

    @pl.when(j == 0)
    def _():
        x = x_ref[...]
        ms = jnp.mean(x * x, axis=-1, keepdims=True)
        h_ref[...] = ((x * lax.rsqrt(ms + NORM_EPS)) * g_ref[...]).astype(BF16)

    cos_t = rope_ref[:, 0:LANES]
    sin_lo = rope_ref[:, LANES:2 * LANES]
    sin_hi = rope_ref[:, 2 * LANES:3 * LANES]
    h = h_ref[...]
    sub_groups = MXU_TILE // LANES
    n_sub = groups_per_tile // sub_groups
    dest = [(o_ref, k) for o_ref, cnt in zip(o_refs, out_groups) for k in range(cnt)]

    def project(s):
        cols = slice(s * MXU_TILE, (s + 1) * MXU_TILE)
        return jnp.dot(h, w_ref[:, cols], preferred_element_type=F32) * cs_ref[:, cols]

    def finish(s, acc):
        for gg in range(sub_groups):
            g = s * sub_groups + gg
            a = acc[:, gg * LANES:(gg + 1) * LANES]
            r = (a * cos_t + pltpu.roll(a, LANES - ROPE_HALF, 1) * sin_lo
                 + pltpu.roll(a, ROPE_HALF, 1) * sin_hi)
            is_rope = (j * groups_per_tile + g) < n_rope_groups
            o_ref, k = dest[g]
            o_ref[:, k * LANES:(k + 1) * LANES] = jnp.where(is_rope, r, a).astype(o_ref.dtype)
            if aux_ref is not None and g == aux_group:
                aux_ref[...] = a

    acc = project(0)
    for s in range(1, n_sub):
        nxt = project(s)
        finish(s - 1, acc)
        acc = nxt
    finish(n_sub - 1, acc)


def _norm_proj(x2, g, w, colscale, rope_tab, *, tn, n_rope_groups, aux_group=None, out_widths=None, tm=512):
    n, d = x2.shape
    width = w.shape[1]
    gpt = tn // LANES
    if out_widths is None:
        out_widths = (tn,)
        out_shape = [jax.ShapeDtypeStruct((n, width), BF16)]
        out_specs = [pl.BlockSpec((tm, tn), lambda i, j: (i, j))]
    else:
        assert tn == width == sum(out_widths)
        out_shape = [jax.ShapeDtypeStruct((n, ow), BF16) for ow in out_widths]
        out_specs = [pl.BlockSpec((tm, ow), lambda i, j: (i, 0)) for ow in out_widths]
    kern = functools.partial(_norm_proj_kernel, n_rope_groups=n_rope_groups, groups_per_tile=gpt,
                             out_groups=tuple(ow // LANES for ow in out_widths), aux_group=aux_group)
    if aux_group is not None:
        out_shape.append(jax.ShapeDtypeStruct((n, LANES), F32))
        out_specs.append(pl.BlockSpec((tm, LANES), lambda i, j: (i, 0)))
    res = pl.pallas_call(
        kern,
        out_shape=out_shape,
        grid=(n // tm, width // tn),
        in_specs=[
            pl.BlockSpec((tm, d), lambda i, j: (i, 0)),
            pl.BlockSpec((1, d), lambda i, j: (0, 0)),
            pl.BlockSpec((d, tn), lambda i, j: (0, j)),
            pl.BlockSpec((1, tn), lambda i, j: (0, j)),
            pl.BlockSpec((tm, 3 * LANES), lambda i, j: (i, 0)),
        ],
        out_specs=out_specs,
        scratch_shapes=[pltpu.VMEM((tm, d), BF16)],
        compiler_params=_cparams(("parallel", "arbitrary")),
        name="norm_proj",
    )(x2, g.reshape(1, d), w, colscale, rope_tab)
    return res


def _out_proj_kernel(o_ref, w_ref, x_ref, y_ref):
    y_ref[...] = x_ref[...] + jnp.dot(o_ref[...], w_ref[...], preferred_element_type=F32)


def _out_proj(o, w, x2, *, tm=512, tn=2048):
    n, k = o.shape
    d = w.shape[1]
    return pl.pallas_call(
        _out_proj_kernel,
        out_shape=jax.ShapeDtypeStruct((n, d), F32),
        grid=(n // tm, d // tn),
        in_specs=[
            pl.BlockSpec((tm, k), lambda i, j: (i, 0)),
            pl.BlockSpec((k, tn), lambda i, j: (0, j)),
            pl.BlockSpec((tm, tn), lambda i, j: (i, j)),
        ],
        out_specs=pl.BlockSpec((tm, tn), lambda i, j: (i, j)),
        compiler_params=_cparams(("parallel", "arbitrary")),
        name="out_proj",
    )(o, w, x2)


def _rms(x, g):
    ms = jnp.mean(x * x, axis=-1, keepdims=True)
    return (x * lax.rsqrt(ms + NORM_EPS)) * g


def _mlp_kernel(x_ref, g_ref, wu_ref, wd_ref, *rest, out_norm):
    y_ref, h_ref = rest[-2:]
    f = pl.program_id(1)

    @pl.when(f == 0)
    def _():
        x = x_ref[...]
        h_ref[...] = _rms(x, g_ref[...]).astype(BF16)
        y_ref[...] = x

    u = jnp.dot(h_ref[...], wu_ref[...], preferred_element_type=F32)
    u = jnp.maximum(u, 0.0)
    a = (u * u).astype(BF16)
    y_ref[...] += jnp.dot(a, wd_ref[...], preferred_element_type=F32)

    if out_norm:
        @pl.when(f == pl.num_programs(1) - 1)
        def _():
            y_ref[...] = _rms(y_ref[...], rest[0][...])


def _mlp(x2, g, w_up, w_down, layer, out_gain=None, *, tm=512, tf=1024):
    n, d = x2.shape
    d_ff = w_up.shape[2]
    vec = pl.BlockSpec((1, d), lambda i, f: (0, 0))
    in_specs = [
        pl.BlockSpec((tm, d), lambda i, f: (i, 0)),
        vec,
        pl.BlockSpec((None, d, tf), lambda i, f: (layer, 0, f)),
        pl.BlockSpec((None, tf, d), lambda i, f: (layer, f, 0)),
    ]
    args = [x2, g.reshape(1, d), w_up, w_down]
    if out_gain is not None:
        in_specs.append(vec)
        args.append(out_gain.reshape(1, d))
    return pl.pallas_call(
        functools.partial(_mlp_kernel, out_norm=out_gain is not None),
        out_shape=jax.ShapeDtypeStruct((n, d), F32),
        grid=(n // tm, d_ff // tf),
        in_specs=in_specs,
        out_specs=pl.BlockSpec((tm, d), lambda i, f: (i, 0)),
        scratch_shapes=[pltpu.VMEM((tm, d), BF16)],
        compiler_params=_cparams(("parallel", "arbitrary")),
        name="mlp",
    )(*args)


def _fill_q_ext(q_ref, qx_ref, row0=0, slot=()):
    lane = lax.broadcasted_iota(jnp.int32, (BLOCK, LANES), 1)
    row = lax.broadcasted_iota(jnp.int32, (BLOCK, LANES), 0)
    eye = jnp.where(lane == row, 1.0, 0.0).astype(BF16)
    low = lane < HEAD_DIM
    for kv in range(N_KV_HEADS):
        for g in range(GROUP):
            h = kv * GROUP + g
            tile = q_ref[row0:row0 + BLOCK, (h // 2) * LANES:(h // 2 + 1) * LANES]
            keep = low if h % 2 == 0 else jnp.logical_not(low)
            rows = slice(g * BLOCK, (g + 1) * BLOCK)
            qx_ref[(*slot, kv, rows, slice(0, LANES))] = jnp.where(keep, tile, jnp.zeros_like(tile))
            qx_ref[(*slot, kv, rows, slice(LANES, EXT))] = eye


def _scores_t(k_tile, bias, qx):
    k_ext = jnp.concatenate([k_tile, bias], axis=1)
    return lax.dot_general(k_ext, qx, (((1,), (1,)), ((), ())), preferred_element_type=F32)


def _vt_ext(vt):
    return jnp.concatenate([vt, jnp.ones((BF16_ROWS, vt.shape[1]), BF16)], axis=0)


def _flash_chunks(c_lo, c_end, tc, make_bias, k_ref, vt_ref, refs):
    qx_ref, s_ref, cm_ref, m_ref, acc_ref, bias_ref = refs
    m_ref[...] = jnp.full(m_ref.shape, NEG_INF, F32)
    acc_ref[...] = jnp.zeros(acc_ref.shape, F32)
    c_last = c_end - 1

    def put_bias(c):
        bias_ref[c & 1] = make_bias(c)

    def issue_scores(c, kv):
        start = pl.multiple_of(c * tc, tc)
        k_tile = k_ref[pl.ds(start, tc), kv * LANES:(kv + 1) * LANES]
        s_t = _scores_t(k_tile, bias_ref[c & 1], qx_ref[kv])
        s_ref[kv % 2] = s_t
        cm_ref[kv % 2] = jnp.max(s_t, axis=0, keepdims=True)

    def consume(c, kv):
        slot = kv % 2
        start = pl.multiple_of(c * tc, tc)
        vt = _vt_ext(vt_ref[kv * HEAD_DIM:(kv + 1) * HEAD_DIM, pl.ds(start, tc)])
        m_old = m_ref[kv]
        m_new = jnp.maximum(m_old, cm_ref[slot])
        alpha = jnp.exp2(m_old - m_new)
        m_ref[kv] = m_new
        for n in range(QROWS // MXU_TILE):
            cols = slice(n * MXU_TILE, (n + 1) * MXU_TILE)
            part = alpha[:, cols] * acc_ref[kv, :, cols]
            for kk in range(tc // MXU_TILE):
                rows = slice(kk * MXU_TILE, (kk + 1) * MXU_TILE)
                p = jnp.exp2(s_ref[slot, rows, cols] - m_new[:, cols])
                part = part + jnp.dot(vt[:, rows], p.astype(BF16), preferred_element_type=F32)
            acc_ref[kv, :, cols] = part

    put_bias(c_lo)
    issue_scores(c_lo, 0)

    def body(c, carry):
        c_next = jnp.minimum(c + 1, c_last)
        put_bias(c_next)
        for kv in range(N_KV_HEADS):
            if kv + 1 < N_KV_HEADS:
                issue_scores(c, kv + 1)
            else:
                issue_scores(c_next, 0)
            consume(c, kv)
        return carry

    lax.fori_loop(c_lo, c_end, body, 0)


def _store_out(o_ref, kv, o_t, row0=0):
    for gp in range(GROUP // 2):
        pair = jnp.concatenate([o_t[:, (2 * gp) * BLOCK:(2 * gp + 1) * BLOCK],
                                o_t[:, (2 * gp + 1) * BLOCK:(2 * gp + 2) * BLOCK]], axis=0)
        col = (kv * GROUP + 2 * gp) * HEAD_DIM
        o_ref[row0:row0 + BLOCK, col:col + LANES] = pair.T.astype(o_ref.dtype)


def _finish_flash(o_ref, acc_ref):
    for kv in range(N_KV_HEADS):
        acc = acc_ref[kv]
        _store_out(o_ref, kv, acc[0:HEAD_DIM, :] / acc[HEAD_DIM:HEAD_DIM + 1, :])


def _bias_of(valid):
    return jnp.where(valid, 0.0, NEG_INF).astype(BF16)


def _qx_scratch():
    return pltpu.VMEM((N_KV_HEADS, QROWS, EXT), BF16)


def _flash_scratch(tc):
    return [
        _qx_scratch(),
        pltpu.VMEM((2, tc, QROWS), F32),
        pltpu.VMEM((2, 1, QROWS), F32),
        pltpu.VMEM((N_KV_HEADS, 1, QROWS), F32),
        pltpu.VMEM((N_KV_HEADS, VT_ROWS, QROWS), F32),
        pltpu.VMEM((2, tc, BLOCK), BF16),
    ]


WIN_TILES = 4


def _window_kernel(*refs, span, tiles, with_sinks, emit_lse):
    refs = list(refs)
    sink_ref = refs.pop(0) if with_sinks else None
    q_ref, kp_ref, kc_ref, vp_ref, vc_ref, o_ref = refs[:6]
    lse_ref = refs[6] if emit_lse else None
    qx_ref, sw_ref = refs[-2:]
    first = pl.program_id(2) == 0
    key = lax.broadcasted_iota(jnp.int32, (2 * BLOCK, BLOCK), 0)
    qry = lax.broadcasted_iota(jnp.int32, (2 * BLOCK, BLOCK), 1)
    dist = qry + BLOCK - key
    band = (dist >= 0) & (dist < span)
    bias_any = _bias_of(band)
    bias_first = _bias_of(band & ((key >= BLOCK) | jnp.logical_not(first)))

    def v_t(block):
        return block.astype(F32).T.astype(BF16)

    ones = jnp.ones((BF16_ROWS, 2 * BLOCK), BF16)
    vts = [v_t(vp_ref[...])] + [v_t(vc_ref[j * BLOCK:(j + 1) * BLOCK, :]) for j in range(tiles)]

    def issue_scores(j):
        _fill_q_ext(q_ref, qx_ref, j * BLOCK, (j,))
        for kv in range(N_KV_HEADS):
            lanes = slice(kv * LANES, (kv + 1) * LANES)
            k_prev = kp_ref[:, lanes] if j == 0 else kc_ref[(j - 1) * BLOCK:j * BLOCK, lanes]
            k_tile = jnp.concatenate([k_prev, kc_ref[j * BLOCK:(j + 1) * BLOCK, lanes]], axis=0)
            sw_ref[j, kv] = _scores_t(k_tile, bias_first if j == 0 else bias_any, qx_ref[j, kv])

    def consume(j):
        vt2 = jnp.concatenate([vts[j], vts[j + 1]], axis=1)
        lse_rows = []
        for kv in range(N_KV_HEADS):
            s_t = sw_ref[j, kv]
            m = jnp.max(s_t, axis=0, keepdims=True)
            if with_sinks:
                sink = jnp.concatenate(
                    [jnp.full((1, BLOCK), sink_ref[kv * GROUP + g] * LOG2E, F32) for g in range(GROUP)], axis=1)
                m = jnp.maximum(m, sink)
            p = jnp.exp2(s_t - m).astype(BF16)
            vt = jnp.concatenate([vt2[kv * HEAD_DIM:(kv + 1) * HEAD_DIM, :], ones], axis=0)
            acc = jnp.dot(vt, p, preferred_element_type=F32)
            den = acc[HEAD_DIM:HEAD_DIM + 1, :]
            if with_sinks:
                den = den + jnp.exp2(sink - m)
            _store_out(o_ref, kv, acc[0:HEAD_DIM, :] / den, j * BLOCK)
            if emit_lse:
                lse = m + jnp.log2(den)
                lse_rows += [lse[:, g * BLOCK:(g + 1) * BLOCK] for g in range(GROUP)]
        if emit_lse:
            pad = jnp.zeros((BLOCK - N_HEADS, BLOCK), F32)
            lse_ref[j * BLOCK:(j + 1) * BLOCK, :] = jnp.concatenate(lse_rows + [pad], axis=0).T

    issue_scores(0)
    for j in range(tiles):
        if j + 1 < tiles:
            issue_scores(j + 1)
        consume(j)


RELAYOUT_ROWS = 512


def _to_residue_kernel(x_ref, o_ref, stage_ref, *, dil):
    groups = x_ref.shape[1] // LANES
    for g in range(groups):
        stage_ref[g] = x_ref[:, g * LANES:(g + 1) * LANES].astype(F32)
    for r in range(dil):
        for g in range(groups):
            rows = stage_ref[g, pl.ds(r, RELAYOUT_ROWS // dil, stride=dil), :]
            o_ref[r, :, g * LANES:(g + 1) * LANES] = rows.astype(o_ref.dtype)


def _to_token_kernel(x_ref, o_ref, stage_ref, *, dil):
    groups = o_ref.shape[1] // LANES
    for r in range(dil):
        for g in range(groups):
            stage_ref[g, pl.ds(r, RELAYOUT_ROWS // dil, stride=dil), :] = (
                x_ref[r, :, g * LANES:(g + 1) * LANES].astype(F32))
    for g in range(groups):
        o_ref[:, g * LANES:(g + 1) * LANES] = stage_ref[g].astype(o_ref.dtype)


def _relayout_specs(batch, seq, dil, w):
    steps = seq // RELAYOUT_ROWS
    token = pl.BlockSpec((RELAYOUT_ROWS, w), lambda b, u: (b * steps + u, 0))
    residue = pl.BlockSpec((None, dil, RELAYOUT_ROWS // dil, w), lambda b, u: (b, 0, u, 0))
    return (batch, steps), token, residue


def _by_residue(a, batch, seq, dil):
    w = a.shape[1]
    if dil == 1:
        return a.reshape(batch, 1, seq, w)
    grid, token, residue = _relayout_specs(batch, seq, dil, w)
    return pl.pallas_call(
        functools.partial(_to_residue_kernel, dil=dil),
        out_shape=jax.ShapeDtypeStruct((batch, dil, seq // dil, w), a.dtype),
        grid=grid, in_specs=[token], out_specs=residue,
        scratch_shapes=[pltpu.VMEM((w // LANES, RELAYOUT_ROWS, LANES), F32)],
        compiler_params=_cparams(("parallel", "parallel")),
        name="to_residue",
    )(a)


def _by_token(a, batch, seq, dil):
    w = a.shape[-1]
    if dil == 1:
        return a.reshape(batch * seq, w)
    grid, token, residue = _relayout_specs(batch, seq, dil, w)
    return pl.pallas_call(
        functools.partial(_to_token_kernel, dil=dil),
        out_shape=jax.ShapeDtypeStruct((batch * seq, w), a.dtype),
        grid=grid, in_specs=[residue], out_specs=token,
        scratch_shapes=[pltpu.VMEM((w // LANES, RELAYOUT_ROWS, LANES), F32)],
        compiler_params=_cparams(("parallel", "parallel")),
        name="to_token",
    )(a)


def _window_attention(q, k, v, sinks, batch, seq, *, dil, span, emit_lse):
    sub = seq // dil
    tiles = max(1, min(WIN_TILES, sub // BLOCK // 2))
    rows = tiles * BLOCK
    with_sinks = sinks is not None
    cur = lambda b, r, u: (b, r, u, 0)
    prev = lambda b, r, u: (b, r, jnp.maximum(u * tiles - 1, 0), 0)
    in_specs = [
        pl.BlockSpec((None, None, rows, ATTN_WIDTH), cur),
        pl.BlockSpec((None, None, BLOCK, KDUP_WIDTH), prev),
        pl.BlockSpec((None, None, rows, KDUP_WIDTH), cur),
        pl.BlockSpec((None, None, BLOCK, KV_WIDTH), prev),
        pl.BlockSpec((None, None, rows, KV_WIDTH), cur),
    ]
    qr, kr, vr = (_by_residue(a, batch, seq, dil) for a in (q, k, v))
    args = [qr, kr, kr, vr, vr]
    if with_sinks:
        in_specs.insert(0, pl.BlockSpec(memory_space=pltpu.SMEM))
        args.insert(0, sinks)
    out_shape = [jax.ShapeDtypeStruct((batch, dil, sub, ATTN_WIDTH), BF16)]
    out_specs = [pl.BlockSpec((None, None, rows, ATTN_WIDTH), cur)]
    if emit_lse:
        out_shape.append(jax.ShapeDtypeStruct((batch, dil, sub, LANES), F32))
        out_specs.append(pl.BlockSpec((None, None, rows, LANES), cur))
    res = pl.pallas_call(
        functools.partial(_window_kernel, span=span, tiles=tiles, with_sinks=with_sinks, emit_lse=emit_lse),
        out_shape=out_shape,
        grid=(batch, dil, sub // rows),
        in_specs=in_specs,
        out_specs=out_specs,
        scratch_shapes=[pltpu.VMEM((tiles, N_KV_HEADS, QROWS, EXT), BF16),
                        pltpu.VMEM((tiles, N_KV_HEADS, 2 * BLOCK, QROWS), F32)],
        compiler_params=_cparams(("parallel", "parallel", "arbitrary")),
        name="window_attention",
    )(*args)
    o = _by_token(res[0], batch, seq, dil)
    return (o, _by_token(res[1], batch, seq, dil)) if emit_lse else o


def _merge_kernel(*refs):
    n_br = (len(refs) - 2) // 2
    o_refs, l_refs, e_ref, out_ref = refs[:n_br], refs[n_br:2 * n_br], refs[-2], refs[-1]
    lses = [l[...] for l in l_refs]
    top = functools.reduce(jnp.maximum, lses)
    ws = [jnp.exp2(l - top) for l in lses]
    tot = functools.reduce(lambda a, b: a + b, ws)
    out = None
    for w, o_ref in zip(ws, o_refs):
        wn = w / tot
        hi = wn.astype(BF16)
        lo = (wn - hi.astype(F32)).astype(BF16)
        spread = (jnp.dot(hi, e_ref[...], preferred_element_type=F32)
                  + jnp.dot(lo, e_ref[...], preferred_element_type=F32))
        term = spread * o_ref[...].astype(F32)
        out = term if out is None else out + term
    out_ref[...] = out.astype(out_ref.dtype)


def _merge_branches(outs, lses, *, tm=512):
    n = outs[0].shape[0]
    head_of_lane = jnp.arange(ATTN_WIDTH, dtype=jnp.int32) // HEAD_DIM
    expand = (jnp.arange(LANES, dtype=jnp.int32)[:, None] == head_of_lane[None, :]).astype(BF16)
    row = lambda width: pl.BlockSpec((tm, width), lambda i: (i, 0))
    return pl.pallas_call(
        _merge_kernel,
        out_shape=jax.ShapeDtypeStruct((n, ATTN_WIDTH), BF16),
        grid=(n // tm,),
        in_specs=[row(ATTN_WIDTH)] * len(outs) + [row(LANES)] * len(lses)
                 + [pl.BlockSpec((LANES, ATTN_WIDTH), lambda i: (0, 0))],
        out_specs=row(ATTN_WIDTH),
        compiler_params=_cparams(("parallel",)),
        name="merge_branches",
    )(*outs, *lses, expand)


def _dilated_attention(q, k, v, batch, seq):
    outs, lses = [], []
    for window, dil in DILATED_BRANCHES:
        o, lse = _window_attention(q, k, v, None, batch, seq, dil=dil, span=window // dil + 1, emit_lse=True)
        outs.append(o)
        lses.append(lse)
    return _merge_branches(outs, lses)


IDX_CHUNK = 512
ATT_CHUNK = 512
INT_MIN = -2 ** 31
F32_BITS = 32
BITS_PER_CHECK = 4


def _sortable_to_f32(t):
    bits = jnp.where(t >= 0, t, t ^ jnp.int32(0x7FFFFFFF))
    return lax.bitcast_convert_type(bits, F32)


def _dsa_kernel(q_ref, qi_ref, wi_ref, ki_ref, k_ref, vt_ref, o_ref, sc_ref, *refs, topk):
    qx_ref, acc_ref = refs[0], refs[4]
    i = pl.program_id(1)
    t0 = i * BLOCK
    n_idx = (t0 + BLOCK + IDX_CHUNK - 1) // IDX_CHUNK
    n_att = (t0 + BLOCK + ATT_CHUNK - 1) // ATT_CHUNK

    qis = jnp.concatenate(
        [qi_ref[:, h * IDX_DIM:(h + 1) * IDX_DIM] for h in range(IDX_HEADS)], axis=0)
    w_t = (wi_ref[...] * IDX_W_SCALE).T
    key = lax.broadcasted_iota(jnp.int32, (IDX_CHUNK, BLOCK), 0)
    qry = lax.broadcasted_iota(jnp.int32, (IDX_CHUNK, BLOCK), 1)

    def idx_body(c, carry):
        start = pl.multiple_of(c * IDX_CHUNK, IDX_CHUNK)
        kic = ki_ref[pl.ds(start, IDX_CHUNK), 0:IDX_DIM]
        rel = jnp.maximum(
            lax.dot_general(kic, qis, (((1,), (1,)), ((), ())), preferred_element_type=F32), 0.0)
        score = jnp.zeros((IDX_CHUNK, BLOCK), F32)
        for h in range(IDX_HEADS):
            score = score + rel[:, h * BLOCK:(h + 1) * BLOCK] * w_t[h:h + 1, :]
        sc_ref[pl.ds(start, IDX_CHUNK), :] = jnp.where(start + key <= t0 + qry, score, NEG_INF)
        return carry

    lax.fori_loop(0, n_idx, idx_body, 0)

    def count_hits(hit_fn):
        def cbody(c, cnt):
            start = pl.multiple_of(c * IDX_CHUNK, IDX_CHUNK)
            hit = hit_fn(start, sc_ref[pl.ds(start, IDX_CHUNK), :])
            parts = [hit[r * SUBLANES:(r + 1) * SUBLANES, :] for r in range(IDX_CHUNK // SUBLANES)]
            while len(parts) > 1:
                parts = [a + b for a, b in zip(parts[0::2], parts[1::2])]
            return cnt + parts[0]
        cnt = lax.fori_loop(0, n_idx, cbody, jnp.zeros((SUBLANES, BLOCK), F32))
        return jnp.sum(cnt, axis=0, keepdims=True)

    def count_ge(cand_f):
        return count_hits(lambda start, blk: jnp.where(blk >= cand_f, 1.0, 0.0))

    def bit_cond(state):
        b, _, _, n_open = state
        return (b < F32_BITS) & (n_open > 0)

    def bit_body(state):
        b0, t, done, _ = state
        for k in range(BITS_PER_CHECK):
            b = b0 + k
            bit = lax.shift_left(jnp.int32(1), F32_BITS - 1 - b)
            cand = jnp.where(b == 0, jnp.zeros_like(t), t | bit)
            cnt = count_ge(_sortable_to_f32(cand))
            take = (cnt >= float(topk)) & (done == 0)
            t = jnp.where(take, cand, t)
            done = jnp.where(take & (cnt == float(topk)), 1, done)
        n_open = jnp.sum(1 - done)
        return b0 + BITS_PER_CHECK, t, done, n_open

    state = (jnp.int32(0), jnp.full((1, BLOCK), INT_MIN, jnp.int32),
             jnp.zeros((1, BLOCK), jnp.int32), jnp.int32(BLOCK))
    _, t_int, _, n_open = lax.while_loop(bit_cond, bit_body, state)
    thr = jnp.maximum(_sortable_to_f32(t_int), jnp.float32(NEG_INF * 0.5))

    def tie_bound():
        need = float(topk) - count_hits(lambda start, blk: jnp.where(blk > thr, 1.0, 0.0))

        def jbody(b, lo):
            cand = lo + lax.shift_left(jnp.int32(1), idx_bits - 1 - b)
            kept = count_hits(lambda start, blk: jnp.where(
                blk == thr, jnp.where(start + key <= cand, 1.0, 0.0), 0.0))
            return jnp.where(kept < need, cand, lo)

        return lax.fori_loop(0, idx_bits, jbody, jnp.full((1, BLOCK), -1, jnp.int32)) + 1

    idx_bits = int(sc_ref.shape[0]).bit_length()
    last_tie = lax.cond(n_open > 0, tie_bound, lambda: jnp.full((1, BLOCK), sc_ref.shape[0], jnp.int32))

    _fill_q_ext(q_ref, qx_ref)

    def make_bias(c):
        start = pl.multiple_of(c * ATT_CHUNK, ATT_CHUNK)
        sc = sc_ref[pl.ds(start, ATT_CHUNK), :]
        tied = jnp.where(start + key <= last_tie, 0.0, NEG_INF)
        return jnp.where(sc > thr, 0.0, jnp.where(sc == thr, tied, NEG_INF)).astype(BF16)

    _flash_chunks(0, n_att, ATT_CHUNK, make_bias, k_ref, vt_ref, refs)
    _finish_flash(o_ref, acc_ref)


DSA_QI_OFF = ATTN_WIDTH
DSA_K_OFF = DSA_QI_OFF + IDX_HEADS * IDX_DIM
DSA_KI_OFF = DSA_K_OFF + KDUP_WIDTH
DSA_WI_OFF = DSA_KI_OFF + LANES
DSA_V_OFF = DSA_WI_OFF + LANES
DSA_WIDTH = DSA_V_OFF + KV_WIDTH
DSA_TN = 2048


def _dsa_attention(proj, wi, vt, batch, seq):
    nb = seq // BLOCK
    n = batch * seq
    topk = min(TOPK_MAX, seq // 4)
    qi_w = IDX_HEADS * IDX_DIM
    chunk = max(IDX_CHUNK, ATT_CHUNK)
    seq_pad = -(-seq // chunk) * chunk
    return pl.pallas_call(
        functools.partial(_dsa_kernel, topk=topk),
        out_shape=jax.ShapeDtypeStruct((n, ATTN_WIDTH), BF16),
        grid=(batch, nb),
        in_specs=[
            pl.BlockSpec((BLOCK, ATTN_WIDTH), lambda b, i: (b * nb + i, 0)),
            pl.BlockSpec((BLOCK, qi_w), lambda b, i: (b * nb + i, DSA_QI_OFF // qi_w)),
            pl.BlockSpec((BLOCK, LANES), lambda b, i: (b * nb + i, 0)),
            pl.BlockSpec((seq, LANES), lambda b, i: (b, DSA_KI_OFF // LANES)),
            pl.BlockSpec((seq, KDUP_WIDTH), lambda b, i: (b, DSA_K_OFF // KDUP_WIDTH)),
            pl.BlockSpec((None, KV_WIDTH, seq), lambda b, i: (b, 0, 0)),
        ],
        out_specs=pl.BlockSpec((BLOCK, ATTN_WIDTH), lambda b, i: (b * nb + i, 0)),
        scratch_shapes=[pltpu.VMEM((seq_pad, BLOCK), F32)] + _flash_scratch(ATT_CHUNK),
        compiler_params=_cparams(("parallel", "arbitrary")),
        name="dsa_attention",
    )(proj, proj, wi, proj, proj, vt)


def _rope_table(positions):
    inv = ROPE_THETA ** (-jnp.arange(0, ROPE_DIM, 2, dtype=F32) / ROPE_DIM)
    ang = positions.astype(F32).reshape(-1, 1) * inv[None, :]
    cos, sin = jnp.cos(ang), jnp.sin(ang)
    n = ang.shape[0]
    pad = HEAD_DIM - ROPE_DIM
    cos_h = jnp.concatenate([cos, cos, jnp.ones((n, pad), F32)], axis=1)
    lo_h = jnp.concatenate([-sin, jnp.zeros((n, HEAD_DIM - ROPE_HALF), F32)], axis=1)
    hi_h = jnp.concatenate([jnp.zeros((n, ROPE_HALF), F32), sin, jnp.zeros((n, pad), F32)], axis=1)
    reps = LANES // HEAD_DIM
    return jnp.concatenate([jnp.tile(cos_h, (1, reps)), jnp.tile(lo_h, (1, reps)),
                            jnp.tile(hi_h, (1, reps))], axis=1)


def _dup_heads(wk):
    d = wk.shape[0]
    w4 = wk.reshape(d, N_KV_HEADS, 1, HEAD_DIM)
    return jnp.broadcast_to(w4, (d, N_KV_HEADS, LANES // HEAD_DIM, HEAD_DIM)).reshape(d, KDUP_WIDTH)


QKV_WIDTH_EXT = ATTN_WIDTH + KDUP_WIDTH + KV_WIDTH


def _qkv_weight(w_in):
    o = ATTN_WIDTH
    return jnp.concatenate([w_in[:, :o], _dup_heads(w_in[:, o:o + KV_WIDTH]),
                            w_in[:, o + KV_WIDTH:o + 2 * KV_WIDTH]], axis=1)


def _qkv_colscale():
    return jnp.concatenate([jnp.full((1, ATTN_WIDTH), SCALE * LOG2E, F32),
                            jnp.ones((1, KDUP_WIDTH + KV_WIDTH), F32)], axis=1)


def _dsa_weight(w_in):
    d = w_in.shape[0]
    o = ATTN_WIDTH
    wq = w_in[:, :o]
    wk = w_in[:, o:o + KV_WIDTH]
    wv = w_in[:, o + KV_WIDTH:o + 2 * KV_WIDTH]
    o += 2 * KV_WIDTH
    wqi = w_in[:, o:o + IDX_HEADS * IDX_DIM]
    o += IDX_HEADS * IDX_DIM
    wki = w_in[:, o:o + IDX_DIM]
    o += IDX_DIM
    wwi = w_in[:, o:o + IDX_HEADS]
    z = lambda c: jnp.zeros((d, c), w_in.dtype)
    return jnp.concatenate([wq, wqi, _dup_heads(wk), wki, z(LANES - IDX_DIM),
                            wwi, z(LANES - IDX_HEADS), wv], axis=1)


def _dsa_colscale():
    return jnp.concatenate([jnp.full((1, ATTN_WIDTH), SCALE * LOG2E, F32),
                            jnp.full((1, IDX_HEADS * IDX_DIM), IDX_SCALE, F32),
                            jnp.ones((1, DSA_WIDTH - DSA_K_OFF), F32)], axis=1)


def _v_transposed(proj, v_off, batch, seq):
    v = proj[:, v_off:v_off + KV_WIDTH].reshape(batch, seq, KV_WIDTH)
    return jnp.swapaxes(v, 1, 2)


def kernel(x, positions, norm_attn, norm_mlp, w_up, w_down, final_norm,
           a_w_in, a_sinks, a_w_out, b_w_in, b_w_out, c_w_in, c_w_out):
    batch, seq, d = x.shape
    depth = norm_attn.shape[0]
    x2 = x.reshape(batch * seq, d)
    rope_tab = _rope_table(positions)
    w_up_bf, w_down_bf = w_up.astype(BF16), w_down.astype(BF16)
    qkv_scale = _qkv_colscale()
    qkv_rope_groups = (ATTN_WIDTH + KDUP_WIDTH) // LANES
    for i in range(depth):
        j, kind = divmod(i, 3)
        if kind == 1:
            proj, wi = _norm_proj(x2, norm_attn[i], _dsa_weight(b_w_in[j]).astype(BF16),
                                  _dsa_colscale(), rope_tab, tn=DSA_TN,
                                  n_rope_groups=DSA_WI_OFF // LANES,
                                  aux_group=(DSA_WI_OFF % DSA_TN) // LANES)
            o = _dsa_attention(proj, wi, _v_transposed(proj, DSA_V_OFF, batch, seq), batch, seq)
            w_out = b_w_out[j]
        else:
            w_in = a_w_in[j] if kind == 0 else c_w_in[j]
            q, k, v = _norm_proj(x2, norm_attn[i], _qkv_weight(w_in).astype(BF16), qkv_scale, rope_tab,
                                 tn=QKV_WIDTH_EXT, n_rope_groups=qkv_rope_groups,
                                 out_widths=(ATTN_WIDTH, KDUP_WIDTH, KV_WIDTH))
            if kind == 0:
                o = _window_attention(q, k, v, a_sinks[j], batch, seq, dil=1, span=SWA_WINDOW, emit_lse=False)
                w_out = a_w_out[j]
            else:
                o = _dilated_attention(q, k, v, batch, seq)
                w_out = c_w_out[j]
        x2 = _out_proj(o, w_out.astype(BF16), x2)
        x2 = _mlp(x2, norm_mlp[i], w_up_bf, w_down_bf, i, final_norm if i == depth - 1 else None)
    return x2.reshape(batch, seq, d)
```

```python
import functools

import jax
import jax.numpy as jnp
from jax import lax
from jax.experimental import pallas as pl
from jax.experimental.pallas import tpu as pltpu

HEAD_DIM = 64
N_KV_HEADS = 4
GROUP = 8
N_HEADS = N_KV_HEADS * GROUP
ATTN_WIDTH = N_HEADS * HEAD_DIM
KV_WIDTH = N_KV_HEADS * HEAD_DIM
ROPE_DIM = HEAD_DIM // 4
ROPE_HALF = ROPE_DIM // 2
ROPE_THETA = 500000.0
SCALE = HEAD_DIM ** -0.5
BLOCK = 128
SWA_WINDOW = 128
IDX_HEADS = 16
IDX_DIM = 64
IDX_SCALE = IDX_DIM ** -0.5
IDX_W_SCALE = IDX_HEADS ** -0.5
TOPK_MAX = 256
DILATED_BRANCHES = ((128, 1), (512, 4), (2048, 16))
NORM_EPS = 1e-5
NEG_INF = -1e30

LANES = 128
BF16_ROWS = 16
SUBLANES = 8
MXU_TILE = 256
LOG2E = 1.4426950408889634
VMEM_LIMIT = 52 * 1024 * 1024

BF16 = jnp.bfloat16
F32 = jnp.float32

KDUP_WIDTH = N_KV_HEADS * LANES
QROWS = GROUP * BLOCK
EXT = 2 * LANES
VT_ROWS = HEAD_DIM + BF16_ROWS


def _cparams(sem):
    return pltpu.CompilerParams(dimension_semantics=sem, vmem_limit_bytes=VMEM_LIMIT)


def _norm_proj_kernel(x_ref, g_ref, w_ref, cs_ref, rope_ref, *rest,
                      n_rope_groups, groups_per_tile, out_groups, aux_group):
    o_refs = rest[:len(out_groups)]
    aux_ref = rest[len(out_groups)] if aux_group is not None else None
    h_ref = rest[-1]
    j = pl.program_id(1)

    @pl.when(j == 0)
    def _():
        x = x_ref[...]
        ms = jnp.mean(x * x, axis=-1, keepdims=True)
        h_ref[...] = ((x * lax.rsqrt(ms + NORM_EPS)) * g_ref[...]).astype(BF16)

    cos_t = rope_ref[:, 0:LANES]
    sin_lo = rope_ref[:, LANES:2 * LANES]
    sin_hi = rope_ref[:, 2 * LANES:3 * LANES]
    h = h_ref[...]
    sub_groups = MXU_TILE // LANES
    n_sub = groups_per_tile // sub_groups
    dest = [(o_ref, k) for o_ref, cnt in zip(o_refs, out_groups) for k in range(cnt)]

    def project(s):
        cols = slice(s * MXU_TILE, (s + 1) * MXU_TILE)
        return jnp.dot(h, w_ref[:, cols], preferred_element_type=F32) * cs_ref[:, cols]

    def finish(s, acc):
        for gg in range(sub_groups):
            g = s * sub_groups + gg
            a = acc[:, gg * LANES:(gg + 1) * LANES]
            r = (a * cos_t + pltpu.roll(a, LANES - ROPE_HALF, 1) * sin_lo
                 + pltpu.roll(a, ROPE_HALF, 1) * sin_hi)
            is_rope = (j * groups_per_tile + g) < n_rope_groups
            o_ref, k = dest[g]
            o_ref[:, k * LANES:(k + 1) * LANES] = jnp.where(is_rope, r, a).astype(o_ref.dtype)
            if aux_ref is not None and g == aux_group:
                aux_ref[...] = a

    acc = project(0)
    for s in range(1, n_sub):
        nxt = project(s)
        finish(s - 1, acc)
        acc = nxt
    finish(n_sub - 1, acc)


def _norm_proj(x2, g, w, colscale, rope_tab, *, tn, n_rope_groups, aux_group=None, out_widths=None, tm=512):
    n, d = x2.shape
    width = w.shape[1]
    gpt = tn // LANES
    if out_widths is None:
        out_widths = (tn,)
        out_shape = [jax.ShapeDtypeStruct((n, width), BF16)]
        out_specs = [pl.BlockSpec((tm, tn), lambda i, j: (i, j))]
    else:
        assert tn == width == sum(out_widths)
        out_shape = [jax.ShapeDtypeStruct((n, ow), BF16) for ow in out_widths]
        out_specs = [pl.BlockSpec((tm, ow), lambda i, j: (i, 0)) for ow in out_widths]
    kern = functools.partial(_norm_proj_kernel, n_rope_groups=n_rope_groups, groups_per_tile=gpt,
                             out_groups=tuple(ow // LANES for ow in out_widths), aux_group=aux_group)
    if aux_group is not None:
        out_shape.append(jax.ShapeDtypeStruct((n, LANES), F32))
        out_specs.append(pl.BlockSpec((tm, LANES), lambda i, j: (i, 0)))
    res = pl.pallas_call(
        kern,
        out_shape=out_shape,
        grid=(n // tm, width // tn),
        in_specs=[
            pl.BlockSpec((tm, d), lambda i, j: (i, 0)),
            pl.BlockSpec((1, d), lambda i, j: (0, 0)),
            pl.BlockSpec((d, tn), lambda i, j: (0, j)),
            pl.BlockSpec((1, tn), lambda i, j: (0, j)),
            pl.BlockSpec((tm, 3 * LANES), lambda i, j: (i, 0)),
        ],
        out_specs=out_specs,
        scratch_shapes=[pltpu.VMEM((tm, d), BF16)],
        compiler_params=_cparams(("parallel", "arbitrary")),
        name="norm_proj",
    )(x2, g.reshape(1, d), w, colscale, rope_tab)
    return res


def _out_proj_kernel(o_ref, w_ref, x_ref, y_ref):
    y_ref[...] = x_ref[...] + jnp.dot(o_ref[...], w_ref[...], preferred_element_type=F32)


def _out_proj(o, w, x2, *, tm=512, tn=2048):
    n, k = o.shape
    d = w.shape[1]
    return pl.pallas_call(
        _out_proj_kernel,
        out_shape=jax.ShapeDtypeStruct((n, d), F32),
        grid=(n // tm, d // tn),
        in_specs=[
            pl.BlockSpec((tm, k), lambda i, j: (i, 0)),
            pl.BlockSpec((k, tn), lambda i, j: (0, j)),
            pl.BlockSpec((tm, tn), lambda i, j: (i, j)),
        ],
        out_specs=pl.BlockSpec((tm, tn), lambda i, j: (i, j)),
        compiler_params=_cparams(("parallel", "arbitrary")),
        name="out_proj",
    )(o, w, x2)


def _rms(x, g):
    ms = jnp.mean(x * x, axis=-1, keepdims=True)
    return (x * lax.rsqrt(ms + NORM_EPS)) * g


def _mlp_kernel(x_ref, g_ref, wu_ref, wd_ref, *rest, out_norm):
    y_ref, h_ref = rest[-2:]
    f = pl.program_id(1)

    @pl.when(f == 0)
    def _():
        x = x_ref[...]
        h_ref[...] = _rms(x, g_ref[...]).astype(BF16)
        y_ref[...] = x

    u = jnp.dot(h_ref[...], wu_ref[...], preferred_element_type=F32)
    u = jnp.maximum(u, 0.0)
    a = (u * u).astype(BF16)
    y_ref[...] += jnp.dot(a, wd_ref[...], preferred_element_type=F32)

    if out_norm:
        @pl.when(f == pl.num_programs(1) - 1)
        def _():
            y_ref[...] = _rms(y_ref[...], rest[0][...])


def _mlp(x2, g, w_up, w_down, layer, out_gain=None, *, tm=512, tf=1024):
    n, d = x2.shape
    d_ff = w_up.shape[2]
    vec = pl.BlockSpec((1, d), lambda i, f: (0, 0))
    in_specs = [
        pl.BlockSpec((tm, d), lambda i, f: (i, 0)),
        vec,
        pl.BlockSpec((None, d, tf), lambda i, f: (layer, 0, f)),
        pl.BlockSpec((None, tf, d), lambda i, f: (layer, f, 0)),
    ]
    args = [x2, g.reshape(1, d), w_up, w_down]
    if out_gain is not None:
        in_specs.append(vec)
        args.append(out_gain.reshape(1, d))
    return pl.pallas_call(
        functools.partial(_mlp_kernel, out_norm=out_gain is not None),
        out_shape=jax.ShapeDtypeStruct((n, d), F32),
        grid=(n // tm, d_ff // tf),
        in_specs=in_specs,
        out_specs=pl.BlockSpec((tm, d), lambda i, f: (i, 0)),
        scratch_shapes=[pltpu.VMEM((tm, d), BF16)],
        compiler_params=_cparams(("parallel", "arbitrary")),
        name="mlp",
    )(*args)


def _fill_q_ext(q_ref, qx_ref, row0=0, slot=()):
    lane = lax.broadcasted_iota(jnp.int32, (BLOCK, LANES), 1)
    row = lax.broadcasted_iota(jnp.int32, (BLOCK, LANES), 0)
    eye = jnp.where(lane == row, 1.0, 0.0).astype(BF16)
    low = lane < HEAD_DIM
    for kv in range(N_KV_HEADS):
        for g in range(GROUP):
            h = kv * GROUP + g
            tile = q_ref[row0:row0 + BLOCK, (h // 2) * LANES:(h // 2 + 1) * LANES]
            keep = low if h % 2 == 0 else jnp.logical_not(low)
            rows = slice(g * BLOCK, (g + 1) * BLOCK)
            qx_ref[(*slot, kv, rows, slice(0, LANES))] = jnp.where(keep, tile, jnp.zeros_like(tile))
            qx_ref[(*slot, kv, rows, slice(LANES, EXT))] = eye


def _scores_t(k_tile, bias, qx):
    k_ext = jnp.concatenate([k_tile, bias], axis=1)
    return lax.dot_general(k_ext, qx, (((1,), (1,)), ((), ())), preferred_element_type=F32)


def _vt_ext(vt):
    return jnp.concatenate([vt, jnp.ones((BF16_ROWS, vt.shape[1]), BF16)], axis=0)


def _flash_chunks(c_lo, c_end, tc, make_bias, k_ref, vt_ref, refs):
    qx_ref, s_ref, cm_ref, m_ref, acc_ref, bias_ref = refs
    m_ref[...] = jnp.full(m_ref.shape, NEG_INF, F32)
    acc_ref[...] = jnp.zeros(acc_ref.shape, F32)
    c_last = c_end - 1

    def put_bias(c):
        bias_ref[c & 1] = make_bias(c)

    def issue_scores(c, kv):
        start = pl.multiple_of(c * tc, tc)
        k_tile = k_ref[pl.ds(start, tc), kv * LANES:(kv + 1) * LANES]
        s_t = _scores_t(k_tile, bias_ref[c & 1], qx_ref[kv])
        s_ref[kv % 2] = s_t
        cm_ref[kv % 2] = jnp.max(s_t, axis=0, keepdims=True)

    def consume(c, kv):
        slot = kv % 2
        start = pl.multiple_of(c * tc, tc)
        vt = _vt_ext(vt_ref[kv * HEAD_DIM:(kv + 1) * HEAD_DIM, pl.ds(start, tc)])
        m_old = m_ref[kv]
        m_new = jnp.maximum(m_old, cm_ref[slot])
        alpha = jnp.exp2(m_old - m_new)
        m_ref[kv] = m_new
        for n in range(QROWS // MXU_TILE):
            cols = slice(n * MXU_TILE, (n + 1) * MXU_TILE)
            part = alpha[:, cols] * acc_ref[kv, :, cols]
            for kk in range(tc // MXU_TILE):
                rows = slice(kk * MXU_TILE, (kk + 1) * MXU_TILE)
                p = jnp.exp2(s_ref[slot, rows, cols] - m_new[:, cols])
                part = part + jnp.dot(vt[:, rows], p.astype(BF16), preferred_element_type=F32)
            acc_ref[kv, :, cols] = part

    put_bias(c_lo)
    issue_scores(c_lo, 0)

    def body(c, carry):
        c_next = jnp.minimum(c + 1, c_last)
        put_bias(c_next)
        for kv in range(N_KV_HEADS):
            if kv + 1 < N_KV_HEADS:
                issue_scores(c, kv + 1)
            else:
                issue_scores(c_next, 0)
            consume(c, kv)
        return carry

    lax.fori_loop(c_lo, c_end, body, 0)


def _store_out(o_ref, kv, o_t, row0=0):
    for gp in range(GROUP // 2):
        pair = jnp.concatenate([o_t[:, (2 * gp) * BLOCK:(2 * gp + 1) * BLOCK],
                                o_t[:, (2 * gp + 1) * BLOCK:(2 * gp + 2) * BLOCK]], axis=0)
        col = (kv * GROUP + 2 * gp) * HEAD_DIM
        o_ref[row0:row0 + BLOCK, col:col + LANES] = pair.T.astype(o_ref.dtype)


def _finish_flash(o_ref, acc_ref):
    for kv in range(N_KV_HEADS):
        acc = acc_ref[kv]
        _store_out(o_ref, kv, acc[0:HEAD_DIM, :] / acc[HEAD_DIM:HEAD_DIM + 1, :])


def _bias_of(valid):
    return jnp.where(valid, 0.0, NEG_INF).astype(BF16)


def _qx_scratch():
    return pltpu.VMEM((N_KV_HEADS, QROWS, EXT), BF16)


def _flash_scratch(tc):
    return [
        _qx_scratch(),
        pltpu.VMEM((2, tc, QROWS), F32),
        pltpu.VMEM((2, 1, QROWS), F32),
        pltpu.VMEM((N_KV_HEADS, 1, QROWS), F32),
        pltpu.VMEM((N_KV_HEADS, VT_ROWS, QROWS), F32),
        pltpu.VMEM((2, tc, BLOCK), BF16),
    ]


WIN_TILES = 4


def _window_kernel(*refs, span, tiles, with_sinks, emit_lse):
    refs = list(refs)
    sink_ref = refs.pop(0) if with_sinks else None
    q_ref, kp_ref, kc_ref, vp_ref, vc_ref, o_ref = refs[:6]
    lse_ref = refs[6] if emit_lse else None
    qx_ref, sw_ref = refs[-2:]
    first = pl.program_id(2) == 0
    key = lax.broadcasted_iota(jnp.int32, (2 * BLOCK, BLOCK), 0)
    qry = lax.broadcasted_iota(jnp.int32, (2 * BLOCK, BLOCK), 1)
    dist = qry + BLOCK - key
    band = (dist >= 0) & (dist < span)
    bias_any = _bias_of(band)
    bias_first = _bias_of(band & ((key >= BLOCK) | jnp.logical_not(first)))

    def v_t(block):
        return block.astype(F32).T.astype(BF16)

    ones = jnp.ones((BF16_ROWS, 2 * BLOCK), BF16)
    vts = [v_t(vp_ref[...])] + [v_t(vc_ref[j * BLOCK:(j + 1) * BLOCK, :]) for j in range(tiles)]

    def issue_scores(j):
        _fill_q_ext(q_ref, qx_ref, j * BLOCK, (j,))
        for kv in range(N_KV_HEADS):
            lanes = slice(kv * LANES, (kv + 1) * LANES)
            k_prev = kp_ref[:, lanes] if j == 0 else kc_ref[(j - 1) * BLOCK:j * BLOCK, lanes]
            k_tile = jnp.concatenate([k_prev, kc_ref[j * BLOCK:(j + 1) * BLOCK, lanes]], axis=0)
            sw_ref[j, kv] = _scores_t(k_tile, bias_first if j == 0 else bias_any, qx_ref[j, kv])

    def consume(j):
        vt2 = jnp.concatenate([vts[j], vts[j + 1]], axis=1)
        lse_rows = []
        for kv in range(N_KV_HEADS):
            s_t = sw_ref[j, kv]
            m = jnp.max(s_t, axis=0, keepdims=True)
            if with_sinks:
                sink = jnp.concatenate(
                    [jnp.full((1, BLOCK), sink_ref[kv * GROUP + g] * LOG2E, F32) for g in range(GROUP)], axis=1)
                m = jnp.maximum(m, sink)
            p = jnp.exp2(s_t - m).astype(BF16)
            vt = jnp.concatenate([vt2[kv * HEAD_DIM:(kv + 1) * HEAD_DIM, :], ones], axis=0)
            acc = jnp.dot(vt, p, preferred_element_type=F32)
            den = acc[HEAD_DIM:HEAD_DIM + 1, :]
            if with_sinks:
                den = den + jnp.exp2(sink - m)
            _store_out(o_ref, kv, acc[0:HEAD_DIM, :] / den, j * BLOCK)
            if emit_lse:
                lse = m + jnp.log2(den)
                lse_rows += [lse[:, g * BLOCK:(g + 1) * BLOCK] for g in range(GROUP)]
        if emit_lse:
            pad = jnp.zeros((BLOCK - N_HEADS, BLOCK), F32)
            lse_ref[j * BLOCK:(j + 1) * BLOCK, :] = jnp.concatenate(lse_rows + [pad], axis=0).T

    issue_scores(0)
    for j in range(tiles):
        if j + 1 < tiles:
            issue_scores(j + 1)
        consume(j)


RELAYOUT_ROWS = 512


def _to_residue_kernel(x_ref, o_ref, stage_ref, *, dil):
    groups = x_ref.shape[1] // LANES
    for g in range(groups):
        stage_ref[g] = x_ref[:, g * LANES:(g + 1) * LANES].astype(F32)
    for r in range(dil):
        for g in range(groups):
            rows = stage_ref[g, pl.ds(r, RELAYOUT_ROWS // dil, stride=dil), :]
            o_ref[r, :, g * LANES:(g + 1) * LANES] = rows.astype(o_ref.dtype)


def _relayout_specs(batch, seq, dil, w):
    steps = seq // RELAYOUT_ROWS
    token = pl.BlockSpec((RELAYOUT_ROWS, w), lambda b, u: (b * steps + u, 0))
    residue = pl.BlockSpec((None, dil, RELAYOUT_ROWS // dil, w), lambda b, u: (b, 0, u, 0))
    return (batch, steps), token, residue


def _by_residue(a, batch, seq, dil):
    w = a.shape[1]
    if dil == 1:
        return a.reshape(batch, 1, seq, w)
    grid, token, residue = _relayout_specs(batch, seq, dil, w)
    return pl.pallas_call(
        functools.partial(_to_residue_kernel, dil=dil),
        out_shape=jax.ShapeDtypeStruct((batch, dil, seq // dil, w), a.dtype),
        grid=grid, in_specs=[token], out_specs=residue,
        scratch_shapes=[pltpu.VMEM((w // LANES, RELAYOUT_ROWS, LANES), F32)],
        compiler_params=_cparams(("parallel", "parallel")),
        name="to_residue",
    )(a)


def _window_attention(q, k, v, sinks, batch, seq, *, dil, span, emit_lse):
    sub = seq // dil
    tiles = max(1, min(WIN_TILES, sub // BLOCK // 2))
    rows = tiles * BLOCK
    with_sinks = sinks is not None
    cur = lambda b, r, u: (b, r, u, 0)
    prev = lambda b, r, u: (b, r, jnp.maximum(u * tiles - 1, 0), 0)
    in_specs = [
        pl.BlockSpec((None, None, rows, ATTN_WIDTH), cur),
        pl.BlockSpec((None, None, BLOCK, KDUP_WIDTH), prev),
        pl.BlockSpec((None, None, rows, KDUP_WIDTH), cur),
        pl.BlockSpec((None, None, BLOCK, KV_WIDTH), prev),
        pl.BlockSpec((None, None, rows, KV_WIDTH), cur),
    ]
    qr, kr, vr = (_by_residue(a, batch, seq, dil) for a in (q, k, v))
    args = [qr, kr, kr, vr, vr]
    if with_sinks:
        in_specs.insert(0, pl.BlockSpec(memory_space=pltpu.SMEM))
        args.insert(0, sinks)
    out_shape = [jax.ShapeDtypeStruct((batch, dil, sub, ATTN_WIDTH), BF16)]
    out_specs = [pl.BlockSpec((None, None, rows, ATTN_WIDTH), cur)]
    if emit_lse:
        out_shape.append(jax.ShapeDtypeStruct((batch, dil, sub, LANES), F32))
        out_specs.append(pl.BlockSpec((None, None, rows, LANES), cur))
    res = pl.pallas_call(
        functools.partial(_window_kernel, span=span, tiles=tiles, with_sinks=with_sinks, emit_lse=emit_lse),
        out_shape=out_shape,
        grid=(batch, dil, sub // rows),
        in_specs=in_specs,
        out_specs=out_specs,
        scratch_shapes=[pltpu.VMEM((tiles, N_KV_HEADS, QROWS, EXT), BF16),
                        pltpu.VMEM((tiles, N_KV_HEADS, 2 * BLOCK, QROWS), F32)],
        compiler_params=_cparams(("parallel", "parallel", "arbitrary")),
        name="window_attention",
    )(*args)
    if dil == 1:
        res = [a.reshape(batch * seq, a.shape[-1]) for a in res]
    return res if emit_lse else res[0]


def _merge_kernel(*refs, dils):
    n_br = len(dils)
    o_refs, l_refs = refs[:n_br], refs[n_br:2 * n_br]
    e_ref, out_ref, ostage_ref, lstage_ref = refs[2 * n_br:]

    def in_token_order(ref, dil, stage_ref):
        if dil == 1:
            return ref[...].astype(F32)
        groups = ref.shape[-1] // LANES
        for r in range(dil):
            for g in range(groups):
                stage_ref[g, pl.ds(r, RELAYOUT_ROWS // dil, stride=dil), :] = (
                    ref[r, :, g * LANES:(g + 1) * LANES].astype(F32))
        return jnp.concatenate([stage_ref[g] for g in range(groups)], axis=1)

    lses = [in_token_order(l, d, lstage_ref) for l, d in zip(l_refs, dils)]
    top = functools.reduce(jnp.maximum, lses)
    ws = [jnp.exp2(l - top) for l in lses]
    tot = functools.reduce(lambda a, b: a + b, ws)
    out = None
    for w, o_ref, dil in zip(ws, o_refs, dils):
        wn = w / tot
        hi = wn.astype(BF16)
        lo = (wn - hi.astype(F32)).astype(BF16)
        spread = (jnp.dot(hi, e_ref[...], preferred_element_type=F32)
                  + jnp.dot(lo, e_ref[...], preferred_element_type=F32))
        term = spread * in_token_order(o_ref, dil, ostage_ref)
        out = term if out is None else out + term
    out_ref[...] = out.astype(out_ref.dtype)


def _merge_branches(outs, lses, dils, batch, seq):
    n = batch * seq
    tm = RELAYOUT_ROWS
    steps = seq // tm
    head_of_lane = jnp.arange(ATTN_WIDTH, dtype=jnp.int32) // HEAD_DIM
    expand = (jnp.arange(LANES, dtype=jnp.int32)[:, None] == head_of_lane[None, :]).astype(BF16)

    def spec(dil, width):
        if dil == 1:
            return pl.BlockSpec((tm, width), lambda i: (i, 0))
        return pl.BlockSpec((None, dil, tm // dil, width), lambda i: (i // steps, 0, i % steps, 0))

    return pl.pallas_call(
        functools.partial(_merge_kernel, dils=tuple(dils)),
        out_shape=jax.ShapeDtypeStruct((n, ATTN_WIDTH), BF16),
        grid=(n // tm,),
        in_specs=[spec(d, ATTN_WIDTH) for d in dils] + [spec(d, LANES) for d in dils]
                 + [pl.BlockSpec((LANES, ATTN_WIDTH), lambda i: (0, 0))],
        out_specs=pl.BlockSpec((tm, ATTN_WIDTH), lambda i: (i, 0)),
        scratch_shapes=[pltpu.VMEM((ATTN_WIDTH // LANES, tm, LANES), F32),
                        pltpu.VMEM((1, tm, LANES), F32)],
        compiler_params=_cparams(("parallel",)),
        name="merge_branches",
    )(*outs, *lses, expand)


def _dilated_attention(q, k, v, batch, seq):
    outs, lses, dils = [], [], []
    for window, dil in DILATED_BRANCHES:
        o, lse = _window_attention(q, k, v, None, batch, seq, dil=dil, span=window // dil + 1, emit_lse=True)
        outs.append(o)
        lses.append(lse)
        dils.append(dil)
    return _merge_branches(outs, lses, dils, batch, seq)


IDX_CHUNK = 512
ATT_CHUNK = 512
INT_MIN = -2 ** 31
F32_BITS = 32
BITS_PER_CHECK = 4


def _sortable_to_f32(t):
    bits = jnp.where(t >= 0, t, t ^ jnp.int32(0x7FFFFFFF))
    return lax.bitcast_convert_type(bits, F32)


def _dsa_kernel(q_ref, qi_ref, wi_ref, ki_ref, k_ref, vt_ref, o_ref, sc_ref, *refs, topk):
    qx_ref, acc_ref = refs[0], refs[4]
    i = pl.program_id(1)
    t0 = i * BLOCK
    n_idx = (t0 + BLOCK + IDX_CHUNK - 1) // IDX_CHUNK
    n_att = (t0 + BLOCK + ATT_CHUNK - 1) // ATT_CHUNK

    qis = jnp.concatenate(
        [qi_ref[:, h * IDX_DIM:(h + 1) * IDX_DIM] for h in range(IDX_HEADS)], axis=0)
    w_t = (wi_ref[...] * IDX_W_SCALE).T
    key = lax.broadcasted_iota(jnp.int32, (IDX_CHUNK, BLOCK), 0)
    qry = lax.broadcasted_iota(jnp.int32, (IDX_CHUNK, BLOCK), 1)

    def idx_body(c, carry):
        start = pl.multiple_of(c * IDX_CHUNK, IDX_CHUNK)
        kic = ki_ref[pl.ds(start, IDX_CHUNK), 0:IDX_DIM]
        rel = jnp.maximum(
            lax.dot_general(kic, qis, (((1,), (1,)), ((), ())), preferred_element_type=F32), 0.0)
        score = jnp.zeros((IDX_CHUNK, BLOCK), F32)
        for h in range(IDX_HEADS):
            score = score + rel[:, h * BLOCK:(h + 1) * BLOCK] * w_t[h:h + 1, :]
        sc_ref[pl.ds(start, IDX_CHUNK), :] = jnp.where(start + key <= t0 + qry, score, NEG_INF)
        return carry

    lax.fori_loop(0, n_idx, idx_body, 0)

    def count_hits(hit_fn):
        def cbody(c, cnt):
            start = pl.multiple_of(c * IDX_CHUNK, IDX_CHUNK)
            hit = hit_fn(start, sc_ref[pl.ds(start, IDX_CHUNK), :])
            parts = [hit[r * SUBLANES:(r + 1) * SUBLANES, :] for r in range(IDX_CHUNK // SUBLANES)]
            while len(parts) > 1:
                parts = [a + b for a, b in zip(parts[0::2], parts[1::2])]
            return cnt + parts[0]
        cnt = lax.fori_loop(0, n_idx, cbody, jnp.zeros((SUBLANES, BLOCK), F32))
        return jnp.sum(cnt, axis=0, keepdims=True)

    def count_ge(cand_f):
        return count_hits(lambda start, blk: jnp.where(blk >= cand_f, 1.0, 0.0))

    def bit_cond(state):
        b, _, _, n_open = state
        return (b < F32_BITS) & (n_open > 0)

    def bit_body(state):
        b0, t, done, _ = state
        for k in range(BITS_PER_CHECK):
            b = b0 + k
            bit = lax.shift_left(jnp.int32(1), F32_BITS - 1 - b)
            cand = jnp.where(b == 0, jnp.zeros_like(t), t | bit)
            cnt = count_ge(_sortable_to_f32(cand))
            take = (cnt >= float(topk)) & (done == 0)
            t = jnp.where(take, cand, t)
            done = jnp.where(take & (cnt == float(topk)), 1, done)
        n_open = jnp.sum(1 - done)
        return b0 + BITS_PER_CHECK, t, done, n_open

    state = (jnp.int32(0), jnp.full((1, BLOCK), INT_MIN, jnp.int32),
             jnp.zeros((1, BLOCK), jnp.int32), jnp.int32(BLOCK))
    _, t_int, _, n_open = lax.while_loop(bit_cond, bit_body, state)
    thr = jnp.maximum(_sortable_to_f32(t_int), jnp.float32(NEG_INF * 0.5))

    def tie_bound():
        need = float(topk) - count_hits(lambda start, blk: jnp.where(blk > thr, 1.0, 0.0))

        def jbody(b, lo):
            cand = lo + lax.shift_left(jnp.int32(1), idx_bits - 1 - b)
            kept = count_hits(lambda start, blk: jnp.where(
                blk == thr, jnp.where(start + key <= cand, 1.0, 0.0), 0.0))
            return jnp.where(kept < need, cand, lo)

        return lax.fori_loop(0, idx_bits, jbody, jnp.full((1, BLOCK), -1, jnp.int32)) + 1

    idx_bits = int(sc_ref.shape[0]).bit_length()
    last_tie = lax.cond(n_open > 0, tie_bound, lambda: jnp.full((1, BLOCK), sc_ref.shape[0], jnp.int32))

    _fill_q_ext(q_ref, qx_ref)

    def make_bias(c):
        start = pl.multiple_of(c * ATT_CHUNK, ATT_CHUNK)
        sc = sc_ref[pl.ds(start, ATT_CHUNK), :]
        tied = jnp.where(start + key <= last_tie, 0.0, NEG_INF)
        return jnp.where(sc > thr, 0.0, jnp.where(sc == thr, tied, NEG_INF)).astype(BF16)

    _flash_chunks(0, n_att, ATT_CHUNK, make_bias, k_ref, vt_ref, refs)
    _finish_flash(o_ref, acc_ref)


DSA_QI_OFF = ATTN_WIDTH
DSA_K_OFF = DSA_QI_OFF + IDX_HEADS * IDX_DIM
DSA_KI_OFF = DSA_K_OFF + KDUP_WIDTH
DSA_WI_OFF = DSA_KI_OFF + LANES
DSA_V_OFF = DSA_WI_OFF + LANES
DSA_WIDTH = DSA_V_OFF + KV_WIDTH
DSA_TN = 2048


def _dsa_attention(proj, wi, vt, batch, seq):
    nb = seq // BLOCK
    n = batch * seq
    topk = min(TOPK_MAX, seq // 4)
    qi_w = IDX_HEADS * IDX_DIM
    chunk = max(IDX_CHUNK, ATT_CHUNK)
    seq_pad = -(-seq // chunk) * chunk
    return pl.pallas_call(
        functools.partial(_dsa_kernel, topk=topk),
        out_shape=jax.ShapeDtypeStruct((n, ATTN_WIDTH), BF16),
        grid=(batch, nb),
        in_specs=[
            pl.BlockSpec((BLOCK, ATTN_WIDTH), lambda b, i: (b * nb + i, 0)),
            pl.BlockSpec((BLOCK, qi_w), lambda b, i: (b * nb + i, DSA_QI_OFF // qi_w)),
            pl.BlockSpec((BLOCK, LANES), lambda b, i: (b * nb + i, 0)),
            pl.BlockSpec((seq, LANES), lambda b, i: (b, DSA_KI_OFF // LANES)),
            pl.BlockSpec((seq, KDUP_WIDTH), lambda b, i: (b, DSA_K_OFF // KDUP_WIDTH)),
            pl.BlockSpec((None, KV_WIDTH, seq), lambda b, i: (b, 0, 0)),
        ],
        out_specs=pl.BlockSpec((BLOCK, ATTN_WIDTH), lambda b, i: (b * nb + i, 0)),
        scratch_shapes=[pltpu.VMEM((seq_pad, BLOCK), F32)] + _flash_scratch(ATT_CHUNK),
        compiler_params=_cparams(("parallel", "arbitrary")),
        name="dsa_attention",
    )(proj, proj, wi, proj, proj, vt)


def _rope_table(positions):
    inv = ROPE_THETA ** (-jnp.arange(0, ROPE_DIM, 2, dtype=F32) / ROPE_DIM)
    ang = positions.astype(F32).reshape(-1, 1) * inv[None, :]
    cos, sin = jnp.cos(ang), jnp.sin(ang)
    n = ang.shape[0]
    pad = HEAD_DIM - ROPE_DIM
    cos_h = jnp.concatenate([cos, cos, jnp.ones((n, pad), F32)], axis=1)
    lo_h = jnp.concatenate([-sin, jnp.zeros((n, HEAD_DIM - ROPE_HALF), F32)], axis=1)
    hi_h = jnp.concatenate([jnp.zeros((n, ROPE_HALF), F32), sin, jnp.zeros((n, pad), F32)], axis=1)
    reps = LANES // HEAD_DIM
    return jnp.concatenate([jnp.tile(cos_h, (1, reps)), jnp.tile(lo_h, (1, reps)),
                            jnp.tile(hi_h, (1, reps))], axis=1)


def _dup_heads(wk):
    d = wk.shape[0]
    w4 = wk.reshape(d, N_KV_HEADS, 1, HEAD_DIM)
    return jnp.broadcast_to(w4, (d, N_KV_HEADS, LANES // HEAD_DIM, HEAD_DIM)).reshape(d, KDUP_WIDTH)


QKV_WIDTH_EXT = ATTN_WIDTH + KDUP_WIDTH + KV_WIDTH


def _qkv_weight(w_in):
    o = ATTN_WIDTH
    return jnp.concatenate([w_in[:, :o], _dup_heads(w_in[:, o:o + KV_WIDTH]),
                            w_in[:, o + KV_WIDTH:o + 2 * KV_WIDTH]], axis=1)


def _qkv_colscale():
    return jnp.concatenate([jnp.full((1, ATTN_WIDTH), SCALE * LOG2E, F32),
                            jnp.ones((1, KDUP_WIDTH + KV_WIDTH), F32)], axis=1)


def _dsa_weight(w_in):
    d = w_in.shape[0]
    o = ATTN_WIDTH
    wq = w_in[:, :o]
    wk = w_in[:, o:o + KV_WIDTH]
    wv = w_in[:, o + KV_WIDTH:o + 2 * KV_WIDTH]
    o += 2 * KV_WIDTH
    wqi = w_in[:, o:o + IDX_HEADS * IDX_DIM]
    o += IDX_HEADS * IDX_DIM
    wki = w_in[:, o:o + IDX_DIM]
    o += IDX_DIM
    wwi = w_in[:, o:o + IDX_HEADS]
    z = lambda c: jnp.zeros((d, c), w_in.dtype)
    return jnp.concatenate([wq, wqi, _dup_heads(wk), wki, z(LANES - IDX_DIM),
                            wwi, z(LANES - IDX_HEADS), wv], axis=1)


def _dsa_colscale():
    return jnp.concatenate([jnp.full((1, ATTN_WIDTH), SCALE * LOG2E, F32),
                            jnp.full((1, IDX_HEADS * IDX_DIM), IDX_SCALE, F32),
                            jnp.ones((1, DSA_WIDTH - DSA_K_OFF), F32)], axis=1)


def _v_transposed(proj, v_off, batch, seq):
    v = proj[:, v_off:v_off + KV_WIDTH].reshape(batch, seq, KV_WIDTH)
    return jnp.swapaxes(v, 1, 2)


def kernel(x, positions, norm_attn, norm_mlp, w_up, w_down, final_norm,
           a_w_in, a_sinks, a_w_out, b_w_in, b_w_out, c_w_in, c_w_out):
    batch, seq, d = x.shape
    depth = norm_attn.shape[0]
    x2 = x.reshape(batch * seq, d)
    rope_tab = _rope_table(positions)
    w_up_bf, w_down_bf = w_up.astype(BF16), w_down.astype(BF16)
    qkv_scale = _qkv_colscale()
    qkv_rope_groups = (ATTN_WIDTH + KDUP_WIDTH) // LANES
    for i in range(depth):
        j, kind = divmod(i, 3)
        if kind == 1:
            proj, wi = _norm_proj(x2, norm_attn[i], _dsa_weight(b_w_in[j]).astype(BF16),
                                  _dsa_colscale(), rope_tab, tn=DSA_TN,
                                  n_rope_groups=DSA_WI_OFF // LANES,
                                  aux_group=(DSA_WI_OFF % DSA_TN) // LANES)
            o = _dsa_attention(proj, wi, _v_transposed(proj, DSA_V_OFF, batch, seq), batch, seq)
            w_out = b_w_out[j]
        else:
            w_in = a_w_in[j] if kind == 0 else c_w_in[j]
            q, k, v = _norm_proj(x2, norm_attn[i], _qkv_weight(w_in).astype(BF16), qkv_scale, rope_tab,
                                 tn=QKV_WIDTH_EXT, n_rope_groups=qkv_rope_groups,
                                 out_widths=(ATTN_WIDTH, KDUP_WIDTH, KV_WIDTH))
            if kind == 0:
                o = _window_attention(q, k, v, a_sinks[j], batch, seq, dil=1, span=SWA_WINDOW, emit_lse=False)
                w_out = a_w_out[j]
            else:
                o = _dilated_attention(q, k, v, batch, seq)
                w_out = c_w_out[j]
        x2 = _out_proj(o, w_out.astype(BF16), x2)
        x2 = _mlp(x2, norm_mlp[i], w_up_bf, w_down_bf, i, final_norm if i == depth - 1 else None)
    return x2.reshape(batch, seq, d)
```

```python
import functools

import jax
import jax.numpy as jnp
from jax import lax
from jax.experimental import pallas as pl
from jax.experimental.pallas import tpu as pltpu

HEAD_DIM = 64
N_KV_HEADS = 4
GROUP = 8
N_HEADS = N_KV_HEADS * GROUP
ATTN_WIDTH = N_HEADS * HEAD_DIM
KV_WIDTH = N_KV_HEADS * HEAD_DIM
ROPE_DIM = HEAD_DIM // 4
ROPE_HALF = ROPE_DIM // 2
ROPE_THETA = 500000.0
SCALE = HEAD_DIM ** -0.5
BLOCK = 128
SWA_WINDOW = 128
IDX_HEADS = 16
IDX_DIM = 64
IDX_SCALE = IDX_DIM ** -0.5
IDX_W_SCALE = IDX_HEADS ** -0.5
TOPK_MAX = 256
DILATED_BRANCHES = ((128, 1), (512, 4), (2048, 16))
NORM_EPS = 1e-5
NEG_INF = -1e30

LANES = 128
BF16_ROWS = 16
SUBLANES = 8
MXU_TILE = 256
LOG2E = 1.4426950408889634
VMEM_LIMIT = 52 * 1024 * 1024

BF16 = jnp.bfloat16
F32 = jnp.float32

KDUP_WIDTH = N_KV_HEADS * LANES
QROWS = GROUP * BLOCK
EXT = 2 * LANES
VT_ROWS = HEAD_DIM + BF16_ROWS


def _cparams(sem):
    return pltpu.CompilerParams(dimension_semantics=sem, vmem_limit_bytes=VMEM_LIMIT)


def _norm_proj_kernel(x_ref, g_ref, w_ref, cs_ref, rope_ref, *rest,
                      n_rope_groups, groups_per_tile, out_groups, aux_group):
    o_refs = rest[:len(out_groups)]
    aux_ref = rest[len(out_groups)] if aux_group is not None else None
    h_ref = rest[-1]
    j = pl.program_id(1)

    @pl.when(j == 0)
    def _():
        x = x_ref[...]
        ms = jnp.mean(x * x, axis=-1, keepdims=True)
        h_ref[...] = ((x * lax.rsqrt(ms + NORM_EPS)) * g_ref[...]).astype(BF16)

    cos_t = rope_ref[:, 0:LANES]
    sin_lo = rope_ref[:, LANES:2 * LANES]
    sin_hi = rope_ref[:, 2 * LANES:3 * LANES]
    h = h_ref[...]
    sub_groups = MXU_TILE // LANES
    n_sub = groups_per_tile // sub_groups
    dest = [(o_ref, k) for o_ref, cnt in zip(o_refs, out_groups) for k in range(cnt)]

    def project(s):
        cols = slice(s * MXU_TILE, (s + 1) * MXU_TILE)
        return jnp.dot(h, w_ref[:, cols], preferred_element_type=F32) * cs_ref[:, cols]

    def finish(s, acc):
        for gg in range(sub_groups):
            g = s * sub_groups + gg
            a = acc[:, gg * LANES:(gg + 1) * LANES]
            r = (a * cos_t + pltpu.roll(a, LANES - ROPE_HALF, 1) * sin_lo
                 + pltpu.roll(a, ROPE_HALF, 1) * sin_hi)
            is_rope = (j * groups_per_tile + g) < n_rope_groups
            o_ref, k = dest[g]
            o_ref[:, k * LANES:(k + 1) * LANES] = jnp.where(is_rope, r, a).astype(o_ref.dtype)
            if aux_ref is not None and g == aux_group:
                aux_ref[...] = a

    acc = project(0)
    for s in range(1, n_sub):
        nxt = project(s)
        finish(s - 1, acc)
        acc = nxt
    finish(n_sub - 1, acc)


def _norm_proj(x2, g, w, colscale, rope_tab, *, tn, n_rope_groups, aux_group=None, out_widths=None, tm=512):
    n, d = x2.shape
    width = w.shape[1]
    gpt = tn // LANES
    if out_widths is None:
        out_widths = (tn,)
        out_shape = [jax.ShapeDtypeStruct((n, width), BF16)]
        out_specs = [pl.BlockSpec((tm, tn), lambda i, j: (i, j))]
    else:
        assert tn == width == sum(out_widths)
        out_shape = [jax.ShapeDtypeStruct((n, ow), BF16) for ow in out_widths]
        out_specs = [pl.BlockSpec((tm, ow), lambda i, j: (i, 0)) for ow in out_widths]
    kern = functools.partial(_norm_proj_kernel, n_rope_groups=n_rope_groups, groups_per_tile=gpt,
                             out_groups=tuple(ow // LANES for ow in out_widths), aux_group=aux_group)
    if aux_group is not None:
        out_shape.append(jax.ShapeDtypeStruct((n, LANES), F32))
        out_specs.append(pl.BlockSpec((tm, LANES), lambda i, j: (i, 0)))
    res = pl.pallas_call(
        kern,
        out_shape=out_shape,
        grid=(n // tm, width // tn),
        in_specs=[
            pl.BlockSpec((tm, d), lambda i, j: (i, 0)),
            pl.BlockSpec((1, d), lambda i, j: (0, 0)),
            pl.BlockSpec((d, tn), lambda i, j: (0, j)),
            pl.BlockSpec((1, tn), lambda i, j: (0, j)),
            pl.BlockSpec((tm, 3 * LANES), lambda i, j: (i, 0)),
        ],
        out_specs=out_specs,
        scratch_shapes=[pltpu.VMEM((tm, d), BF16)],
        compiler_params=_cparams(("parallel", "arbitrary")),
        name="norm_proj",
    )(x2, g.reshape(1, d), w, colscale, rope_tab)
    return res


def _out_proj_kernel(o_ref, w_ref, x_ref, y_ref):
    y_ref[...] = x_ref[...] + jnp.dot(o_ref[...], w_ref[...], preferred_element_type=F32)


def _out_proj(o, w, x2, *, tm=512, tn=2048):
    n, k = o.shape
    d = w.shape[1]
    return pl.pallas_call(
        _out_proj_kernel,
        out_shape=jax.ShapeDtypeStruct((n, d), F32),
        grid=(n // tm, d // tn),
        in_specs=[
            pl.BlockSpec((tm, k), lambda i, j: (i, 0)),
            pl.BlockSpec((k, tn), lambda i, j: (0, j)),
            pl.BlockSpec((tm, tn), lambda i, j: (i, j)),
        ],
        out_specs=pl.BlockSpec((tm, tn), lambda i, j: (i, j)),
        compiler_params=_cparams(("parallel", "arbitrary")),
        name="out_proj",
    )(o, w, x2)


def _rms(x, g):
    ms = jnp.mean(x * x, axis=-1, keepdims=True)
    return (x * lax.rsqrt(ms + NORM_EPS)) * g


def _mlp_kernel(x_ref, g_ref, wu_ref, wd_ref, *rest, out_norm):
    y_ref, h_ref = rest[-2:]
    f = pl.program_id(1)

    @pl.when(f == 0)
    def _():
        x = x_ref[...]
        h_ref[...] = _rms(x, g_ref[...]).astype(BF16)
        y_ref[...] = x

    u = jnp.dot(h_ref[...], wu_ref[...], preferred_element_type=F32)
    u = jnp.maximum(u, 0.0)
    a = (u * u).astype(BF16)
    y_ref[...] += jnp.dot(a, wd_ref[...], preferred_element_type=F32)

    if out_norm:
        @pl.when(f == pl.num_programs(1) - 1)
        def _():
            y_ref[...] = _rms(y_ref[...], rest[0][...])


def _mlp(x2, g, w_up, w_down, layer, out_gain=None, *, tm=512, tf=1024):
    n, d = x2.shape
    d_ff = w_up.shape[2]
    vec = pl.BlockSpec((1, d), lambda i, f: (0, 0))
    in_specs = [
        pl.BlockSpec((tm, d), lambda i, f: (i, 0)),
        vec,
        pl.BlockSpec((None, d, tf), lambda i, f: (layer, 0, f)),
        pl.BlockSpec((None, tf, d), lambda i, f: (layer, f, 0)),
    ]
    args = [x2, g.reshape(1, d), w_up, w_down]
    if out_gain is not None:
        in_specs.append(vec)
        args.append(out_gain.reshape(1, d))
    return pl.pallas_call(
        functools.partial(_mlp_kernel, out_norm=out_gain is not None),
        out_shape=jax.ShapeDtypeStruct((n, d), F32),
        grid=(n // tm, d_ff // tf),
        in_specs=in_specs,
        out_specs=pl.BlockSpec((tm, d), lambda i, f: (i, 0)),
        scratch_shapes=[pltpu.VMEM((tm, d), BF16)],
        compiler_params=_cparams(("parallel", "arbitrary")),
        name="mlp",
    )(*args)


def _fill_q_ext(q_ref, qx_ref, row0=0, slot=()):
    lane = lax.broadcasted_iota(jnp.int32, (BLOCK, LANES), 1)
    row = lax.broadcasted_iota(jnp.int32, (BLOCK, LANES), 0)
    eye = jnp.where(lane == row, 1.0, 0.0).astype(BF16)
    low = lane < HEAD_DIM
    for kv in range(N_KV_HEADS):
        for g in range(GROUP):
            h = kv * GROUP + g
            tile = q_ref[row0:row0 + BLOCK, (h // 2) * LANES:(h // 2 + 1) * LANES]
            keep = low if h % 2 == 0 else jnp.logical_not(low)
            rows = slice(g * BLOCK, (g + 1) * BLOCK)
            qx_ref[(*slot, kv, rows, slice(0, LANES))] = jnp.where(keep, tile, jnp.zeros_like(tile))
            qx_ref[(*slot, kv, rows, slice(LANES, EXT))] = eye


def _scores_t(k_tile, bias, qx):
    k_ext = jnp.concatenate([k_tile, bias], axis=1)
    return lax.dot_general(k_ext, qx, (((1,), (1,)), ((), ())), preferred_element_type=F32)


def _vt_ext(vt):
    return jnp.concatenate([vt, jnp.ones((BF16_ROWS, vt.shape[1]), BF16)], axis=0)


def _flash_chunks(c_lo, c_end, tc, make_bias, k_ref, vt_ref, refs):
    qx_ref, s_ref, cm_ref, m_ref, acc_ref, bias_ref = refs
    m_ref[...] = jnp.full(m_ref.shape, NEG_INF, F32)
    acc_ref[...] = jnp.zeros(acc_ref.shape, F32)
    c_last = c_end - 1

    def put_bias(c):
        bias_ref[c & 1] = make_bias(c)

    def issue_scores(c, kv):
        start = pl.multiple_of(c * tc, tc)
        k_tile = k_ref[pl.ds(start, tc), kv * LANES:(kv + 1) * LANES]
        s_t = _scores_t(k_tile, bias_ref[c & 1], qx_ref[kv])
        s_ref[kv % 2] = s_t
        cm_ref[kv % 2] = jnp.max(s_t, axis=0, keepdims=True)

    def consume(c, kv):
        slot = kv % 2
        start = pl.multiple_of(c * tc, tc)
        vt = _vt_ext(vt_ref[kv * HEAD_DIM:(kv + 1) * HEAD_DIM, pl.ds(start, tc)])
        m_old = m_ref[kv]
        m_new = jnp.maximum(m_old, cm_ref[slot])
        alpha = jnp.exp2(m_old - m_new)
        m_ref[kv] = m_new
        for n in range(QROWS // MXU_TILE):
            cols = slice(n * MXU_TILE, (n + 1) * MXU_TILE)
            part = alpha[:, cols] * acc_ref[kv, :, cols]
            for kk in range(tc // MXU_TILE):
                rows = slice(kk * MXU_TILE, (kk + 1) * MXU_TILE)
                p = jnp.exp2(s_ref[slot, rows, cols] - m_new[:, cols])
                part = part + jnp.dot(vt[:, rows], p.astype(BF16), preferred_element_type=F32)
            acc_ref[kv, :, cols] = part

    put_bias(c_lo)
    issue_scores(c_lo, 0)

    def body(c, carry):
        c_next = jnp.minimum(c + 1, c_last)
        put_bias(c_next)
        for kv in range(N_KV_HEADS):
            if kv + 1 < N_KV_HEADS:
                issue_scores(c, kv + 1)
            else:
                issue_scores(c_next, 0)
            consume(c, kv)
        return carry

    lax.fori_loop(c_lo, c_end, body, 0)


def _store_out(o_ref, kv, o_t, row0=0):
    for gp in range(GROUP // 2):
        pair = jnp.concatenate([o_t[:, (2 * gp) * BLOCK:(2 * gp + 1) * BLOCK],
                                o_t[:, (2 * gp + 1) * BLOCK:(2 * gp + 2) * BLOCK]], axis=0)
        col = (kv * GROUP + 2 * gp) * HEAD_DIM
        o_ref[row0:row0 + BLOCK, col:col + LANES] = pair.T.astype(o_ref.dtype)


def _finish_flash(o_ref, acc_ref):
    for kv in range(N_KV_HEADS):
        acc = acc_ref[kv]
        _store_out(o_ref, kv, acc[0:HEAD_DIM, :] / acc[HEAD_DIM:HEAD_DIM + 1, :])


def _bias_of(valid):
    return jnp.where(valid, 0.0, NEG_INF).astype(BF16)


def _qx_scratch():
    return pltpu.VMEM((N_KV_HEADS, QROWS, EXT), BF16)


def _flash_scratch(tc):
    return [
        _qx_scratch(),
        pltpu.VMEM((2, tc, QROWS), F32),
        pltpu.VMEM((2, 1, QROWS), F32),
        pltpu.VMEM((N_KV_HEADS, 1, QROWS), F32),
        pltpu.VMEM((N_KV_HEADS, VT_ROWS, QROWS), F32),
        pltpu.VMEM((2, tc, BLOCK), BF16),
    ]


WIN_TILES = 4


def _window_kernel(*refs, span, tiles, with_sinks, emit_lse):
    refs = list(refs)
    sink_ref = refs.pop(0) if with_sinks else None
    q_ref, kp_ref, kc_ref, vp_ref, vc_ref, o_ref = refs[:6]
    lse_ref = refs[6] if emit_lse else None
    qx_ref, sw_ref = refs[-2:]
    first = pl.program_id(2) == 0
    key = lax.broadcasted_iota(jnp.int32, (2 * BLOCK, BLOCK), 0)
    qry = lax.broadcasted_iota(jnp.int32, (2 * BLOCK, BLOCK), 1)
    dist = qry + BLOCK - key
    band = (dist >= 0) & (dist < span)
    bias_any = _bias_of(band)
    bias_first = _bias_of(band & ((key >= BLOCK) | jnp.logical_not(first)))

    def v_t(block):
        return block.astype(F32).T.astype(BF16)

    ones = jnp.ones((BF16_ROWS, 2 * BLOCK), BF16)
    vts = [v_t(vp_ref[...])] + [v_t(vc_ref[j * BLOCK:(j + 1) * BLOCK, :]) for j in range(tiles)]

    def issue_scores(j):
        _fill_q_ext(q_ref, qx_ref, j * BLOCK, (j,))
        for kv in range(N_KV_HEADS):
            lanes = slice(kv * LANES, (kv + 1) * LANES)
            k_prev = kp_ref[:, lanes] if j == 0 else kc_ref[(j - 1) * BLOCK:j * BLOCK, lanes]
            k_tile = jnp.concatenate([k_prev, kc_ref[j * BLOCK:(j + 1) * BLOCK, lanes]], axis=0)
            sw_ref[j, kv] = _scores_t(k_tile, bias_first if j == 0 else bias_any, qx_ref[j, kv])

    def consume(j):
        vt2 = jnp.concatenate([vts[j], vts[j + 1]], axis=1)
        lse_rows = []
        for kv in range(N_KV_HEADS):
            s_t = sw_ref[j, kv]
            m = jnp.max(s_t, axis=0, keepdims=True)
            if with_sinks:
                sink = jnp.concatenate(
                    [jnp.full((1, BLOCK), sink_ref[kv * GROUP + g] * LOG2E, F32) for g in range(GROUP)], axis=1)
                m = jnp.maximum(m, sink)
            p = jnp.exp2(s_t - m).astype(BF16)
            vt = jnp.concatenate([vt2[kv * HEAD_DIM:(kv + 1) * HEAD_DIM, :], ones], axis=0)
            acc = jnp.dot(vt, p, preferred_element_type=F32)
            den = acc[HEAD_DIM:HEAD_DIM + 1, :]
            if with_sinks:
                den = den + jnp.exp2(sink - m)
            _store_out(o_ref, kv, acc[0:HEAD_DIM, :] / den, j * BLOCK)
            if emit_lse:
                lse = m + jnp.log2(den)
                lse_rows += [lse[:, g * BLOCK:(g + 1) * BLOCK] for g in range(GROUP)]
        if emit_lse:
            pad = jnp.zeros((BLOCK - N_HEADS, BLOCK), F32)
            lse_ref[j * BLOCK:(j + 1) * BLOCK, :] = jnp.concatenate(lse_rows + [pad], axis=0).T

    issue_scores(0)
    for j in range(tiles):
        if j + 1 < tiles:
            issue_scores(j + 1)
        consume(j)


RELAYOUT_ROWS = 512


def _to_residue_kernel(x_ref, o_ref, stage_ref, *, dil):
    groups = x_ref.shape[1] // LANES
    for g in range(groups):
        stage_ref[g] = x_ref[:, g * LANES:(g + 1) * LANES].astype(F32)
    for r in range(dil):
        for g in range(groups):
            rows = stage_ref[g, pl.ds(r, RELAYOUT_ROWS // dil, stride=dil), :]
            o_ref[r, :, g * LANES:(g + 1) * LANES] = rows.astype(o_ref.dtype)


def _relayout_specs(batch, seq, dil, w):
    steps = seq // RELAYOUT_ROWS
    token = pl.BlockSpec((RELAYOUT_ROWS, w), lambda b, u: (b * steps + u, 0))
    residue = pl.BlockSpec((None, dil, RELAYOUT_ROWS // dil, w), lambda b, u: (b, 0, u, 0))
    return (batch, steps), token, residue


def _by_residue(a, batch, seq, dil):
    w = a.shape[1]
    if dil == 1:
        return a.reshape(batch, 1, seq, w)
    grid, token, residue = _relayout_specs(batch, seq, dil, w)
    return pl.pallas_call(
        functools.partial(_to_residue_kernel, dil=dil),
        out_shape=jax.ShapeDtypeStruct((batch, dil, seq // dil, w), a.dtype),
        grid=grid, in_specs=[token], out_specs=residue,
        scratch_shapes=[pltpu.VMEM((w // LANES, RELAYOUT_ROWS, LANES), F32)],
        compiler_params=_cparams(("parallel", "parallel")),
        name="to_residue",
    )(a)


def _window_attention(q, k, v, sinks, batch, seq, *, dil, span, emit_lse):
    sub = seq // dil
    tiles = max(1, min(WIN_TILES, sub // BLOCK // 2))
    rows = tiles * BLOCK
    with_sinks = sinks is not None
    cur = lambda b, r, u: (b, r, u, 0)
    prev = lambda b, r, u: (b, r, jnp.maximum(u * tiles - 1, 0), 0)
    in_specs = [
        pl.BlockSpec((None, None, rows, ATTN_WIDTH), cur),
        pl.BlockSpec((None, None, BLOCK, KDUP_WIDTH), prev),
        pl.BlockSpec((None, None, rows, KDUP_WIDTH), cur),
        pl.BlockSpec((None, None, BLOCK, KV_WIDTH), prev),
        pl.BlockSpec((None, None, rows, KV_WIDTH), cur),
    ]
    qr, kr, vr = (_by_residue(a, batch, seq, dil) for a in (q, k, v))
    args = [qr, kr, kr, vr, vr]
    if with_sinks:
        in_specs.insert(0, pl.BlockSpec(memory_space=pltpu.SMEM))
        args.insert(0, sinks)
    out_shape = [jax.ShapeDtypeStruct((batch, dil, sub, ATTN_WIDTH), BF16)]
    out_specs = [pl.BlockSpec((None, None, rows, ATTN_WIDTH), cur)]
    if emit_lse:
        out_shape.append(jax.ShapeDtypeStruct((batch, dil, sub, LANES), F32))
        out_specs.append(pl.BlockSpec((None, None, rows, LANES), cur))
    res = pl.pallas_call(
        functools.partial(_window_kernel, span=span, tiles=tiles, with_sinks=with_sinks, emit_lse=emit_lse),
        out_shape=out_shape,
        grid=(batch, dil, sub // rows),
        in_specs=in_specs,
        out_specs=out_specs,
        scratch_shapes=[pltpu.VMEM((tiles, N_KV_HEADS, QROWS, EXT), BF16),
                        pltpu.VMEM((tiles, N_KV_HEADS, 2 * BLOCK, QROWS), F32)],
        compiler_params=_cparams(("parallel", "parallel", "arbitrary")),
        name="window_attention",
    )(*args)
    if dil == 1:
        res = [a.reshape(batch * seq, a.shape[-1]) for a in res]
    return res if emit_lse else res[0]


def _merge_kernel(*refs, dils):
    n_br = len(dils)
    o_refs, l_refs = refs[:n_br], refs[n_br:2 * n_br]
    e_ref, out_ref, ostage_ref, lstage_ref = refs[2 * n_br:]

    def in_token_order(ref, dil, stage_ref):
        if dil == 1:
            return ref[...].astype(F32)
        groups = ref.shape[-1] // LANES
        for r in range(dil):
            for g in range(groups):
                stage_ref[g, pl.ds(r, RELAYOUT_ROWS // dil, stride=dil), :] = (
                    ref[r, :, g * LANES:(g + 1) * LANES].astype(F32))
        return jnp.concatenate([stage_ref[g] for g in range(groups)], axis=1)

    lses = [in_token_order(l, d, lstage_ref) for l, d in zip(l_refs, dils)]
    top = functools.reduce(jnp.maximum, lses)
    ws = [jnp.exp2(l - top) for l in lses]
    tot = functools.reduce(lambda a, b: a + b, ws)
    out = None
    for w, o_ref, dil in zip(ws, o_refs, dils):
        wn = w / tot
        hi = wn.astype(BF16)
        lo = (wn - hi.astype(F32)).astype(BF16)
        spread = jnp.dot(jnp.concatenate([hi, lo], axis=1), e_ref[...], preferred_element_type=F32)
        term = spread * in_token_order(o_ref, dil, ostage_ref)
        out = term if out is None else out + term
    out_ref[...] = out.astype(out_ref.dtype)


def _merge_branches(outs, lses, dils, batch, seq):
    n = batch * seq
    tm = RELAYOUT_ROWS
    steps = seq // tm
    head_of_lane = jnp.arange(ATTN_WIDTH, dtype=jnp.int32) // HEAD_DIM
    expand = (jnp.arange(LANES, dtype=jnp.int32)[:, None] == head_of_lane[None, :]).astype(BF16)
    expand = jnp.concatenate([expand, expand], axis=0)

    def spec(dil, width):
        if dil == 1:
            return pl.BlockSpec((tm, width), lambda i: (i, 0))
        return pl.BlockSpec((None, dil, tm // dil, width), lambda i: (i // steps, 0, i % steps, 0))

    return pl.pallas_call(
        functools.partial(_merge_kernel, dils=tuple(dils)),
        out_shape=jax.ShapeDtypeStruct((n, ATTN_WIDTH), BF16),
        grid=(n // tm,),
        in_specs=[spec(d, ATTN_WIDTH) for d in dils] + [spec(d, LANES) for d in dils]
                 + [pl.BlockSpec((2 * LANES, ATTN_WIDTH), lambda i: (0, 0))],
        out_specs=pl.BlockSpec((tm, ATTN_WIDTH), lambda i: (i, 0)),
        scratch_shapes=[pltpu.VMEM((ATTN_WIDTH // LANES, tm, LANES), F32),
                        pltpu.VMEM((1, tm, LANES), F32)],
        compiler_params=_cparams(("parallel",)),
        name="merge_branches",
    )(*outs, *lses, expand)


def _dilated_attention(q, k, v, batch, seq):
    outs, lses, dils = [], [], []
    for window, dil in DILATED_BRANCHES:
        o, lse = _window_attention(q, k, v, None, batch, seq, dil=dil, span=window // dil + 1, emit_lse=True)
        outs.append(o)
        lses.append(lse)
        dils.append(dil)
    return _merge_branches(outs, lses, dils, batch, seq)


IDX_CHUNK = 512
ATT_CHUNK = 512
INT_MIN = -2 ** 31
F32_BITS = 32
BITS_PER_CHECK = 4


def _sortable_to_f32(t):
    bits = jnp.where(t >= 0, t, t ^ jnp.int32(0x7FFFFFFF))
    return lax.bitcast_convert_type(bits, F32)


def _dsa_kernel(q_ref, qi_ref, wi_ref, ki_ref, k_ref, vt_ref, o_ref, sc_ref, *refs, topk):
    qx_ref, acc_ref = refs[0], refs[4]
    i = pl.program_id(1)
    t0 = i * BLOCK
    n_idx = (t0 + BLOCK + IDX_CHUNK - 1) // IDX_CHUNK
    n_att = (t0 + BLOCK + ATT_CHUNK - 1) // ATT_CHUNK

    qis = jnp.concatenate(
        [qi_ref[:, h * IDX_DIM:(h + 1) * IDX_DIM] for h in range(IDX_HEADS)], axis=0)
    w_t = (wi_ref[...] * IDX_W_SCALE).T
    key = lax.broadcasted_iota(jnp.int32, (IDX_CHUNK, BLOCK), 0)
    qry = lax.broadcasted_iota(jnp.int32, (IDX_CHUNK, BLOCK), 1)

    def idx_body(c, carry):
        start = pl.multiple_of(c * IDX_CHUNK, IDX_CHUNK)
        kic = ki_ref[pl.ds(start, IDX_CHUNK), 0:IDX_DIM]
        rel = jnp.maximum(
            lax.dot_general(kic, qis, (((1,), (1,)), ((), ())), preferred_element_type=F32), 0.0)
        score = jnp.zeros((IDX_CHUNK, BLOCK), F32)
        for h in range(IDX_HEADS):
            score = score + rel[:, h * BLOCK:(h + 1) * BLOCK] * w_t[h:h + 1, :]
        sc_ref[pl.ds(start, IDX_CHUNK), :] = jnp.where(start + key <= t0 + qry, score, NEG_INF)
        return carry

    lax.fori_loop(0, n_idx, idx_body, 0)

    def count_hits(hit_fn):
        def cbody(c, cnt):
            start = pl.multiple_of(c * IDX_CHUNK, IDX_CHUNK)
            hit = hit_fn(start, sc_ref[pl.ds(start, IDX_CHUNK), :])
            parts = [hit[r * SUBLANES:(r + 1) * SUBLANES, :] for r in range(IDX_CHUNK // SUBLANES)]
            while len(parts) > 1:
                parts = [a + b for a, b in zip(parts[0::2], parts[1::2])]
            return cnt + parts[0]
        cnt = lax.fori_loop(0, n_idx, cbody, jnp.zeros((SUBLANES, BLOCK), F32))
        return jnp.sum(cnt, axis=0, keepdims=True)

    def count_ge(cand_f):
        return count_hits(lambda start, blk: jnp.where(blk >= cand_f, 1.0, 0.0))

    def bit_cond(state):
        b, _, _, n_open = state
        return (b < F32_BITS) & (n_open > 0)

    def bit_body(state):
        b0, t, done, _ = state
        for k in range(BITS_PER_CHECK):
            b = b0 + k
            bit = lax.shift_left(jnp.int32(1), F32_BITS - 1 - b)
            cand = jnp.where(b == 0, jnp.zeros_like(t), t | bit)
            cnt = count_ge(_sortable_to_f32(cand))
            take = (cnt >= float(topk)) & (done == 0)
            t = jnp.where(take, cand, t)
            done = jnp.where(take & (cnt == float(topk)), 1, done)
        n_open = jnp.sum(1 - done)
        return b0 + BITS_PER_CHECK, t, done, n_open

    state = (jnp.int32(0), jnp.full((1, BLOCK), INT_MIN, jnp.int32),
             jnp.zeros((1, BLOCK), jnp.int32), jnp.int32(BLOCK))
    _, t_int, _, n_open = lax.while_loop(bit_cond, bit_body, state)
    thr = jnp.maximum(_sortable_to_f32(t_int), jnp.float32(NEG_INF * 0.5))

    def tie_bound():
        need = float(topk) - count_hits(lambda start, blk: jnp.where(blk > thr, 1.0, 0.0))

        def jbody(b, lo):
            cand = lo + lax.shift_left(jnp.int32(1), idx_bits - 1 - b)
            kept = count_hits(lambda start, blk: jnp.where(
                blk == thr, jnp.where(start + key <= cand, 1.0, 0.0), 0.0))
            return jnp.where(kept < need, cand, lo)

        return lax.fori_loop(0, idx_bits, jbody, jnp.full((1, BLOCK), -1, jnp.int32)) + 1

    idx_bits = int(sc_ref.shape[0]).bit_length()
    last_tie = lax.cond(n_open > 0, tie_bound, lambda: jnp.full((1, BLOCK), sc_ref.shape[0], jnp.int32))

    _fill_q_ext(q_ref, qx_ref)

    def make_bias(c):
        start = pl.multiple_of(c * ATT_CHUNK, ATT_CHUNK)
        sc = sc_ref[pl.ds(start, ATT_CHUNK), :]
        tied = jnp.where(start + key <= last_tie, 0.0, NEG_INF)
        return jnp.where(sc > thr, 0.0, jnp.where(sc == thr, tied, NEG_INF)).astype(BF16)

    _flash_chunks(0, n_att, ATT_CHUNK, make_bias, k_ref, vt_ref, refs)
    _finish_flash(o_ref, acc_ref)


DSA_QI_OFF = ATTN_WIDTH
DSA_K_OFF = DSA_QI_OFF + IDX_HEADS * IDX_DIM
DSA_KI_OFF = DSA_K_OFF + KDUP_WIDTH
DSA_WI_OFF = DSA_KI_OFF + LANES
DSA_V_OFF = DSA_WI_OFF + LANES
DSA_WIDTH = DSA_V_OFF + KV_WIDTH
DSA_TN = 2048


def _dsa_attention(proj, wi, vt, batch, seq):
    nb = seq // BLOCK
    n = batch * seq
    topk = min(TOPK_MAX, seq // 4)
    qi_w = IDX_HEADS * IDX_DIM
    chunk = max(IDX_CHUNK, ATT_CHUNK)
    seq_pad = -(-seq // chunk) * chunk
    return pl.pallas_call(
        functools.partial(_dsa_kernel, topk=topk),
        out_shape=jax.ShapeDtypeStruct((n, ATTN_WIDTH), BF16),
        grid=(batch, nb),
        in_specs=[
            pl.BlockSpec((BLOCK, ATTN_WIDTH), lambda b, i: (b * nb + i, 0)),
            pl.BlockSpec((BLOCK, qi_w), lambda b, i: (b * nb + i, DSA_QI_OFF // qi_w)),
            pl.BlockSpec((BLOCK, LANES), lambda b, i: (b * nb + i, 0)),
            pl.BlockSpec((seq, LANES), lambda b, i: (b, DSA_KI_OFF // LANES)),
            pl.BlockSpec((seq, KDUP_WIDTH), lambda b, i: (b, DSA_K_OFF // KDUP_WIDTH)),
            pl.BlockSpec((None, KV_WIDTH, seq), lambda b, i: (b, 0, 0)),
        ],
        out_specs=pl.BlockSpec((BLOCK, ATTN_WIDTH), lambda b, i: (b * nb + i, 0)),
        scratch_shapes=[pltpu.VMEM((seq_pad, BLOCK), F32)] + _flash_scratch(ATT_CHUNK),
        compiler_params=_cparams(("parallel", "arbitrary")),
        name="dsa_attention",
    )(proj, proj, wi, proj, proj, vt)


def _rope_table(positions):
    inv = ROPE_THETA ** (-jnp.arange(0, ROPE_DIM, 2, dtype=F32) / ROPE_DIM)
    ang = positions.astype(F32).reshape(-1, 1) * inv[None, :]
    cos, sin = jnp.cos(ang), jnp.sin(ang)
    n = ang.shape[0]
    pad = HEAD_DIM - ROPE_DIM
    cos_h = jnp.concatenate([cos, cos, jnp.ones((n, pad), F32)], axis=1)
    lo_h = jnp.concatenate([-sin, jnp.zeros((n, HEAD_DIM - ROPE_HALF), F32)], axis=1)
    hi_h = jnp.concatenate([jnp.zeros((n, ROPE_HALF), F32), sin, jnp.zeros((n, pad), F32)], axis=1)
    reps = LANES // HEAD_DIM
    return jnp.concatenate([jnp.tile(cos_h, (1, reps)), jnp.tile(lo_h, (1, reps)),
                            jnp.tile(hi_h, (1, reps))], axis=1)


def _dup_heads(wk):
    d = wk.shape[0]
    w4 = wk.reshape(d, N_KV_HEADS, 1, HEAD_DIM)
    return jnp.broadcast_to(w4, (d, N_KV_HEADS, LANES // HEAD_DIM, HEAD_DIM)).reshape(d, KDUP_WIDTH)


QKV_WIDTH_EXT = ATTN_WIDTH + KDUP_WIDTH + KV_WIDTH


def _qkv_weight(w_in):
    o = ATTN_WIDTH
    return jnp.concatenate([w_in[:, :o], _dup_heads(w_in[:, o:o + KV_WIDTH]),
                            w_in[:, o + KV_WIDTH:o + 2 * KV_WIDTH]], axis=1)


def _qkv_colscale():
    return jnp.concatenate([jnp.full((1, ATTN_WIDTH), SCALE * LOG2E, F32),
                            jnp.ones((1, KDUP_WIDTH + KV_WIDTH), F32)], axis=1)


def _dsa_weight(w_in):
    d = w_in.shape[0]
    o = ATTN_WIDTH
    wq = w_in[:, :o]
    wk = w_in[:, o:o + KV_WIDTH]
    wv = w_in[:, o + KV_WIDTH:o + 2 * KV_WIDTH]
    o += 2 * KV_WIDTH
    wqi = w_in[:, o:o + IDX_HEADS * IDX_DIM]
    o += IDX_HEADS * IDX_DIM
    wki = w_in[:, o:o + IDX_DIM]
    o += IDX_DIM
    wwi = w_in[:, o:o + IDX_HEADS]
    z = lambda c: jnp.zeros((d, c), w_in.dtype)
    return jnp.concatenate([wq, wqi, _dup_heads(wk), wki, z(LANES - IDX_DIM),
                            wwi, z(LANES - IDX_HEADS), wv], axis=1)


def _dsa_colscale():
    return jnp.concatenate([jnp.full((1, ATTN_WIDTH), SCALE * LOG2E, F32),
                            jnp.full((1, IDX_HEADS * IDX_DIM), IDX_SCALE, F32),
                            jnp.ones((1, DSA_WIDTH - DSA_K_OFF), F32)], axis=1)


def _v_transposed(proj, v_off, batch, seq):
    v = proj[:, v_off:v_off + KV_WIDTH].reshape(batch, seq, KV_WIDTH)
    return jnp.swapaxes(v, 1, 2)


def kernel(x, positions, norm_attn, norm_mlp, w_up, w_down, final_norm,
           a_w_in, a_sinks, a_w_out, b_w_in, b_w_out, c_w_in, c_w_out):
    batch, seq, d = x.shape
    depth = norm_attn.shape[0]
    x2 = x.reshape(batch * seq, d)
    rope_tab = _rope_table(positions)
    w_up_bf, w_down_bf = w_up.astype(BF16), w_down.astype(BF16)
    qkv_scale = _qkv_colscale()
    qkv_rope_groups = (ATTN_WIDTH + KDUP_WIDTH) // LANES
    for i in range(depth):
        j, kind = divmod(i, 3)
        if kind == 1:
            proj, wi = _norm_proj(x2, norm_attn[i], _dsa_weight(b_w_in[j]).astype(BF16),
                                  _dsa_colscale(), rope_tab, tn=DSA_TN,
                                  n_rope_groups=DSA_WI_OFF // LANES,
                                  aux_group=(DSA_WI_OFF % DSA_TN) // LANES)
            o = _dsa_attention(proj, wi, _v_transposed(proj, DSA_V_OFF, batch, seq), batch, seq)
            w_out = b_w_out[j]
        else:
            w_in = a_w_in[j] if kind == 0 else c_w_in[j]
            q, k, v = _norm_proj(x2, norm_attn[i], _qkv_weight(w_in).astype(BF16), qkv_scale, rope_tab,
                                 tn=QKV_WIDTH_EXT, n_rope_groups=qkv_rope_groups,
                                 out_widths=(ATTN_WIDTH, KDUP_WIDTH, KV_WIDTH))
            if kind == 0:
                o = _window_attention(q, k, v, a_sinks[j], batch, seq, dil=1, span=SWA_WINDOW, emit_lse=False)
                w_out = a_w_out[j]
            else:
                o = _dilated_attention(q, k, v, batch, seq)
                w_out = c_w_out[j]
        x2 = _out_proj(o, w_out.astype(BF16), x2)
        x2 = _mlp(x2, norm_mlp[i], w_up_bf, w_down_bf, i, final_norm if i == depth - 1 else None)
    return x2.reshape(batch, seq, d)
```
